```python
import math
import jax, jax.numpy as jnp
from jax import lax
import numpy as np

D_MODEL = 1024
BATCH = 8
SEQ = 2048
DEPTH = 2
DEC_BATCH = 128
DEC_SEQ = 1
PAST_LEN = 16384
PAGE_SIZE = 128

N_MIXERS = 2
N_A_LAYERS = (DEPTH + 1) // 2
N_B_LAYERS = DEPTH // 2
D_CONV = D_MODEL
CONV_A_W = 3
D_RNN = D_MODEL
N_RG_BLOCKS = 8
RG_BLOCK = D_RNN // N_RG_BLOCKS
CONV_B_W = 4
RG_C = 8.0
N_GROUPS = 4
EXP_PER_GROUP = 8
N_EXPERTS = N_GROUPS * EXP_PER_GROUP
TOP_K = 2
D_EXPERT = 512
MOE_BLOCK = 128
ALPHA = (2.0 * DEPTH) ** 0.25
BETA = (8.0 * DEPTH) ** -0.25
LN_EPS = 1e-5

kernel_name = "hybrid_shortconv_rglru_hmoe_step"


def layer_norm(x, g, b):
    xf = x.astype(jnp.float32)
    mu = jnp.mean(xf, axis=-1, keepdims=True)
    var = jnp.mean(jnp.square(xf - mu), axis=-1, keepdims=True)
    y = (xf - mu) * lax.rsqrt(var + LN_EPS)
    return (y * g.astype(jnp.float32) + b.astype(jnp.float32)).astype(x.dtype)


def causal_dwconv(u, buf, w):
    width = w.shape[0]
    s = u.shape[1]
    full = jnp.concatenate([buf.astype(u.dtype), u], axis=1)
    out = full[:, 0:s] * w[0]
    for k in range(1, width):
        out = out + full[:, k:k + s] * w[k]
    return out, full[:, s:]


def short_conv_mixer(x, buf, w_in, conv_w, w_out):
    bcx = jnp.einsum('bsd,de->bse', x, w_in)
    gb, gc, xh = jnp.split(bcx, 3, axis=-1)
    conv, new_buf = causal_dwconv(gc * xh, buf, conv_w)
    y = jnp.einsum('bsc,cd->bsd', gb * conv, w_out)
    return y, new_buf


def block_diag(x, w, bias):
    b, s, _ = x.shape
    xb = x.reshape(b, s, N_RG_BLOCKS, RG_BLOCK)
    y = jnp.einsum('bsnk,nkj->bsnj', xb, w).reshape(b, s, D_RNN)
    return y + bias


def rglru(x, h0, ga_w, ga_b, gx_w, gx_b, lam):
    r = jax.nn.sigmoid(block_diag(x, ga_w, ga_b).astype(jnp.float32))
    i = jax.nn.sigmoid(block_diag(x, gx_w, gx_b).astype(jnp.float32))
    log_a = -RG_C * r * jax.nn.softplus(-lam.astype(jnp.float32))
    a = jnp.exp(log_a)
    mult = jnp.sqrt(-jnp.expm1(2.0 * log_a))
    bterm = mult * (i * x.astype(jnp.float32))
    bterm = bterm.at[:, 0].add(a[:, 0] * h0.astype(jnp.float32))

    def combine(left, right):
        a1, b1 = left
        a2, b2 = right
        return a1 * a2, a2 * b1 + b2

    _, h = lax.associative_scan(combine, (a, bterm), axis=1)
    return h.astype(x.dtype), h[:, -1].astype(h0.dtype)


def rglru_mixer(x, conv_buf, h0, w_in, conv_w, conv_b, ga_w, ga_b, gx_w, gx_b, lam, w_out):
    gx = jnp.einsum('bsd,de->bse', x, w_in)
    gate, xr = jnp.split(gx, 2, axis=-1)
    xc, new_buf = causal_dwconv(xr, conv_buf, conv_w)
    xc = xc + conv_b
    h, h_last = rglru(xc, h0, ga_w, ga_b, gx_w, gx_b, lam)
    y = jnp.einsum('bsc,cd->bsd', jax.nn.gelu(gate, approximate=True) * h, w_out)
    return y, new_buf, h_last


def hier_moe(x, w_group, w_expert, w_gate, w_up, w_down):
    bsz, s, d = x.shape
    t = bsz * s
    x2 = x.reshape(t, d)
    gp = jax.nn.softmax((x2 @ w_group).astype(jnp.float32), axis=-1)
    g = jnp.argmax(gp, axis=-1).astype(jnp.int32)
    gw = jnp.max(gp, axis=-1)
    el = (x2 @ w_expert).astype(jnp.float32).reshape(t, N_GROUPS, EXP_PER_GROUP)
    el = jnp.take_along_axis(el, g[:, None, None], axis=1)[:, 0]
    ep = jax.nn.softmax(el, axis=-1)
    vals, idx = lax.top_k(ep, TOP_K)
    vals = vals / jnp.sum(vals, axis=-1, keepdims=True)
    wts = gw[:, None] * vals
    eid = (g[:, None] * EXP_PER_GROUP + idx).reshape(-1).astype(jnp.int32)
    tok = jnp.repeat(jnp.arange(t, dtype=jnp.int32), TOP_K)
    wa = wts.reshape(-1)
    n_assign = t * TOP_K
    order = jnp.argsort(eid)
    eid_s, tok_s, w_s = eid[order], tok[order], wa[order]
    counts = jnp.zeros((N_EXPERTS,), jnp.int32).at[eid].add(1)
    starts = jnp.cumsum(counts) - counts
    pcounts = (counts + MOE_BLOCK - 1) // MOE_BLOCK * MOE_BLOCK
    pend = jnp.cumsum(pcounts)
    pstarts = pend - pcounts
    dest = pstarts[eid_s] + (jnp.arange(n_assign, dtype=jnp.int32) - starts[eid_s])
    n_blk = (n_assign + N_EXPERTS * (MOE_BLOCK - 1) + MOE_BLOCK - 1) // MOE_BLOCK
    p_rows = n_blk * MOE_BLOCK
    row_tok = jnp.full((p_rows,), t, jnp.int32).at[dest].set(tok_s)
    x_pad = jnp.concatenate([x2, jnp.zeros((1, d), x2.dtype)], axis=0)
    xs = x_pad[row_tok].reshape(n_blk, MOE_BLOCK, d)
    blk_e = jnp.clip(jnp.searchsorted(pend, jnp.arange(n_blk, dtype=jnp.int32) * MOE_BLOCK, side='right'),
                     0, N_EXPERTS - 1).astype(jnp.int32)

    def expert_block(args):
        xb, e = args
        hmid = jax.nn.silu(xb @ w_gate[e]) * (xb @ w_up[e])
        return hmid @ w_down[e]

    out = lax.map(expert_block, (xs, blk_e)).reshape(p_rows, d)
    y_rows = out[dest] * w_s[:, None].astype(out.dtype)
    y = jax.ops.segment_sum(y_rows, tok_s, num_segments=t)
    return y.reshape(bsz, s, d)


def setup_inputs(seed: int = 0) -> dict:
    key = jax.random.key(seed)
    ks = iter(jax.random.split(key, 40))
    f32 = jnp.float32

    def nrm(shape, scale):
        return jax.random.normal(next(ks), shape, f32) * scale

    x_prompt = nrm((BATCH, SEQ, D_MODEL), 1.0)
    x_sample = nrm((DEC_BATCH, DEC_SEQ, D_MODEL), 1.0)
    state_conv_a = nrm((N_A_LAYERS, DEC_BATCH, CONV_A_W - 1, D_CONV), 1.0)
    state_conv_b = nrm((N_B_LAYERS, DEC_BATCH, CONV_B_W - 1, D_RNN), 1.0)
    state_h = nrm((N_B_LAYERS, DEC_BATCH, D_RNN), 0.5)
    a_w_in = nrm((N_A_LAYERS, D_MODEL, 3 * D_CONV), D_MODEL ** -0.5)
    a_conv_w = nrm((N_A_LAYERS, CONV_A_W, D_CONV), CONV_A_W ** -0.5)
    a_w_out = nrm((N_A_LAYERS, D_CONV, D_MODEL), D_CONV ** -0.5 * BETA)
    b_w_in = nrm((N_B_LAYERS, D_MODEL, 2 * D_RNN), D_MODEL ** -0.5)
    b_conv_w = nrm((N_B_LAYERS, CONV_B_W, D_RNN), CONV_B_W ** -0.5)
    b_conv_b = nrm((N_B_LAYERS, D_RNN), 0.01)
    b_gate_a_w = nrm((N_B_LAYERS, N_RG_BLOCKS, RG_BLOCK, RG_BLOCK), RG_BLOCK ** -0.5)
    b_gate_a_b = nrm((N_B_LAYERS, D_RNN), 0.01)
    b_gate_x_w = nrm((N_B_LAYERS, N_RG_BLOCKS, RG_BLOCK, RG_BLOCK), RG_BLOCK ** -0.5)
    b_gate_x_b = nrm((N_B_LAYERS, D_RNN), 0.01)
    a_c = jax.random.uniform(next(ks), (N_B_LAYERS, D_RNN), f32, 0.9, 0.999)
    a_base = a_c ** (1.0 / RG_C)
    b_lambda = jnp.log(a_base) - jnp.log1p(-a_base)
    b_w_out = nrm((N_B_LAYERS, D_RNN, D_MODEL), D_RNN ** -0.5 * BETA)
    ln1_g = 1.0 + nrm((DEPTH, D_MODEL), 0.02)
    ln1_b = nrm((DEPTH, D_MODEL), 0.02)
    ln2_g = 1.0 + nrm((DEPTH, D_MODEL), 0.02)
    ln2_b = nrm((DEPTH, D_MODEL), 0.02)
    moe_w_group = nrm((DEPTH, D_MODEL, N_GROUPS), D_MODEL ** -0.5)
    moe_w_expert = nrm((DEPTH, D_MODEL, N_EXPERTS), D_MODEL ** -0.5)
    moe_w_gate = nrm((DEPTH, N_EXPERTS, D_MODEL, D_EXPERT), D_MODEL ** -0.5)
    moe_w_up = nrm((DEPTH, N_EXPERTS, D_MODEL, D_EXPERT), D_MODEL ** -0.5)
    moe_w_down = nrm((DEPTH, N_EXPERTS, D_EXPERT, D_MODEL), D_EXPERT ** -0.5 * BETA)
    return {
        "x_prompt": x_prompt, "x_sample": x_sample,
        "state_conv_a": state_conv_a, "state_conv_b": state_conv_b, "state_h": state_h,
        "a_w_in": a_w_in, "a_conv_w": a_conv_w, "a_w_out": a_w_out,
        "b_w_in": b_w_in, "b_conv_w": b_conv_w, "b_conv_b": b_conv_b,
        "b_gate_a_w": b_gate_a_w, "b_gate_a_b": b_gate_a_b,
        "b_gate_x_w": b_gate_x_w, "b_gate_x_b": b_gate_x_b,
        "b_lambda": b_lambda, "b_w_out": b_w_out,
        "ln1_g": ln1_g, "ln1_b": ln1_b, "ln2_g": ln2_g, "ln2_b": ln2_b,
        "moe_w_group": moe_w_group, "moe_w_expert": moe_w_expert,
        "moe_w_gate": moe_w_gate, "moe_w_up": moe_w_up, "moe_w_down": moe_w_down,
    }


def reference(x_prompt, x_sample, state_conv_a, state_conv_b, state_h,
              a_w_in, a_conv_w, a_w_out,
              b_w_in, b_conv_w, b_conv_b, b_gate_a_w, b_gate_a_b, b_gate_x_w, b_gate_x_b,
              b_lambda, b_w_out,
              ln1_g, ln1_b, ln2_g, ln2_b,
              moe_w_group, moe_w_expert, moe_w_gate, moe_w_up, moe_w_down):
    xp, xs = x_prompt, x_sample
    bp = x_prompt.shape[0]
    conv_a_p, conv_a_s, conv_b_p, conv_b_s, h_p, h_s = [], [], [], [], [], []
    for i in range(DEPTH):
        j = i // N_MIXERS
        if i % N_MIXERS == 0:
            buf0 = jnp.zeros((bp, CONV_A_W - 1, D_CONV), xp.dtype)
            yp, nbp = short_conv_mixer(xp, buf0, a_w_in[j], a_conv_w[j], a_w_out[j])
            ys, nbs = short_conv_mixer(xs, state_conv_a[j], a_w_in[j], a_conv_w[j], a_w_out[j])
            conv_a_p.append(nbp)
            conv_a_s.append(nbs)
        else:
            buf0 = jnp.zeros((bp, CONV_B_W - 1, D_RNN), xp.dtype)
            h0 = jnp.zeros((bp, D_RNN), state_h.dtype)
            yp, nbp, hlp = rglru_mixer(xp, buf0, h0, b_w_in[j], b_conv_w[j], b_conv_b[j],
                                      b_gate_a_w[j], b_gate_a_b[j], b_gate_x_w[j], b_gate_x_b[j],
                                      b_lambda[j], b_w_out[j])
            ys, nbs, hls = rglru_mixer(xs, state_conv_b[j], state_h[j], b_w_in[j], b_conv_w[j], b_conv_b[j],
                                      b_gate_a_w[j], b_gate_a_b[j], b_gate_x_w[j], b_gate_x_b[j],
                                      b_lambda[j], b_w_out[j])
            conv_b_p.append(nbp)
            conv_b_s.append(nbs)
            h_p.append(hlp)
            h_s.append(hls)
        xp = layer_norm(ALPHA * xp + yp, ln1_g[i], ln1_b[i])
        xs = layer_norm(ALPHA * xs + ys, ln1_g[i], ln1_b[i])
        mp = hier_moe(xp, moe_w_group[i], moe_w_expert[i], moe_w_gate[i], moe_w_up[i], moe_w_down[i])
        ms = hier_moe(xs, moe_w_group[i], moe_w_expert[i], moe_w_gate[i], moe_w_up[i], moe_w_down[i])
        xp = layer_norm(ALPHA * xp + mp, ln2_g[i], ln2_b[i])
        xs = layer_norm(ALPHA * xs + ms, ln2_g[i], ln2_b[i])
    new_conv_a_prompt = jnp.stack(conv_a_p, axis=0)
    new_conv_a_sample = jnp.stack(conv_a_s, axis=0)
    new_conv_b_prompt = jnp.stack(conv_b_p, axis=0)
    new_conv_b_sample = jnp.stack(conv_b_s, axis=0)
    new_h_prompt = jnp.stack(h_p, axis=0)
    new_h_sample = jnp.stack(h_s, axis=0)
    return (xp, xs, new_conv_a_prompt, new_conv_a_sample, new_conv_b_prompt, new_conv_b_sample,
            new_h_prompt, new_h_sample)
```

```python
import functools

import jax
import jax.numpy as jnp
from jax import lax
from jax.experimental import pallas as pl
from jax.experimental.pallas import tpu as pltpu

F32 = jnp.float32
BF16 = jnp.bfloat16
I32 = jnp.int32

DEPTH = 2
N_RG_BLOCKS = 8
RG_C = 8.0
N_GROUPS = 4
EXP_PER_GROUP = 8
N_EXPERTS = N_GROUPS * EXP_PER_GROUP
ALPHA = (2.0 * DEPTH) ** 0.25
LN_EPS = 1e-5

LANES = 128
SUBLANES = 8
VMEM_LIMIT = 56 * 1024 * 1024

TS_A = 512
TS_B = 256
TT_ROUTE = 1024
TD = 128
TC = 256
BLK = 256
ROUTE_ROWS = 128
GROUP_ROW0 = N_EXPERTS


def _dot(a, b):
    return jnp.dot(a, b, preferred_element_type=F32)


def _layer_norm(r, g, b):
    mu = jnp.mean(r, axis=-1, keepdims=True)
    d = r - mu
    var = jnp.mean(d * d, axis=-1, keepdims=True)
    return d * lax.rsqrt(var + LN_EPS) * g + b


def _shift_rows(v, k, prev8):
    rolled = pltpu.roll(v, k, axis=0)
    rows8 = lax.broadcasted_iota(I32, (SUBLANES, v.shape[1]), 0)
    first = jnp.where(rows8 < k, pltpu.roll(prev8, k, axis=0), rolled[0:SUBLANES])
    return jnp.concatenate([first, rolled[SUBLANES:]], axis=0)


def _softplus(v):
    return jnp.maximum(v, 0.0) + jnp.log1p(jnp.exp(-jnp.abs(v)))


def _rglru_coeffs(xc, wcat_ref, gab, gxb, lam):
    d = xc.shape[1]
    blk = d // N_RG_BLOCKS
    xcb = xc.astype(BF16)
    rs, is_ = [], []
    for n in range(N_RG_BLOCKS):
        o = _dot(xcb[:, n * blk:(n + 1) * blk], wcat_ref[n])
        rs.append(o[:, :blk])
        is_.append(o[:, blk:])
    r = jax.nn.sigmoid(jnp.concatenate(rs, axis=1) + gab)
    i = jax.nn.sigmoid(jnp.concatenate(is_, axis=1) + gxb)
    log_a = -RG_C * r * _softplus(-lam)
    a = jnp.exp(log_a)
    mult = jnp.sqrt(-jnp.tanh(log_a) * (a * a + 1.0))
    return a, mult * (i * xc)


def _scan_rows(a, b, h0):
    m = a.shape[0]
    sub = lax.broadcasted_iota(I32, a.shape, 0) % SUBLANES
    for k in (1, 2, 4):
        a_sh = pltpu.roll(a, k, axis=0)
        b_sh = pltpu.roll(b, k, axis=0)
        keep = sub >= k
        b = jnp.where(keep, a * b_sh + b, b)
        a = jnp.where(keep, a * a_sh, a)
    outs = []
    h = h0
    for g in range(m // SUBLANES):
        hg = a[g * SUBLANES:(g + 1) * SUBLANES] * h + b[g * SUBLANES:(g + 1) * SUBLANES]
        outs.append(hg)
        h = hg[SUBLANES - 1:SUBLANES]
    return jnp.concatenate(outs, axis=0), h


def _conv_a_prompt_body(x_ref, win_ref, cw_ref, wout_ref, g_ref, b_ref,
                        o_ref, buf_ref, carry):
    s = pl.program_id(1)

    @pl.when(s == 0)
    def _():
        carry[...] = jnp.zeros_like(carry)

    x = x_ref[0]
    d = x.shape[1]
    bcx = _dot(x.astype(BF16), win_ref[...])
    gb, gc, xh = bcx[:, :d], bcx[:, d:2 * d], bcx[:, 2 * d:]
    u = gc * xh
    prev = carry[...]
    cw = cw_ref[...]
    conv = (cw[0:1] * _shift_rows(u, 2, prev) + cw[1:2] * _shift_rows(u, 1, prev)
            + cw[2:3] * u)
    y = _dot((gb * conv).astype(BF16), wout_ref[...])
    o_ref[0] = _layer_norm(ALPHA * x + y, g_ref[...], b_ref[...])
    ts = u.shape[0]
    carry[...] = u[ts - SUBLANES:ts]

    @pl.when(s == pl.num_programs(1) - 1)
    def _():
        buf_ref[0] = u[ts - 2:ts]


def _conv_a_prompt(x, win, cw, wout, g, b):
    bsz, seq, d = x.shape
    ts = min(TS_A, seq)
    grid = (bsz, seq // ts)
    return pl.pallas_call(
        _conv_a_prompt_body,
        grid=grid,
        in_specs=[
            pl.BlockSpec((1, ts, d), lambda i, j: (i, j, 0)),
            pl.BlockSpec((d, 3 * d), lambda i, j: (0, 0)),
            pl.BlockSpec((3, d), lambda i, j: (0, 0)),
            pl.BlockSpec((d, d), lambda i, j: (0, 0)),
            pl.BlockSpec((1, d), lambda i, j: (0, 0)),
            pl.BlockSpec((1, d), lambda i, j: (0, 0)),
        ],
        out_specs=[
            pl.BlockSpec((1, ts, d), lambda i, j: (i, j, 0)),
            pl.BlockSpec((1, 2, d), lambda i, j: (i, 0, 0)),
        ],
        out_shape=[
            jax.ShapeDtypeStruct((bsz, seq, d), F32),
            jax.ShapeDtypeStruct((bsz, 2, d), F32),
        ],
        scratch_shapes=[pltpu.VMEM((SUBLANES, d), F32)],
        compiler_params=pltpu.CompilerParams(
            dimension_semantics=("arbitrary", "arbitrary"), vmem_limit_bytes=VMEM_LIMIT),
        name="conv_a_prompt",
    )(x, win, cw, wout, g, b)


def _conv_a_sample_body(x_ref, s0_ref, s1_ref, win_ref, cw_ref, wout_ref, g_ref, b_ref,
                        o_ref, u_ref):
    x = x_ref[...]
    d = x.shape[1]
    bcx = _dot(x.astype(BF16), win_ref[...])
    gb, gc, xh = bcx[:, :d], bcx[:, d:2 * d], bcx[:, 2 * d:]
    u = gc * xh
    cw = cw_ref[...]
    conv = cw[0:1] * s0_ref[...] + cw[1:2] * s1_ref[...] + cw[2:3] * u
    y = _dot((gb * conv).astype(BF16), wout_ref[...])
    o_ref[...] = _layer_norm(ALPHA * x + y, g_ref[...], b_ref[...])
    u_ref[...] = u


def _conv_a_sample(x, s0, s1, win, cw, wout, g, b):
    n, d = x.shape
    return pl.pallas_call(
        _conv_a_sample_body,
        out_shape=[jax.ShapeDtypeStruct((n, d), F32), jax.ShapeDtypeStruct((n, d), F32)],
        compiler_params=pltpu.CompilerParams(vmem_limit_bytes=VMEM_LIMIT),
        name="conv_a_sample",
    )(x, s0, s1, win, cw, wout, g, b)


def _rglru_prompt_body(x_ref, win_ref, cw_ref, cb_ref, wcat_ref, gab_ref, gxb_ref, lam_ref,
                       wout_ref, g_ref, b_ref, o_ref, buf_ref, hl_ref, xcarry, hcarry):
    s = pl.program_id(1)

    @pl.when(s == 0)
    def _():
        xcarry[...] = jnp.zeros_like(xcarry)
        hcarry[...] = jnp.zeros_like(hcarry)

    x = x_ref[0]
    d = x.shape[1]
    gx = _dot(x.astype(BF16), win_ref[...])
    gate, xr = gx[:, :d], gx[:, d:]
    prev = xcarry[...]
    cw = cw_ref[...]
    xc = (cw[0:1] * _shift_rows(xr, 3, prev) + cw[1:2] * _shift_rows(xr, 2, prev)
          + cw[2:3] * _shift_rows(xr, 1, prev) + cw[3:4] * xr) + cb_ref[...]
    a, bt = _rglru_coeffs(xc, wcat_ref, gab_ref[...], gxb_ref[...], lam_ref[...])
    hs, hlast = _scan_rows(a, bt, hcarry[0:1])
    y = _dot((jax.nn.gelu(gate, approximate=True) * hs).astype(BF16), wout_ref[...])
    o_ref[0] = _layer_norm(ALPHA * x + y, g_ref[...], b_ref[...])
    ts = xr.shape[0]
    xcarry[...] = xr[ts - SUBLANES:ts]
    hcarry[...] = jnp.broadcast_to(hlast, hcarry.shape)

    @pl.when(s == pl.num_programs(1) - 1)
    def _():
        buf_ref[0] = xr[ts - 3:ts]
        hl_ref[0] = hlast


def _rglru_prompt(x, win, cw, cb, wcat, gab, gxb, lam, wout, g, b):
    bsz, seq, d = x.shape
    ts = min(TS_B, seq)
    grid = (bsz, seq // ts)
    blk = d // N_RG_BLOCKS
    const2 = lambda i, j: (0, 0)
    return pl.pallas_call(
        _rglru_prompt_body,
        grid=grid,
        in_specs=[
            pl.BlockSpec((1, ts, d), lambda i, j: (i, j, 0)),
            pl.BlockSpec((d, 2 * d), const2),
            pl.BlockSpec((4, d), const2),
            pl.BlockSpec((1, d), const2),
            pl.BlockSpec((N_RG_BLOCKS, blk, 2 * blk), lambda i, j: (0, 0, 0)),
            pl.BlockSpec((1, d), const2),
            pl.BlockSpec((1, d), const2),
            pl.BlockSpec((1, d), const2),
            pl.BlockSpec((d, d), const2),
            pl.BlockSpec((1, d), const2),
            pl.BlockSpec((1, d), const2),
        ],
        out_specs=[
            pl.BlockSpec((1, ts, d), lambda i, j: (i, j, 0)),
            pl.BlockSpec((1, 3, d), lambda i, j: (i, 0, 0)),
            pl.BlockSpec((1, 1, d), lambda i, j: (i, 0, 0)),
        ],
        out_shape=[
            jax.ShapeDtypeStruct((bsz, seq, d), F32),
            jax.ShapeDtypeStruct((bsz, 3, d), F32),
            jax.ShapeDtypeStruct((bsz, 1, d), F32),
        ],
        scratch_shapes=[pltpu.VMEM((SUBLANES, d), F32), pltpu.VMEM((SUBLANES, d), F32)],
        compiler_params=pltpu.CompilerParams(
            dimension_semantics=("arbitrary", "arbitrary"), vmem_limit_bytes=VMEM_LIMIT),
        name="rglru_prompt",
    )(x, win, cw, cb, wcat, gab, gxb, lam, wout, g, b)


def _rglru_sample_body(x_ref, s0_ref, s1_ref, s2_ref, h0_ref, win_ref, cw_ref, cb_ref, wcat_ref,
                       gab_ref, gxb_ref, lam_ref, wout_ref, g_ref, b_ref,
                       o_ref, xr_ref, h_ref):
    x = x_ref[...]
    d = x.shape[1]
    gx = _dot(x.astype(BF16), win_ref[...])
    gate, xr = gx[:, :d], gx[:, d:]
    cw = cw_ref[...]
    xc = (cw[0:1] * s0_ref[...] + cw[1:2] * s1_ref[...] + cw[2:3] * s2_ref[...]
          + cw[3:4] * xr) + cb_ref[...]
    a, bt = _rglru_coeffs(xc, wcat_ref, gab_ref[...], gxb_ref[...], lam_ref[...])
    h = a * h0_ref[...] + bt
    y = _dot((jax.nn.gelu(gate, approximate=True) * h).astype(BF16), wout_ref[...])
    o_ref[...] = _layer_norm(ALPHA * x + y, g_ref[...], b_ref[...])
    xr_ref[...] = xr
    h_ref[...] = h


def _rglru_sample(x, s0, s1, s2, h0, win, cw, cb, wcat, gab, gxb, lam, wout, g, b):
    n, d = x.shape
    return pl.pallas_call(
        _rglru_sample_body,
        out_shape=[jax.ShapeDtypeStruct((n, d), F32)] * 3,
        compiler_params=pltpu.CompilerParams(vmem_limit_bytes=VMEM_LIMIT),
        name="rglru_sample",
    )(x, s0, s1, s2, h0, win, cw, cb, wcat, gab, gxb, lam, wout, g, b)


def _first_argmax(v, rows):
    m = jnp.max(v, axis=0, keepdims=True)
    idx = jnp.min(jnp.where(v == m, rows, v.shape[0]), axis=0, keepdims=True)
    return m, idx


def _route_body(x_ref, wrt_ref, tri_ref, cin_ref, ints_ref, wts_ref, cnt_ref, carry):
    @pl.when(pl.program_id(0) == 0)
    def _():
        carry[...] = cin_ref[...]

    xb = x_ref[...].astype(BF16)
    tt = xb.shape[0]
    lt = lax.dot_general(wrt_ref[...], xb, (((1,), (1,)), ((), ())), preferred_element_type=F32)
    rows8 = lax.broadcasted_iota(I32, (SUBLANES, tt), 0)
    neg_inf = jnp.float32(-jnp.inf)

    gl = jnp.where(rows8 < N_GROUPS, lt[GROUP_ROW0:GROUP_ROW0 + SUBLANES], neg_inf)
    gmax, gidx = _first_argmax(gl, rows8)
    gw = 1.0 / jnp.sum(jnp.exp(gl - gmax), axis=0, keepdims=True)

    el = lt[0:EXP_PER_GROUP]
    for g in range(1, N_GROUPS):
        el = jnp.where(gidx == g, lt[g * EXP_PER_GROUP:(g + 1) * EXP_PER_GROUP], el)
    emax, i1 = _first_argmax(el, rows8)
    el2 = jnp.where(rows8 == i1, neg_inf, el)
    m2, i2 = _first_argmax(el2, rows8)
    psum = jnp.sum(jnp.exp(el - emax), axis=0, keepdims=True)
    ep1 = 1.0 / psum
    ep2 = jnp.exp(m2 - emax) / psum
    tot = ep1 + ep2
    wa = gw * (ep1 / tot)
    wb = gw * (ep2 / tot)
    ea = gidx * EXP_PER_GROUP + i1
    eb = gidx * EXP_PER_GROUP + i2

    rows_e = lax.broadcasted_iota(I32, (N_EXPERTS, tt), 0)
    oha = rows_e == ea
    ohb = rows_e == eb
    oh = jnp.where(oha | ohb, 1.0, 0.0)
    base = carry[...][:, 0:1]
    excl = _dot(oh.astype(BF16), tri_ref[...]) + base
    ra = jnp.sum(jnp.where(oha, excl, 0.0), axis=0, keepdims=True)
    rb = jnp.sum(jnp.where(ohb, excl, 0.0), axis=0, keepdims=True)
    new = carry[...] + jnp.sum(oh, axis=1, keepdims=True)
    carry[...] = new
    cnt_ref[...] = new

    ints_ref[0:1, :] = ea
    ints_ref[1:2, :] = eb
    ints_ref[2:3, :] = ra.astype(I32)
    ints_ref[3:4, :] = rb.astype(I32)
    wts_ref[0:1, :] = wa
    wts_ref[1:2, :] = wb


def _route(x, wrt, cin):
    t, d = x.shape
    tt = min(TT_ROUTE, t)
    tri = (jnp.arange(tt)[:, None] < jnp.arange(tt)[None, :]).astype(BF16)
    return pl.pallas_call(
        _route_body,
        grid=(t // tt,),
        in_specs=[
            pl.BlockSpec((tt, d), lambda i: (i, 0)),
            pl.BlockSpec((ROUTE_ROWS, d), lambda i: (0, 0)),
            pl.BlockSpec((tt, tt), lambda i: (0, 0)),
            pl.BlockSpec((N_EXPERTS, LANES), lambda i: (0, 0)),
        ],
        out_specs=[
            pl.BlockSpec((4, tt), lambda i: (0, i)),
            pl.BlockSpec((2, tt), lambda i: (0, i)),
            pl.BlockSpec((N_EXPERTS, LANES), lambda i: (0, 0)),
        ],
        out_shape=[
            jax.ShapeDtypeStruct((4, t), I32),
            jax.ShapeDtypeStruct((2, t), F32),
            jax.ShapeDtypeStruct((N_EXPERTS, LANES), F32),
        ],
        scratch_shapes=[pltpu.VMEM((N_EXPERTS, LANES), F32)],
        compiler_params=pltpu.CompilerParams(
            dimension_semantics=("arbitrary",), vmem_limit_bytes=VMEM_LIMIT),
        name="route",
    )(x, wrt, tri, cin)


def _dest_body(ints_ref, pst_ref, dest_ref):
    ints = ints_ref[...]
    tt = ints.shape[1]
    rows_e = lax.broadcasted_iota(I32, (N_EXPERTS, tt), 0)
    pst = pst_ref[...][:, 0:1]
    for k in range(2):
        start = jnp.sum(jnp.where(rows_e == ints[k:k + 1], pst, 0.0), axis=0, keepdims=True)
        dest_ref[k:k + 1, :] = start.astype(I32) + ints[2 + k:3 + k]


def _dest(ints, pstart_f):
    t = ints.shape[1]
    return pl.pallas_call(
        _dest_body,
        out_shape=jax.ShapeDtypeStruct((2, t), I32),
        compiler_params=pltpu.CompilerParams(vmem_limit_bytes=VMEM_LIMIT),
        name="dest",
    )(ints, pstart_f)


def _dispatch_body(dest_ref, xp_ref, xs_ref, zero_ref, out_ref, sem, *, n_prompt_tiles, t_total):
    del zero_ref
    i = pl.program_id(0)

    def row_copy(src_ref, r, slot):
        return pltpu.make_async_copy(src_ref.at[pl.ds(r, 1)], out_ref.at[pl.ds(slot, 1)], sem)

    def scatter(src_ref, tok0, n):
        def start(r, c):
            row_copy(src_ref, r, dest_ref[tok0 + r]).start()
            row_copy(src_ref, r, dest_ref[t_total + tok0 + r]).start()
            return c

        def wait(r, c):
            row_copy(src_ref, r, dest_ref[tok0 + r]).wait()
            row_copy(src_ref, r, dest_ref[t_total + tok0 + r]).wait()
            return c

        lax.fori_loop(0, n, start, 0, unroll=8)
        lax.fori_loop(0, n, wait, 0, unroll=8)

    @pl.when(i < n_prompt_tiles)
    def _():
        scatter(xp_ref, i * TD, TD)

    @pl.when(i == n_prompt_tiles)
    def _():
        scatter(xs_ref, n_prompt_tiles * TD, xs_ref.shape[0])


def _dispatch(dest_flat, xp, xs, p_rows):
    tp, d = xp.shape
    ts = xs.shape[0]
    n_prompt_tiles = tp // TD
    zeros = jnp.zeros((p_rows, d), F32)
    grid_spec = pltpu.PrefetchScalarGridSpec(
        num_scalar_prefetch=1,
        grid=(n_prompt_tiles + 1,),
        in_specs=[
            pl.BlockSpec((TD, d), lambda i, dr: (jnp.minimum(i, n_prompt_tiles - 1), 0)),
            pl.BlockSpec((ts, d), lambda i, dr: (0, 0)),
            pl.BlockSpec(memory_space=pl.ANY),
        ],
        out_specs=pl.BlockSpec(memory_space=pl.ANY),
        scratch_shapes=[pltpu.SemaphoreType.DMA],
    )
    return pl.pallas_call(
        functools.partial(_dispatch_body, n_prompt_tiles=n_prompt_tiles, t_total=tp + ts),
        grid_spec=grid_spec,
        out_shape=jax.ShapeDtypeStruct((p_rows, d), F32),
        input_output_aliases={3: 0},
        compiler_params=pltpu.CompilerParams(
            dimension_semantics=("arbitrary",), vmem_limit_bytes=VMEM_LIMIT),
        name="dispatch",
    )(dest_flat, xp, xs, zeros)


def _experts_body(be_ref, nu_ref, xs_ref, wg_ref, wu_ref, wd_ref, o_ref):
    b = pl.program_id(0)

    @pl.when(b < nu_ref[0])
    def _():
        xb = xs_ref[...].astype(BF16)
        h = jax.nn.silu(_dot(xb, wg_ref[0])) * _dot(xb, wu_ref[0])
        o_ref[...] = _dot(h.astype(BF16), wd_ref[0])

    @pl.when(b >= nu_ref[0])
    def _():
        o_ref[...] = jnp.zeros_like(o_ref)


def _experts(blk_e, n_used, xs, wg, wu, wd):
    p_rows, d = xs.shape
    de = wg.shape[2]
    nb = p_rows // BLK
    grid_spec = pltpu.PrefetchScalarGridSpec(
        num_scalar_prefetch=2,
        grid=(nb,),
        in_specs=[
            pl.BlockSpec((BLK, d), lambda b, be, nu: (jnp.minimum(b, nu[0] - 1), 0)),
            pl.BlockSpec((1, d, de), lambda b, be, nu: (be[b], 0, 0)),
            pl.BlockSpec((1, d, de), lambda b, be, nu: (be[b], 0, 0)),
            pl.BlockSpec((1, de, d), lambda b, be, nu: (be[b], 0, 0)),
        ],
        out_specs=pl.BlockSpec((BLK, d), lambda b, be, nu: (b, 0)),
    )
    return pl.pallas_call(
        _experts_body,
        grid_spec=grid_spec,
        out_shape=jax.ShapeDtypeStruct((p_rows, d), F32),
        compiler_params=pltpu.CompilerParams(
            dimension_semantics=("arbitrary",), vmem_limit_bytes=VMEM_LIMIT),
        name="experts",
    )(blk_e, n_used, xs, wg, wu, wd)


def _combine_body(dest_ref, x_ref, w_ref, g_ref, b_ref, ob_ref, o_ref, buf, sem,
                  *, tok0, t_total, tc):
    i = pl.program_id(0)
    n = pl.num_programs(0)

    def row_copy(slot, k, r, src_row):
        return pltpu.make_async_copy(ob_ref.at[pl.ds(src_row, 1)],
                                     buf.at[slot, k, pl.ds(r, 1)], sem.at[slot])

    def for_rows(tile, slot, fn):
        base = tok0 + tile * tc

        def body(r, c):
            fn(row_copy(slot, 0, r, dest_ref[base + r]))
            fn(row_copy(slot, 1, r, dest_ref[t_total + base + r]))
            return c

        lax.fori_loop(0, tc, body, 0, unroll=8)

    @pl.when(i == 0)
    def _():
        for_rows(0, 0, lambda cp: cp.start())

    @pl.when(i + 1 < n)
    def _():
        for_rows(i + 1, (i + 1) % 2, lambda cp: cp.start())

    slot = i % 2
    for_rows(i, slot, lambda cp: cp.wait())
    w = w_ref[...]
    y = w[:, 0:1] * buf[slot, 0] + w[:, 1:2] * buf[slot, 1]
    o_ref[...] = _layer_norm(ALPHA * x_ref[...] + y, g_ref[...], b_ref[...])


def _combine(dest_flat, x, w_cols, g, b, ob, tok0, t_total):
    t, d = x.shape
    tc = min(TC, t)
    grid_spec = pltpu.PrefetchScalarGridSpec(
        num_scalar_prefetch=1,
        grid=(t // tc,),
        in_specs=[
            pl.BlockSpec((tc, d), lambda i, dr: (i, 0)),
            pl.BlockSpec((tc, 2), lambda i, dr: (i, 0)),
            pl.BlockSpec((1, d), lambda i, dr: (0, 0)),
            pl.BlockSpec((1, d), lambda i, dr: (0, 0)),
            pl.BlockSpec(memory_space=pl.ANY),
        ],
        out_specs=pl.BlockSpec((tc, d), lambda i, dr: (i, 0)),
        scratch_shapes=[pltpu.VMEM((2, 2, tc, d), F32), pltpu.SemaphoreType.DMA((2,))],
    )
    return pl.pallas_call(
        functools.partial(_combine_body, tok0=tok0, t_total=t_total, tc=tc),
        grid_spec=grid_spec,
        out_shape=jax.ShapeDtypeStruct((t, d), F32),
        compiler_params=pltpu.CompilerParams(
            dimension_semantics=("arbitrary",), vmem_limit_bytes=VMEM_LIMIT),
        name="combine",
    )(dest_flat, x, w_cols, g, b, ob)


def _moe_layer(xp, xs, w_group, w_expert, wg, wu, wd, g, b):
    tp, d = xp.shape
    ts = xs.shape[0]
    t_total = tp + ts
    wrt = jnp.zeros((ROUTE_ROWS, d), F32)
    wrt = wrt.at[0:N_EXPERTS].set(w_expert.T).at[GROUP_ROW0:GROUP_ROW0 + N_GROUPS].set(w_group.T)
    wrt = wrt.astype(BF16)

    zero_cnt = jnp.zeros((N_EXPERTS, LANES), F32)
    ints_p, wts_p, cnt_p = _route(xp, wrt, zero_cnt)
    ints_s, wts_s, cnt = _route(xs, wrt, cnt_p)
    ints = jnp.concatenate([ints_p, ints_s], axis=1)
    wts = jnp.concatenate([wts_p, wts_s], axis=1)

    counts = cnt[:, 0].astype(I32)
    pcounts = (counts + BLK - 1) // BLK * BLK
    pend = jnp.cumsum(pcounts)
    pstart = pend - pcounts
    nb = (2 * t_total + N_EXPERTS * (BLK - 1) + BLK - 1) // BLK
    p_rows = nb * BLK
    n_used = (pend[-1] // BLK).astype(I32).reshape(1)
    blk_e = jnp.searchsorted(pend, jnp.arange(nb, dtype=I32) * BLK, side='right').astype(I32)
    blk_e = jnp.minimum(blk_e, blk_e[jnp.maximum(n_used[0] - 1, 0)])

    pstart_f = jnp.broadcast_to(pstart.astype(F32)[:, None], (N_EXPERTS, LANES))
    dest_flat = _dest(ints, pstart_f).reshape(2 * t_total)

    xsorted = _dispatch(dest_flat, xp, xs, p_rows)
    ob = _experts(blk_e, n_used, xsorted, wg, wu, wd)
    w_cols = wts.T
    yp = _combine(dest_flat, xp, w_cols[:tp], g, b, ob, 0, t_total)
    ys = _combine(dest_flat, xs, w_cols[tp:], g, b, ob, tp, t_total)
    return yp, ys


def kernel(x_prompt, x_sample, state_conv_a, state_conv_b, state_h, a_w_in, a_conv_w, a_w_out,
           b_w_in, b_conv_w, b_conv_b, b_gate_a_w, b_gate_a_b, b_gate_x_w, b_gate_x_b, b_lambda,
           b_w_out, ln1_g, ln1_b, ln2_g, ln2_b, moe_w_group, moe_w_expert, moe_w_gate, moe_w_up,
           moe_w_down):
    bsz, seq, d = x_prompt.shape
    n_s = x_sample.shape[0]
    row = lambda v: v.reshape(1, d)

    win, wout = a_w_in[0].astype(BF16), a_w_out[0].astype(BF16)
    xp, conv_a_p = _conv_a_prompt(x_prompt, win, a_conv_w[0], wout, row(ln1_g[0]), row(ln1_b[0]))
    sa = state_conv_a[0]
    xs, u_s = _conv_a_sample(x_sample.reshape(n_s, d), sa[:, 0], sa[:, 1], win, a_conv_w[0], wout,
                             row(ln1_g[0]), row(ln1_b[0]))
    conv_a_s = jnp.stack([sa[:, 1], u_s], axis=1)

    xp, xs = _moe_layer(xp.reshape(bsz * seq, d), xs, moe_w_group[0], moe_w_expert[0],
                        moe_w_gate[0].astype(BF16), moe_w_up[0].astype(BF16),
                        moe_w_down[0].astype(BF16), row(ln2_g[0]), row(ln2_b[0]))

    win, wout = b_w_in[0].astype(BF16), b_w_out[0].astype(BF16)
    wcat = jnp.concatenate([b_gate_a_w[0], b_gate_x_w[0]], axis=-1).astype(BF16)
    args = (win, b_conv_w[0], row(b_conv_b[0]), wcat, row(b_gate_a_b[0]), row(b_gate_x_b[0]),
            row(b_lambda[0]), wout, row(ln1_g[1]), row(ln1_b[1]))
    xp, conv_b_p, h_p = _rglru_prompt(xp.reshape(bsz, seq, d), *args)
    sb = state_conv_b[0]
    xs, xr_s, h_s = _rglru_sample(xs, sb[:, 0], sb[:, 1], sb[:, 2], state_h[0], *args)
    conv_b_s = jnp.stack([sb[:, 1], sb[:, 2], xr_s], axis=1)

    xp, xs = _moe_layer(xp.reshape(bsz * seq, d), xs, moe_w_group[1], moe_w_expert[1],
                        moe_w_gate[1].astype(BF16), moe_w_up[1].astype(BF16),
                        moe_w_down[1].astype(BF16), row(ln2_g[1]), row(ln2_b[1]))

    return (xp.reshape(bsz, seq, d), xs.reshape(n_s, 1, d),
            conv_a_p[None], conv_a_s[None], conv_b_p[None], conv_b_s[None],
            h_p.reshape(1, bsz, d), h_s[None])
```

```python
import functools

import jax
import jax.numpy as jnp
from jax import lax
from jax.experimental import pallas as pl
from jax.experimental.pallas import tpu as pltpu

F32 = jnp.float32
BF16 = jnp.bfloat16
I32 = jnp.int32

DEPTH = 2
N_RG_BLOCKS = 8
RG_C = 8.0
N_GROUPS = 4
EXP_PER_GROUP = 8
N_EXPERTS = N_GROUPS * EXP_PER_GROUP
ALPHA = (2.0 * DEPTH) ** 0.25
LN_EPS = 1e-5

LANES = 128
SUBLANES = 8
VMEM_LIMIT = 56 * 1024 * 1024

TS_A = 512
TS_B = 256
TT_ROUTE = 1024
TD = 128
TC = 256
BLK = 256
ROUTE_ROWS = 128
GROUP_ROW0 = N_EXPERTS
W_CHUNK = 512


def _dot(a, b):
    return jnp.dot(a, b, preferred_element_type=F32)


def _load_weight_bf16(w_hbm, w_bf, stage, sem):
    nch = w_hbm.shape[1] // W_CHUNK

    def chunk_copy(c):
        return pltpu.make_async_copy(w_hbm.at[:, pl.ds(c * W_CHUNK, W_CHUNK)],
                                     stage.at[c % 2], sem.at[c % 2])

    chunk_copy(0).start()
    for c in range(nch):
        if c + 1 < nch:
            chunk_copy(c + 1).start()
        chunk_copy(c).wait()
        w_bf[:, c * W_CHUNK:(c + 1) * W_CHUNK] = stage[c % 2].astype(BF16)


def _weight_scratch(k, *ns):
    return ([pltpu.VMEM((k, n), BF16) for n in ns]
            + [pltpu.VMEM((2, k, W_CHUNK), F32), pltpu.SemaphoreType.DMA((2,))])


def _layer_norm(r, g, b):
    mu = jnp.mean(r, axis=-1, keepdims=True)
    d = r - mu
    var = jnp.mean(d * d, axis=-1, keepdims=True)
    return d * lax.rsqrt(var + LN_EPS) * g + b


def _shift_rows(v, k, prev8):
    rolled = pltpu.roll(v, k, axis=0)
    rows8 = lax.broadcasted_iota(I32, (SUBLANES, v.shape[1]), 0)
    first = jnp.where(rows8 < k, pltpu.roll(prev8, k, axis=0), rolled[0:SUBLANES])
    return jnp.concatenate([first, rolled[SUBLANES:]], axis=0)


def _softplus(v):
    return jnp.maximum(v, 0.0) + jnp.log1p(jnp.exp(-jnp.abs(v)))


def _rglru_coeffs(xc, wcat_ref, gab, gxb, lam):
    d = xc.shape[1]
    blk = d // N_RG_BLOCKS
    xcb = xc.astype(BF16)
    rs, is_ = [], []
    for n in range(N_RG_BLOCKS):
        o = _dot(xcb[:, n * blk:(n + 1) * blk], wcat_ref[n])
        rs.append(o[:, :blk])
        is_.append(o[:, blk:])
    r = jax.nn.sigmoid(jnp.concatenate(rs, axis=1) + gab)
    i = jax.nn.sigmoid(jnp.concatenate(is_, axis=1) + gxb)
    log_a = -RG_C * r * _softplus(-lam)
    a = jnp.exp(log_a)
    mult = jnp.sqrt(-jnp.tanh(log_a) * (a * a + 1.0))
    return a, mult * (i * xc)


def _scan_rows(a, b, h0):
    m = a.shape[0]
    sub = lax.broadcasted_iota(I32, a.shape, 0) % SUBLANES
    for k in (1, 2, 4):
        a_sh = pltpu.roll(a, k, axis=0)
        b_sh = pltpu.roll(b, k, axis=0)
        keep = sub >= k
        b = jnp.where(keep, a * b_sh + b, b)
        a = jnp.where(keep, a * a_sh, a)
    outs = []
    h = h0
    for g in range(m // SUBLANES):
        hg = a[g * SUBLANES:(g + 1) * SUBLANES] * h + b[g * SUBLANES:(g + 1) * SUBLANES]
        outs.append(hg)
        h = hg[SUBLANES - 1:SUBLANES]
    return jnp.concatenate(outs, axis=0), h


def _conv_a_prompt_body(x_ref, win_hbm, cw_ref, wout_hbm, g_ref, b_ref,
                        o_ref, buf_ref, carry, win_ref, wout_ref, stage, wsem):
    s = pl.program_id(1)

    @pl.when(jnp.logical_and(pl.program_id(0) == 0, s == 0))
    def _():
        _load_weight_bf16(win_hbm, win_ref, stage, wsem)
        _load_weight_bf16(wout_hbm, wout_ref, stage, wsem)

    @pl.when(s == 0)
    def _():
        carry[...] = jnp.zeros_like(carry)

    x = x_ref[0]
    d = x.shape[1]
    bcx = _dot(x.astype(BF16), win_ref[...])
    gb, gc, xh = bcx[:, :d], bcx[:, d:2 * d], bcx[:, 2 * d:]
    u = gc * xh
    prev = carry[...]
    cw = cw_ref[...]
    conv = (cw[0:1] * _shift_rows(u, 2, prev) + cw[1:2] * _shift_rows(u, 1, prev)
            + cw[2:3] * u)
    y = _dot((gb * conv).astype(BF16), wout_ref[...])
    o_ref[0] = _layer_norm(ALPHA * x + y, g_ref[...], b_ref[...])
    ts = u.shape[0]
    carry[...] = u[ts - SUBLANES:ts]

    @pl.when(s == pl.num_programs(1) - 1)
    def _():
        buf_ref[0] = u[ts - 2:ts]


def _conv_a_prompt(x, win, cw, wout, g, b):
    bsz, seq, d = x.shape
    ts = min(TS_A, seq)
    grid = (bsz, seq // ts)
    return pl.pallas_call(
        _conv_a_prompt_body,
        grid=grid,
        in_specs=[
            pl.BlockSpec((1, ts, d), lambda i, j: (i, j, 0)),
            pl.BlockSpec(memory_space=pl.ANY),
            pl.BlockSpec((3, d), lambda i, j: (0, 0)),
            pl.BlockSpec(memory_space=pl.ANY),
            pl.BlockSpec((1, d), lambda i, j: (0, 0)),
            pl.BlockSpec((1, d), lambda i, j: (0, 0)),
        ],
        out_specs=[
            pl.BlockSpec((1, ts, d), lambda i, j: (i, j, 0)),
            pl.BlockSpec((1, 2, d), lambda i, j: (i, 0, 0)),
        ],
        out_shape=[
            jax.ShapeDtypeStruct((bsz, seq, d), F32),
            jax.ShapeDtypeStruct((bsz, 2, d), F32),
        ],
        scratch_shapes=[pltpu.VMEM((SUBLANES, d), F32)] + _weight_scratch(d, 3 * d, d),
        compiler_params=pltpu.CompilerParams(
            dimension_semantics=("arbitrary", "arbitrary"), vmem_limit_bytes=VMEM_LIMIT),
        name="conv_a_prompt",
    )(x, win, cw, wout, g, b)


def _conv_a_sample_body(x_ref, s0_ref, s1_ref, win_ref, cw_ref, wout_ref, g_ref, b_ref,
                        o_ref, u_ref):
    x = x_ref[...]
    d = x.shape[1]
    bcx = _dot(x.astype(BF16), win_ref[...].astype(BF16))
    gb, gc, xh = bcx[:, :d], bcx[:, d:2 * d], bcx[:, 2 * d:]
    u = gc * xh
    cw = cw_ref[...]
    conv = cw[0:1] * s0_ref[...] + cw[1:2] * s1_ref[...] + cw[2:3] * u
    y = _dot((gb * conv).astype(BF16), wout_ref[...].astype(BF16))
    o_ref[...] = _layer_norm(ALPHA * x + y, g_ref[...], b_ref[...])
    u_ref[...] = u


def _conv_a_sample(x, s0, s1, win, cw, wout, g, b):
    n, d = x.shape
    return pl.pallas_call(
        _conv_a_sample_body,
        out_shape=[jax.ShapeDtypeStruct((n, d), F32), jax.ShapeDtypeStruct((n, d), F32)],
        compiler_params=pltpu.CompilerParams(vmem_limit_bytes=VMEM_LIMIT),
        name="conv_a_sample",
    )(x, s0, s1, win, cw, wout, g, b)


def _rglru_prompt_body(x_ref, win_hbm, cw_ref, cb_ref, wcat_ref, gab_ref, gxb_ref, lam_ref,
                       wout_hbm, g_ref, b_ref, o_ref, buf_ref, hl_ref, xcarry, hcarry,
                       win_ref, wout_ref, stage, wsem):
    s = pl.program_id(1)

    @pl.when(jnp.logical_and(pl.program_id(0) == 0, s == 0))
    def _():
        _load_weight_bf16(win_hbm, win_ref, stage, wsem)
        _load_weight_bf16(wout_hbm, wout_ref, stage, wsem)

    @pl.when(s == 0)
    def _():
        xcarry[...] = jnp.zeros_like(xcarry)
        hcarry[...] = jnp.zeros_like(hcarry)

    x = x_ref[0]
    d = x.shape[1]
    gx = _dot(x.astype(BF16), win_ref[...])
    gate, xr = gx[:, :d], gx[:, d:]
    prev = xcarry[...]
    cw = cw_ref[...]
    xc = (cw[0:1] * _shift_rows(xr, 3, prev) + cw[1:2] * _shift_rows(xr, 2, prev)
          + cw[2:3] * _shift_rows(xr, 1, prev) + cw[3:4] * xr) + cb_ref[...]
    a, bt = _rglru_coeffs(xc, wcat_ref, gab_ref[...], gxb_ref[...], lam_ref[...])
    hs, hlast = _scan_rows(a, bt, hcarry[0:1])
    y = _dot((jax.nn.gelu(gate, approximate=True) * hs).astype(BF16), wout_ref[...])
    o_ref[0] = _layer_norm(ALPHA * x + y, g_ref[...], b_ref[...])
    ts = xr.shape[0]
    xcarry[...] = xr[ts - SUBLANES:ts]
    hcarry[...] = jnp.broadcast_to(hlast, hcarry.shape)

    @pl.when(s == pl.num_programs(1) - 1)
    def _():
        buf_ref[0] = xr[ts - 3:ts]
        hl_ref[0] = hlast


def _rglru_prompt(x, win, cw, cb, wcat, gab, gxb, lam, wout, g, b):
    bsz, seq, d = x.shape
    ts = min(TS_B, seq)
    grid = (bsz, seq // ts)
    blk = d // N_RG_BLOCKS
    const2 = lambda i, j: (0, 0)
    return pl.pallas_call(
        _rglru_prompt_body,
        grid=grid,
        in_specs=[
            pl.BlockSpec((1, ts, d), lambda i, j: (i, j, 0)),
            pl.BlockSpec(memory_space=pl.ANY),
            pl.BlockSpec((4, d), const2),
            pl.BlockSpec((1, d), const2),
            pl.BlockSpec((N_RG_BLOCKS, blk, 2 * blk), lambda i, j: (0, 0, 0)),
            pl.BlockSpec((1, d), const2),
            pl.BlockSpec((1, d), const2),
            pl.BlockSpec((1, d), const2),
            pl.BlockSpec(memory_space=pl.ANY),
            pl.BlockSpec((1, d), const2),
            pl.BlockSpec((1, d), const2),
        ],
        out_specs=[
            pl.BlockSpec((1, ts, d), lambda i, j: (i, j, 0)),
            pl.BlockSpec((1, 3, d), lambda i, j: (i, 0, 0)),
            pl.BlockSpec((1, 1, d), lambda i, j: (i, 0, 0)),
        ],
        out_shape=[
            jax.ShapeDtypeStruct((bsz, seq, d), F32),
            jax.ShapeDtypeStruct((bsz, 3, d), F32),
            jax.ShapeDtypeStruct((bsz, 1, d), F32),
        ],
        scratch_shapes=([pltpu.VMEM((SUBLANES, d), F32), pltpu.VMEM((SUBLANES, d), F32)]
                        + _weight_scratch(d, 2 * d, d)),
        compiler_params=pltpu.CompilerParams(
            dimension_semantics=("arbitrary", "arbitrary"), vmem_limit_bytes=VMEM_LIMIT),
        name="rglru_prompt",
    )(x, win, cw, cb, wcat, gab, gxb, lam, wout, g, b)


def _rglru_sample_body(x_ref, s0_ref, s1_ref, s2_ref, h0_ref, win_ref, cw_ref, cb_ref, wcat_ref,
                       gab_ref, gxb_ref, lam_ref, wout_ref, g_ref, b_ref,
                       o_ref, xr_ref, h_ref):
    x = x_ref[...]
    d = x.shape[1]
    gx = _dot(x.astype(BF16), win_ref[...].astype(BF16))
    gate, xr = gx[:, :d], gx[:, d:]
    cw = cw_ref[...]
    xc = (cw[0:1] * s0_ref[...] + cw[1:2] * s1_ref[...] + cw[2:3] * s2_ref[...]
          + cw[3:4] * xr) + cb_ref[...]
    a, bt = _rglru_coeffs(xc, wcat_ref, gab_ref[...], gxb_ref[...], lam_ref[...])
    h = a * h0_ref[...] + bt
    y = _dot((jax.nn.gelu(gate, approximate=True) * h).astype(BF16),
             wout_ref[...].astype(BF16))
    o_ref[...] = _layer_norm(ALPHA * x + y, g_ref[...], b_ref[...])
    xr_ref[...] = xr
    h_ref[...] = h


def _rglru_sample(x, s0, s1, s2, h0, win, cw, cb, wcat, gab, gxb, lam, wout, g, b):
    n, d = x.shape
    return pl.pallas_call(
        _rglru_sample_body,
        out_shape=[jax.ShapeDtypeStruct((n, d), F32)] * 3,
        compiler_params=pltpu.CompilerParams(vmem_limit_bytes=VMEM_LIMIT),
        name="rglru_sample",
    )(x, s0, s1, s2, h0, win, cw, cb, wcat, gab, gxb, lam, wout, g, b)


def _first_argmax(v, rows):
    m = jnp.max(v, axis=0, keepdims=True)
    idx = jnp.min(jnp.where(v == m, rows, v.shape[0]), axis=0, keepdims=True)
    return m, idx


def _route_body(x_ref, wrt_ref, tri_ref, cin_ref, ints_ref, wts_ref, cnt_ref, carry):
    @pl.when(pl.program_id(0) == 0)
    def _():
        carry[...] = cin_ref[...]

    xb = x_ref[...].astype(BF16)
    tt = xb.shape[0]
    lt = lax.dot_general(wrt_ref[...], xb, (((1,), (1,)), ((), ())), preferred_element_type=F32)
    rows8 = lax.broadcasted_iota(I32, (SUBLANES, tt), 0)
    neg_inf = jnp.float32(-jnp.inf)

    gl = jnp.where(rows8 < N_GROUPS, lt[GROUP_ROW0:GROUP_ROW0 + SUBLANES], neg_inf)
    gmax, gidx = _first_argmax(gl, rows8)
    gw = 1.0 / jnp.sum(jnp.exp(gl - gmax), axis=0, keepdims=True)

    el = lt[0:EXP_PER_GROUP]
    for g in range(1, N_GROUPS):
        el = jnp.where(gidx == g, lt[g * EXP_PER_GROUP:(g + 1) * EXP_PER_GROUP], el)
    emax, i1 = _first_argmax(el, rows8)
    el2 = jnp.where(rows8 == i1, neg_inf, el)
    m2, i2 = _first_argmax(el2, rows8)
    psum = jnp.sum(jnp.exp(el - emax), axis=0, keepdims=True)
    ep1 = 1.0 / psum
    ep2 = jnp.exp(m2 - emax) / psum
    tot = ep1 + ep2
    wa = gw * (ep1 / tot)
    wb = gw * (ep2 / tot)
    ea = gidx * EXP_PER_GROUP + i1
    eb = gidx * EXP_PER_GROUP + i2

    rows_e = lax.broadcasted_iota(I32, (N_EXPERTS, tt), 0)
    oha = rows_e == ea
    ohb = rows_e == eb
    oh = jnp.where(oha | ohb, 1.0, 0.0)
    base = carry[...][:, 0:1]
    excl = _dot(oh.astype(BF16), tri_ref[...]) + base
    ra = jnp.sum(jnp.where(oha, excl, 0.0), axis=0, keepdims=True)
    rb = jnp.sum(jnp.where(ohb, excl, 0.0), axis=0, keepdims=True)
    new = carry[...] + jnp.sum(oh, axis=1, keepdims=True)
    carry[...] = new
    cnt_ref[...] = new

    ints_ref[0:1, :] = ea
    ints_ref[1:2, :] = eb
    ints_ref[2:3, :] = ra.astype(I32)
    ints_ref[3:4, :] = rb.astype(I32)
    wts_ref[0:1, :] = wa
    wts_ref[1:2, :] = wb


def _route(x, wrt, cin):
    t, d = x.shape
    tt = min(TT_ROUTE, t)
    tri = (jnp.arange(tt)[:, None] < jnp.arange(tt)[None, :]).astype(BF16)
    return pl.pallas_call(
        _route_body,
        grid=(t // tt,),
        in_specs=[
            pl.BlockSpec((tt, d), lambda i: (i, 0)),
            pl.BlockSpec((ROUTE_ROWS, d), lambda i: (0, 0)),
            pl.BlockSpec((tt, tt), lambda i: (0, 0)),
            pl.BlockSpec((N_EXPERTS, LANES), lambda i: (0, 0)),
        ],
        out_specs=[
            pl.BlockSpec((4, tt), lambda i: (0, i)),
            pl.BlockSpec((2, tt), lambda i: (0, i)),
            pl.BlockSpec((N_EXPERTS, LANES), lambda i: (0, 0)),
        ],
        out_shape=[
            jax.ShapeDtypeStruct((4, t), I32),
            jax.ShapeDtypeStruct((2, t), F32),
            jax.ShapeDtypeStruct((N_EXPERTS, LANES), F32),
        ],
        scratch_shapes=[pltpu.VMEM((N_EXPERTS, LANES), F32)],
        compiler_params=pltpu.CompilerParams(
            dimension_semantics=("arbitrary",), vmem_limit_bytes=VMEM_LIMIT),
        name="route",
    )(x, wrt, tri, cin)


def _dest_body(ints_ref, pst_ref, dest_ref):
    ints = ints_ref[...]
    tt = ints.shape[1]
    rows_e = lax.broadcasted_iota(I32, (N_EXPERTS, tt), 0)
    pst = pst_ref[...][:, 0:1]
    for k in range(2):
        start = jnp.sum(jnp.where(rows_e == ints[k:k + 1], pst, 0.0), axis=0, keepdims=True)
        dest_ref[k:k + 1, :] = start.astype(I32) + ints[2 + k:3 + k]


def _dest(ints, pstart_f):
    t = ints.shape[1]
    return pl.pallas_call(
        _dest_body,
        out_shape=jax.ShapeDtypeStruct((2, t), I32),
        compiler_params=pltpu.CompilerParams(vmem_limit_bytes=VMEM_LIMIT),
        name="dest",
    )(ints, pstart_f)


def _dispatch_body(dest_ref, xp_ref, xs_ref, zero_ref, out_ref, sem, *, n_prompt_tiles, t_total):
    del zero_ref
    i = pl.program_id(0)

    def row_copy(src_ref, r, slot):
        return pltpu.make_async_copy(src_ref.at[pl.ds(r, 1)], out_ref.at[pl.ds(slot, 1)], sem)

    def scatter(src_ref, tok0, n):
        def start(r, c):
            row_copy(src_ref, r, dest_ref[tok0 + r]).start()
            row_copy(src_ref, r, dest_ref[t_total + tok0 + r]).start()
            return c

        def wait(r, c):
            row_copy(src_ref, r, dest_ref[tok0 + r]).wait()
            row_copy(src_ref, r, dest_ref[t_total + tok0 + r]).wait()
            return c

        lax.fori_loop(0, n, start, 0, unroll=8)
        lax.fori_loop(0, n, wait, 0, unroll=8)

    @pl.when(i < n_prompt_tiles)
    def _():
        scatter(xp_ref, i * TD, TD)

    @pl.when(i == n_prompt_tiles)
    def _():
        scatter(xs_ref, n_prompt_tiles * TD, xs_ref.shape[0])


def _dispatch(dest_flat, xp, xs, p_rows):
    tp, d = xp.shape
    ts = xs.shape[0]
    n_prompt_tiles = tp // TD
    zeros = jnp.zeros((p_rows, d), F32)
    grid_spec = pltpu.PrefetchScalarGridSpec(
        num_scalar_prefetch=1,
        grid=(n_prompt_tiles + 1,),
        in_specs=[
            pl.BlockSpec((TD, d), lambda i, dr: (jnp.minimum(i, n_prompt_tiles - 1), 0)),
            pl.BlockSpec((ts, d), lambda i, dr: (0, 0)),
            pl.BlockSpec(memory_space=pl.ANY),
        ],
        out_specs=pl.BlockSpec(memory_space=pl.ANY),
        scratch_shapes=[pltpu.SemaphoreType.DMA],
    )
    return pl.pallas_call(
        functools.partial(_dispatch_body, n_prompt_tiles=n_prompt_tiles, t_total=tp + ts),
        grid_spec=grid_spec,
        out_shape=jax.ShapeDtypeStruct((p_rows, d), F32),
        input_output_aliases={3: 0},
        compiler_params=pltpu.CompilerParams(
            dimension_semantics=("arbitrary",), vmem_limit_bytes=VMEM_LIMIT),
        name="dispatch",
    )(dest_flat, xp, xs, zeros)


def _experts_body(be_ref, nu_ref, nblk_ref, xs_ref, wg_hbm, wu_hbm, wd_hbm, o_ref,
                  sg, su, sd, wg_ref, wu_ref, wd_ref, wsem, slot_ref, *, layer):
    b = pl.program_id(0)
    nu = nu_ref[0]
    e = be_ref[b]

    def fetch(ex, slot):
        return (pltpu.make_async_copy(wg_hbm.at[layer, ex], sg.at[slot], wsem.at[slot, 0]),
                pltpu.make_async_copy(wu_hbm.at[layer, ex], su.at[slot], wsem.at[slot, 1]),
                pltpu.make_async_copy(wd_hbm.at[layer, ex], sd.at[slot], wsem.at[slot, 2]))

    @pl.when(b == 0)
    def _():
        slot_ref[0] = 0
        for cp in fetch(e, 0):
            cp.start()

    first_of_expert = jnp.logical_or(b == 0, e != be_ref[jnp.maximum(b - 1, 0)])

    @pl.when(jnp.logical_and(b < nu, first_of_expert))
    def _():
        slot = slot_ref[0]
        nxt = b + nblk_ref[e]

        @pl.when(nxt < nu)
        def _():
            for cp in fetch(be_ref[nxt], 1 - slot):
                cp.start()

        for cp in fetch(e, slot):
            cp.wait()
        wg_ref[...] = sg[slot].astype(BF16)
        wu_ref[...] = su[slot].astype(BF16)
        wd_ref[...] = sd[slot].astype(BF16)
        slot_ref[0] = 1 - slot

    @pl.when(b < nu)
    def _():
        xb = xs_ref[...].astype(BF16)
        h = jax.nn.silu(_dot(xb, wg_ref[...])) * _dot(xb, wu_ref[...])
        o_ref[...] = _dot(h.astype(BF16), wd_ref[...])

    @pl.when(b >= nu)
    def _():
        o_ref[...] = jnp.zeros_like(o_ref)


def _experts(blk_e, n_used, nblk, xs, wg, wu, wd, layer):
    p_rows, d = xs.shape
    de = wg.shape[3]
    nb = p_rows // BLK
    grid_spec = pltpu.PrefetchScalarGridSpec(
        num_scalar_prefetch=3,
        grid=(nb,),
        in_specs=[
            pl.BlockSpec((BLK, d), lambda b, be, nu, nk: (jnp.minimum(b, nu[0] - 1), 0)),
            pl.BlockSpec(memory_space=pl.ANY),
            pl.BlockSpec(memory_space=pl.ANY),
            pl.BlockSpec(memory_space=pl.ANY),
        ],
        out_specs=pl.BlockSpec((BLK, d), lambda b, be, nu, nk: (b, 0)),
        scratch_shapes=[
            pltpu.VMEM((2, d, de), F32), pltpu.VMEM((2, d, de), F32), pltpu.VMEM((2, de, d), F32),
            pltpu.VMEM((d, de), BF16), pltpu.VMEM((d, de), BF16), pltpu.VMEM((de, d), BF16),
            pltpu.SemaphoreType.DMA((2, 3)), pltpu.SMEM((1,), I32),
        ],
    )
    return pl.pallas_call(
        functools.partial(_experts_body, layer=layer),
        grid_spec=grid_spec,
        out_shape=jax.ShapeDtypeStruct((p_rows, d), F32),
        compiler_params=pltpu.CompilerParams(
            dimension_semantics=("arbitrary",), vmem_limit_bytes=VMEM_LIMIT),
        name="experts",
    )(blk_e, n_used, nblk, xs, wg, wu, wd)


def _combine_body(dest_ref, x_ref, w_ref, g_ref, b_ref, ob_ref, o_ref, buf, sem,
                  *, tok0, t_total, tc):
    i = pl.program_id(0)
    n = pl.num_programs(0)

    def row_copy(slot, k, r, src_row):
        return pltpu.make_async_copy(ob_ref.at[pl.ds(src_row, 1)],
                                     buf.at[slot, k, pl.ds(r, 1)], sem.at[slot])

    def for_rows(tile, slot, fn):
        base = tok0 + tile * tc

        def body(r, c):
            fn(row_copy(slot, 0, r, dest_ref[base + r]))
            fn(row_copy(slot, 1, r, dest_ref[t_total + base + r]))
            return c

        lax.fori_loop(0, tc, body, 0, unroll=8)

    @pl.when(i == 0)
    def _():
        for_rows(0, 0, lambda cp: cp.start())

    @pl.when(i + 1 < n)
    def _():
        for_rows(i + 1, (i + 1) % 2, lambda cp: cp.start())

    slot = i % 2
    for_rows(i, slot, lambda cp: cp.wait())
    w = w_ref[...]
    y = w[:, 0:1] * buf[slot, 0] + w[:, 1:2] * buf[slot, 1]
    o_ref[...] = _layer_norm(ALPHA * x_ref[...] + y, g_ref[...], b_ref[...])


def _combine(dest_flat, x, w_cols, g, b, ob, tok0, t_total):
    t, d = x.shape
    tc = min(TC, t)
    grid_spec = pltpu.PrefetchScalarGridSpec(
        num_scalar_prefetch=1,
        grid=(t // tc,),
        in_specs=[
            pl.BlockSpec((tc, d), lambda i, dr: (i, 0)),
            pl.BlockSpec((tc, 2), lambda i, dr: (i, 0)),
            pl.BlockSpec((1, d), lambda i, dr: (0, 0)),
            pl.BlockSpec((1, d), lambda i, dr: (0, 0)),
            pl.BlockSpec(memory_space=pl.ANY),
        ],
        out_specs=pl.BlockSpec((tc, d), lambda i, dr: (i, 0)),
        scratch_shapes=[pltpu.VMEM((2, 2, tc, d), F32), pltpu.SemaphoreType.DMA((2,))],
    )
    return pl.pallas_call(
        functools.partial(_combine_body, tok0=tok0, t_total=t_total, tc=tc),
        grid_spec=grid_spec,
        out_shape=jax.ShapeDtypeStruct((t, d), F32),
        compiler_params=pltpu.CompilerParams(
            dimension_semantics=("arbitrary",), vmem_limit_bytes=VMEM_LIMIT),
        name="combine",
    )(dest_flat, x, w_cols, g, b, ob)


def _moe_layer(xp, xs, w_group, w_expert, wg, wu, wd, layer, g, b):
    tp, d = xp.shape
    ts = xs.shape[0]
    t_total = tp + ts
    wrt = jnp.zeros((ROUTE_ROWS, d), F32)
    wrt = wrt.at[0:N_EXPERTS].set(w_expert.T).at[GROUP_ROW0:GROUP_ROW0 + N_GROUPS].set(w_group.T)
    wrt = wrt.astype(BF16)

    zero_cnt = jnp.zeros((N_EXPERTS, LANES), F32)
    ints_p, wts_p, cnt_p = _route(xp, wrt, zero_cnt)
    ints_s, wts_s, cnt = _route(xs, wrt, cnt_p)
    ints = jnp.concatenate([ints_p, ints_s], axis=1)
    wts = jnp.concatenate([wts_p, wts_s], axis=1)

    counts = cnt[:, 0].astype(I32)
    pcounts = (counts + BLK - 1) // BLK * BLK
    pend = jnp.cumsum(pcounts)
    pstart = pend - pcounts
    nb = (2 * t_total + N_EXPERTS * (BLK - 1) + BLK - 1) // BLK
    p_rows = nb * BLK
    n_used = (pend[-1] // BLK).astype(I32).reshape(1)
    blk_first = jnp.minimum(jnp.arange(nb, dtype=I32), n_used[0] - 1) * BLK
    blk_e = jnp.sum((pend[None, :] <= blk_first[:, None]).astype(I32), axis=1)
    nblk = pcounts // BLK

    pstart_f = jnp.broadcast_to(pstart.astype(F32)[:, None], (N_EXPERTS, LANES))
    dest_flat = _dest(ints, pstart_f).reshape(2 * t_total)

    xsorted = _dispatch(dest_flat, xp, xs, p_rows)
    ob = _experts(blk_e, n_used, nblk, xsorted, wg, wu, wd, layer)
    w_cols = wts.T
    yp = _combine(dest_flat, xp, w_cols[:tp], g, b, ob, 0, t_total)
    ys = _combine(dest_flat, xs, w_cols[tp:], g, b, ob, tp, t_total)
    return yp, ys


def kernel(x_prompt, x_sample, state_conv_a, state_conv_b, state_h, a_w_in, a_conv_w, a_w_out,
           b_w_in, b_conv_w, b_conv_b, b_gate_a_w, b_gate_a_b, b_gate_x_w, b_gate_x_b, b_lambda,
           b_w_out, ln1_g, ln1_b, ln2_g, ln2_b, moe_w_group, moe_w_expert, moe_w_gate, moe_w_up,
           moe_w_down):
    bsz, seq, d = x_prompt.shape
    n_s = x_sample.shape[0]
    row = lambda v: v.reshape(1, d)

    win, wout = a_w_in[0], a_w_out[0]
    xp, conv_a_p = _conv_a_prompt(x_prompt, win, a_conv_w[0], wout, row(ln1_g[0]), row(ln1_b[0]))
    sa = state_conv_a[0]
    xs, u_s = _conv_a_sample(x_sample.reshape(n_s, d), sa[:, 0], sa[:, 1], win, a_conv_w[0], wout,
                             row(ln1_g[0]), row(ln1_b[0]))
    conv_a_s = jnp.stack([sa[:, 1], u_s], axis=1)

    xp, xs = _moe_layer(xp.reshape(bsz * seq, d), xs, moe_w_group[0], moe_w_expert[0],
                        moe_w_gate, moe_w_up, moe_w_down, 0, row(ln2_g[0]), row(ln2_b[0]))

    win, wout = b_w_in[0], b_w_out[0]
    wcat = jnp.concatenate([b_gate_a_w[0], b_gate_x_w[0]], axis=-1).astype(BF16)
    args = (win, b_conv_w[0], row(b_conv_b[0]), wcat, row(b_gate_a_b[0]), row(b_gate_x_b[0]),
            row(b_lambda[0]), wout, row(ln1_g[1]), row(ln1_b[1]))
    xp, conv_b_p, h_p = _rglru_prompt(xp.reshape(bsz, seq, d), *args)
    sb = state_conv_b[0]
    xs, xr_s, h_s = _rglru_sample(xs, sb[:, 0], sb[:, 1], sb[:, 2], state_h[0], *args)
    conv_b_s = jnp.stack([sb[:, 1], sb[:, 2], xr_s], axis=1)

    xp, xs = _moe_layer(xp.reshape(bsz * seq, d), xs, moe_w_group[1], moe_w_expert[1],
                        moe_w_gate, moe_w_up, moe_w_down, 1, row(ln2_g[1]), row(ln2_b[1]))

    return (xp.reshape(bsz, seq, d), xs.reshape(n_s, 1, d),
            conv_a_p[None], conv_a_s[None], conv_b_p[None], conv_b_s[None],
            h_p.reshape(1, bsz, d), h_s[None])
```

```python
import functools

import jax
import jax.numpy as jnp
from jax import lax
from jax.experimental import pallas as pl
from jax.experimental.pallas import tpu as pltpu

F32 = jnp.float32
BF16 = jnp.bfloat16
I32 = jnp.int32

DEPTH = 2
N_RG_BLOCKS = 8
RG_C = 8.0
N_GROUPS = 4
EXP_PER_GROUP = 8
N_EXPERTS = N_GROUPS * EXP_PER_GROUP
ALPHA = (2.0 * DEPTH) ** 0.25
LN_EPS = 1e-5

LANES = 128
SUBLANES = 8
VMEM_LIMIT = 56 * 1024 * 1024

TS_A = 512
TS_B = 256
TT_ROUTE = 1024
TD = 256
TC = 256
BLK = 256
ROUTE_ROWS = 128
GROUP_ROW0 = N_EXPERTS
W_CHUNK = 512
PAD_BITS = BLK.bit_length() - 1
ZROWS = BLK // 2
ROW_UNROLL = 8


def _dot(a, b):
    return jnp.dot(a, b, preferred_element_type=F32)


def _load_rows(ref, m, idx=()):
    return jnp.concatenate(
        [ref[idx + (pl.ds(s, m, stride=SUBLANES), slice(None))] for s in range(SUBLANES)], axis=1)


def _store_rows(ref, v, idx=()):
    m = v.shape[0]
    for s in range(SUBLANES):
        ref[idx + (pl.ds(s, m, stride=SUBLANES), slice(None))] = v[:, s * LANES:(s + 1) * LANES]


def _tile_of_row(ref, r):
    return ref.at[pl.ds(pl.multiple_of(r * SUBLANES, SUBLANES), SUBLANES)]


def _load_weight_bf16(w_hbm, w_bf, stage, sem):
    nch = w_hbm.shape[1] // W_CHUNK

    def chunk_copy(c):
        return pltpu.make_async_copy(w_hbm.at[:, pl.ds(c * W_CHUNK, W_CHUNK)],
                                     stage.at[c % 2], sem.at[c % 2])

    chunk_copy(0).start()
    for c in range(nch):
        if c + 1 < nch:
            chunk_copy(c + 1).start()
        chunk_copy(c).wait()
        w_bf[:, c * W_CHUNK:(c + 1) * W_CHUNK] = stage[c % 2].astype(BF16)


def _weight_scratch(k, *ns):
    return ([pltpu.VMEM((k, n), BF16) for n in ns]
            + [pltpu.VMEM((2, k, W_CHUNK), F32), pltpu.SemaphoreType.DMA((2,))])


def _layer_norm(r, g, b):
    mu = jnp.mean(r, axis=-1, keepdims=True)
    d = r - mu
    var = jnp.mean(d * d, axis=-1, keepdims=True)
    return d * lax.rsqrt(var + LN_EPS) * g + b


def _shift_rows(v, k, prev8):
    rolled = pltpu.roll(v, k, axis=0)
    rows8 = lax.broadcasted_iota(I32, (SUBLANES, v.shape[1]), 0)
    first = jnp.where(rows8 < k, pltpu.roll(prev8, k, axis=0), rolled[0:SUBLANES])
    return jnp.concatenate([first, rolled[SUBLANES:]], axis=0)


def _softplus(v):
    return jnp.maximum(v, 0.0) + jnp.log1p(jnp.exp(-jnp.abs(v)))


def _rglru_coeffs(xc, wcat_ref, gab, gxb, lam):
    d = xc.shape[1]
    blk = d // N_RG_BLOCKS
    xcb = xc.astype(BF16)
    rs, is_ = [], []
    for n in range(N_RG_BLOCKS):
        o = _dot(xcb[:, n * blk:(n + 1) * blk], wcat_ref[n])
        rs.append(o[:, :blk])
        is_.append(o[:, blk:])
    r = jax.nn.sigmoid(jnp.concatenate(rs, axis=1) + gab)
    i = jax.nn.sigmoid(jnp.concatenate(is_, axis=1) + gxb)
    log_a = -RG_C * r * _softplus(-lam)
    a = jnp.exp(log_a)
    mult = jnp.sqrt(-jnp.tanh(log_a) * (a * a + 1.0))
    return a, mult * (i * xc)


def _scan_rows(a, b, h0):
    m = a.shape[0]
    sub = lax.broadcasted_iota(I32, a.shape, 0) % SUBLANES
    for k in (1, 2, 4):
        a_sh = pltpu.roll(a, k, axis=0)
        b_sh = pltpu.roll(b, k, axis=0)
        keep = sub >= k
        b = jnp.where(keep, a * b_sh + b, b)
        a = jnp.where(keep, a * a_sh, a)
    outs = []
    h = h0
    for g in range(m // SUBLANES):
        hg = a[g * SUBLANES:(g + 1) * SUBLANES] * h + b[g * SUBLANES:(g + 1) * SUBLANES]
        outs.append(hg)
        h = hg[SUBLANES - 1:SUBLANES]
    return jnp.concatenate(outs, axis=0), h


def _conv_a_prompt_body(x_ref, win_hbm, cw_ref, wout_hbm, g_ref, b_ref,
                        o_ref, buf_ref, carry, win_ref, wout_ref, stage, wsem):
    s = pl.program_id(1)

    @pl.when(jnp.logical_and(pl.program_id(0) == 0, s == 0))
    def _():
        _load_weight_bf16(win_hbm, win_ref, stage, wsem)
        _load_weight_bf16(wout_hbm, wout_ref, stage, wsem)

    @pl.when(s == 0)
    def _():
        carry[...] = jnp.zeros_like(carry)

    x = x_ref[0]
    d = x.shape[1]
    bcx = _dot(x.astype(BF16), win_ref[...])
    gb, gc, xh = bcx[:, :d], bcx[:, d:2 * d], bcx[:, 2 * d:]
    u = gc * xh
    prev = carry[...]
    cw = cw_ref[...]
    conv = (cw[0:1] * _shift_rows(u, 2, prev) + cw[1:2] * _shift_rows(u, 1, prev)
            + cw[2:3] * u)
    y = _dot((gb * conv).astype(BF16), wout_ref[...])
    _store_rows(o_ref, _layer_norm(ALPHA * x + y, g_ref[...], b_ref[...]), (0,))
    ts = u.shape[0]
    carry[...] = u[ts - SUBLANES:ts]

    @pl.when(s == pl.num_programs(1) - 1)
    def _():
        buf_ref[0] = u[ts - 2:ts]


def _conv_a_prompt(x, win, cw, wout, g, b):
    bsz, seq, d = x.shape
    ts = min(TS_A, seq)
    grid = (bsz, seq // ts)
    return pl.pallas_call(
        _conv_a_prompt_body,
        grid=grid,
        in_specs=[
            pl.BlockSpec((1, ts, d), lambda i, j: (i, j, 0)),
            pl.BlockSpec(memory_space=pl.ANY),
            pl.BlockSpec((3, d), lambda i, j: (0, 0)),
            pl.BlockSpec(memory_space=pl.ANY),
            pl.BlockSpec((1, d), lambda i, j: (0, 0)),
            pl.BlockSpec((1, d), lambda i, j: (0, 0)),
        ],
        out_specs=[
            pl.BlockSpec((1, ts * SUBLANES, LANES), lambda i, j: (i, j, 0)),
            pl.BlockSpec((1, 2, d), lambda i, j: (i, 0, 0)),
        ],
        out_shape=[
            jax.ShapeDtypeStruct((bsz, seq * SUBLANES, LANES), F32),
            jax.ShapeDtypeStruct((bsz, 2, d), F32),
        ],
        scratch_shapes=[pltpu.VMEM((SUBLANES, d), F32)] + _weight_scratch(d, 3 * d, d),
        compiler_params=pltpu.CompilerParams(
            dimension_semantics=("arbitrary", "arbitrary"), vmem_limit_bytes=VMEM_LIMIT),
        name="conv_a_prompt",
    )(x, win, cw, wout, g, b)


def _conv_a_sample_body(x_ref, s0_ref, s1_ref, win_ref, cw_ref, wout_ref, g_ref, b_ref,
                        o_ref, u_ref):
    x = x_ref[...]
    d = x.shape[1]
    bcx = _dot(x.astype(BF16), win_ref[...].astype(BF16))
    gb, gc, xh = bcx[:, :d], bcx[:, d:2 * d], bcx[:, 2 * d:]
    u = gc * xh
    cw = cw_ref[...]
    conv = cw[0:1] * s0_ref[...] + cw[1:2] * s1_ref[...] + cw[2:3] * u
    y = _dot((gb * conv).astype(BF16), wout_ref[...].astype(BF16))
    _store_rows(o_ref, _layer_norm(ALPHA * x + y, g_ref[...], b_ref[...]))
    u_ref[...] = u


def _conv_a_sample(x, s0, s1, win, cw, wout, g, b):
    n, d = x.shape
    return pl.pallas_call(
        _conv_a_sample_body,
        out_shape=[jax.ShapeDtypeStruct((n * SUBLANES, LANES), F32),
                   jax.ShapeDtypeStruct((n, d), F32)],
        compiler_params=pltpu.CompilerParams(vmem_limit_bytes=VMEM_LIMIT),
        name="conv_a_sample",
    )(x, s0, s1, win, cw, wout, g, b)


def _rglru_prompt_body(x_ref, win_hbm, cw_ref, cb_ref, wcat_ref, gab_ref, gxb_ref, lam_ref,
                       wout_hbm, g_ref, b_ref, o_ref, buf_ref, hl_ref, xcarry, hcarry,
                       win_ref, wout_ref, stage, wsem):
    s = pl.program_id(1)

    @pl.when(jnp.logical_and(pl.program_id(0) == 0, s == 0))
    def _():
        _load_weight_bf16(win_hbm, win_ref, stage, wsem)
        _load_weight_bf16(wout_hbm, wout_ref, stage, wsem)

    @pl.when(s == 0)
    def _():
        xcarry[...] = jnp.zeros_like(xcarry)
        hcarry[...] = jnp.zeros_like(hcarry)

    x = x_ref[0]
    d = x.shape[1]
    gx = _dot(x.astype(BF16), win_ref[...])
    gate, xr = gx[:, :d], gx[:, d:]
    prev = xcarry[...]
    cw = cw_ref[...]
    xc = (cw[0:1] * _shift_rows(xr, 3, prev) + cw[1:2] * _shift_rows(xr, 2, prev)
          + cw[2:3] * _shift_rows(xr, 1, prev) + cw[3:4] * xr) + cb_ref[...]
    a, bt = _rglru_coeffs(xc, wcat_ref, gab_ref[...], gxb_ref[...], lam_ref[...])
    hs, hlast = _scan_rows(a, bt, hcarry[0:1])
    y = _dot((jax.nn.gelu(gate, approximate=True) * hs).astype(BF16), wout_ref[...])
    _store_rows(o_ref, _layer_norm(ALPHA * x + y, g_ref[...], b_ref[...]), (0,))
    ts = xr.shape[0]
    xcarry[...] = xr[ts - SUBLANES:ts]
    hcarry[...] = jnp.broadcast_to(hlast, hcarry.shape)

    @pl.when(s == pl.num_programs(1) - 1)
    def _():
        buf_ref[0] = xr[ts - 3:ts]
        hl_ref[0] = hlast


def _rglru_prompt(x, win, cw, cb, wcat, gab, gxb, lam, wout, g, b):
    bsz, seq, d = x.shape
    ts = min(TS_B, seq)
    grid = (bsz, seq // ts)
    blk = d // N_RG_BLOCKS
    const2 = lambda i, j: (0, 0)
    return pl.pallas_call(
        _rglru_prompt_body,
        grid=grid,
        in_specs=[
            pl.BlockSpec((1, ts, d), lambda i, j: (i, j, 0)),
            pl.BlockSpec(memory_space=pl.ANY),
            pl.BlockSpec((4, d), const2),
            pl.BlockSpec((1, d), const2),
            pl.BlockSpec((N_RG_BLOCKS, blk, 2 * blk), lambda i, j: (0, 0, 0)),
            pl.BlockSpec((1, d), const2),
            pl.BlockSpec((1, d), const2),
            pl.BlockSpec((1, d), const2),
            pl.BlockSpec(memory_space=pl.ANY),
            pl.BlockSpec((1, d), const2),
            pl.BlockSpec((1, d), const2),
        ],
        out_specs=[
            pl.BlockSpec((1, ts * SUBLANES, LANES), lambda i, j: (i, j, 0)),
            pl.BlockSpec((1, 3, d), lambda i, j: (i, 0, 0)),
            pl.BlockSpec((1, 1, d), lambda i, j: (i, 0, 0)),
        ],
        out_shape=[
            jax.ShapeDtypeStruct((bsz, seq * SUBLANES, LANES), F32),
            jax.ShapeDtypeStruct((bsz, 3, d), F32),
            jax.ShapeDtypeStruct((bsz, 1, d), F32),
        ],
        scratch_shapes=([pltpu.VMEM((SUBLANES, d), F32), pltpu.VMEM((SUBLANES, d), F32)]
                        + _weight_scratch(d, 2 * d, d)),
        compiler_params=pltpu.CompilerParams(
            dimension_semantics=("arbitrary", "arbitrary"), vmem_limit_bytes=VMEM_LIMIT),
        name="rglru_prompt",
    )(x, win, cw, cb, wcat, gab, gxb, lam, wout, g, b)


def _rglru_sample_body(x_ref, s0_ref, s1_ref, s2_ref, h0_ref, win_ref, cw_ref, cb_ref, wcat_ref,
                       gab_ref, gxb_ref, lam_ref, wout_ref, g_ref, b_ref,
                       o_ref, xr_ref, h_ref):
    x = x_ref[...]
    d = x.shape[1]
    gx = _dot(x.astype(BF16), win_ref[...].astype(BF16))
    gate, xr = gx[:, :d], gx[:, d:]
    cw = cw_ref[...]
    xc = (cw[0:1] * s0_ref[...] + cw[1:2] * s1_ref[...] + cw[2:3] * s2_ref[...]
          + cw[3:4] * xr) + cb_ref[...]
    a, bt = _rglru_coeffs(xc, wcat_ref, gab_ref[...], gxb_ref[...], lam_ref[...])
    h = a * h0_ref[...] + bt
    y = _dot((jax.nn.gelu(gate, approximate=True) * h).astype(BF16),
             wout_ref[...].astype(BF16))
    _store_rows(o_ref, _layer_norm(ALPHA * x + y, g_ref[...], b_ref[...]))
    xr_ref[...] = xr
    h_ref[...] = h


def _rglru_sample(x, s0, s1, s2, h0, win, cw, cb, wcat, gab, gxb, lam, wout, g, b):
    n, d = x.shape
    return pl.pallas_call(
        _rglru_sample_body,
        out_shape=[jax.ShapeDtypeStruct((n * SUBLANES, LANES), F32),
                   jax.ShapeDtypeStruct((n, d), F32), jax.ShapeDtypeStruct((n, d), F32)],
        compiler_params=pltpu.CompilerParams(vmem_limit_bytes=VMEM_LIMIT),
        name="rglru_sample",
    )(x, s0, s1, s2, h0, win, cw, cb, wcat, gab, gxb, lam, wout, g, b)


def _first_argmax(v, rows):
    m = jnp.max(v, axis=0, keepdims=True)
    idx = jnp.min(jnp.where(v == m, rows, v.shape[0]), axis=0, keepdims=True)
    return m, idx


def _route_body(x_ref, wrt_ref, tri_ref, cin_ref, ints_ref, wts_ref, cnt_ref, carry):
    @pl.when(pl.program_id(0) == 0)
    def _():
        carry[...] = cin_ref[...]

    tt = x_ref.shape[0] // SUBLANES
    xb = _load_rows(x_ref, tt).astype(BF16)
    lt = lax.dot_general(wrt_ref[...], xb, (((1,), (1,)), ((), ())), preferred_element_type=F32)
    rows8 = lax.broadcasted_iota(I32, (SUBLANES, tt), 0)
    neg_inf = jnp.float32(-jnp.inf)

    gl = jnp.where(rows8 < N_GROUPS, lt[GROUP_ROW0:GROUP_ROW0 + SUBLANES], neg_inf)
    gmax, gidx = _first_argmax(gl, rows8)
    gw = 1.0 / jnp.sum(jnp.exp(gl - gmax), axis=0, keepdims=True)

    el = lt[0:EXP_PER_GROUP]
    for g in range(1, N_GROUPS):
        el = jnp.where(gidx == g, lt[g * EXP_PER_GROUP:(g + 1) * EXP_PER_GROUP], el)
    emax, i1 = _first_argmax(el, rows8)
    el2 = jnp.where(rows8 == i1, neg_inf, el)
    m2, i2 = _first_argmax(el2, rows8)
    psum = jnp.sum(jnp.exp(el - emax), axis=0, keepdims=True)
    ep1 = 1.0 / psum
    ep2 = jnp.exp(m2 - emax) / psum
    tot = ep1 + ep2
    wa = gw * (ep1 / tot)
    wb = gw * (ep2 / tot)
    ea = gidx * EXP_PER_GROUP + i1
    eb = gidx * EXP_PER_GROUP + i2

    rows_e = lax.broadcasted_iota(I32, (N_EXPERTS, tt), 0)
    oha = rows_e == ea
    ohb = rows_e == eb
    oh = jnp.where(oha | ohb, 1.0, 0.0)
    base = carry[...][:, 0:1]
    excl = _dot(oh.astype(BF16), tri_ref[...]) + base
    ra = jnp.sum(jnp.where(oha, excl, 0.0), axis=0, keepdims=True)
    rb = jnp.sum(jnp.where(ohb, excl, 0.0), axis=0, keepdims=True)
    new = carry[...] + jnp.sum(oh, axis=1, keepdims=True)
    carry[...] = new
    cnt_ref[...] = new

    ints_ref[0:1, :] = ea
    ints_ref[1:2, :] = eb
    ints_ref[2:3, :] = ra.astype(I32)
    ints_ref[3:4, :] = rb.astype(I32)
    wts_ref[0:1, :] = wa
    wts_ref[1:2, :] = wb


def _route(x, wrt, cin):
    t = x.shape[0] // SUBLANES
    d = wrt.shape[1]
    tt = min(TT_ROUTE, t)
    tri = (jnp.arange(tt)[:, None] < jnp.arange(tt)[None, :]).astype(BF16)
    return pl.pallas_call(
        _route_body,
        grid=(t // tt,),
        in_specs=[
            pl.BlockSpec((tt * SUBLANES, LANES), lambda i: (i, 0)),
            pl.BlockSpec((ROUTE_ROWS, d), lambda i: (0, 0)),
            pl.BlockSpec((tt, tt), lambda i: (0, 0)),
            pl.BlockSpec((N_EXPERTS, LANES), lambda i: (0, 0)),
        ],
        out_specs=[
            pl.BlockSpec((4, tt), lambda i: (0, i)),
            pl.BlockSpec((2, tt), lambda i: (0, i)),
            pl.BlockSpec((N_EXPERTS, LANES), lambda i: (0, 0)),
        ],
        out_shape=[
            jax.ShapeDtypeStruct((4, t), I32),
            jax.ShapeDtypeStruct((2, t), F32),
            jax.ShapeDtypeStruct((N_EXPERTS, LANES), F32),
        ],
        scratch_shapes=[pltpu.VMEM((N_EXPERTS, LANES), F32)],
        compiler_params=pltpu.CompilerParams(
            dimension_semantics=("arbitrary",), vmem_limit_bytes=VMEM_LIMIT),
        name="route",
    )(x, wrt, tri, cin)


def _dest_body(ints_ref, pst_ref, dest_ref):
    ints = ints_ref[...]
    tt = ints.shape[1]
    rows_e = lax.broadcasted_iota(I32, (N_EXPERTS, tt), 0)
    pst = pst_ref[...][:, 0:1]
    for k in range(2):
        start = jnp.sum(jnp.where(rows_e == ints[k:k + 1], pst, 0.0), axis=0, keepdims=True)
        dest_ref[k:k + 1, :] = start.astype(I32) + ints[2 + k:3 + k]


def _dest(ints, pstart_f):
    t = ints.shape[1]
    return pl.pallas_call(
        _dest_body,
        out_shape=jax.ShapeDtypeStruct((2, t), I32),
        compiler_params=pltpu.CompilerParams(vmem_limit_bytes=VMEM_LIMIT),
        name="dest",
    )(ints, pstart_f)


def _dispatch_body(dest_ref, padpos_ref, padlen_ref, nu_ref, xp_ref, xs_ref, out_ref, zbuf, sem, zsem,
                   *, n_prompt_tiles, n_sample, t_total, nb):
    i = pl.program_id(0)

    def for_rows(src_ref, tok0, n, fn):
        def body(j, c):
            rows = [j * ROW_UNROLL + u for u in range(ROW_UNROLL)]
            slots = [[dest_ref[k * t_total + tok0 + r] for k in range(2)] for r in rows]
            for r, rs in zip(rows, slots):
                for k in range(2):
                    fn(pltpu.make_async_copy(_tile_of_row(src_ref, r),
                                             _tile_of_row(out_ref, rs[k]), sem), k)
            return c

        lax.fori_loop(0, n // ROW_UNROLL, body, 0)

    def start(cp, k):
        cp.start(priority=k)

    def wait(cp, k):
        cp.wait()

    @pl.when(i < n_prompt_tiles)
    def _():
        for_rows(xp_ref, i * TD, TD, start)
        for_rows(xp_ref, i * TD, TD, wait)

    @pl.when(i == n_prompt_tiles)
    def _():
        for_rows(xs_ref, n_prompt_tiles * TD, n_sample, start)

        zbuf[...] = jnp.zeros_like(zbuf)

        def pad_copies(e, fn):
            pos, ln = padpos_ref[e], padlen_ref[e]
            for k in reversed(range(PAD_BITS)):
                size = 1 << k
                off = (ln >> (k + 1)) << (k + 1)

                @pl.when((ln & size) != 0)
                def _():
                    first = pl.multiple_of((pos + off) * SUBLANES, SUBLANES)
                    fn(pltpu.make_async_copy(zbuf.at[pl.ds(0, size * SUBLANES)],
                                             out_ref.at[pl.ds(first, size * SUBLANES)], zsem))

        def tail_copies(b, fn):
            for h in range(BLK // ZROWS):
                first = pl.multiple_of((b * BLK + h * ZROWS) * SUBLANES, SUBLANES)
                fn(pltpu.make_async_copy(zbuf, out_ref.at[pl.ds(first, ZROWS * SUBLANES)], zsem))

        def fill(fn):
            def pad_body(e, c):
                pad_copies(e, fn)
                return c

            def tail_body(b, c):
                tail_copies(b, fn)
                return c

            lax.fori_loop(0, N_EXPERTS, pad_body, 0)
            lax.fori_loop(nu_ref[0], nb, tail_body, 0)

        fill(lambda cp: cp.start())
        fill(lambda cp: cp.wait())
        for_rows(xs_ref, n_prompt_tiles * TD, n_sample, wait)


def _dispatch(dest_flat, pad_pos, pad_len, n_used, xp, xs, p_rows):
    tp = xp.shape[0] // SUBLANES
    ts = xs.shape[0] // SUBLANES
    n_prompt_tiles = tp // TD
    grid_spec = pltpu.PrefetchScalarGridSpec(
        num_scalar_prefetch=4,
        grid=(n_prompt_tiles + 1,),
        in_specs=[
            pl.BlockSpec((TD * SUBLANES, LANES),
                         lambda i, *_: (jnp.minimum(i, n_prompt_tiles - 1), 0)),
            pl.BlockSpec((ts * SUBLANES, LANES), lambda i, *_: (0, 0)),
        ],
        out_specs=pl.BlockSpec(memory_space=pl.ANY),
        scratch_shapes=[pltpu.VMEM((ZROWS * SUBLANES, LANES), F32),
                        pltpu.SemaphoreType.DMA, pltpu.SemaphoreType.DMA],
    )
    return pl.pallas_call(
        functools.partial(_dispatch_body, n_prompt_tiles=n_prompt_tiles, n_sample=ts,
                          t_total=tp + ts, nb=p_rows // BLK),
        grid_spec=grid_spec,
        out_shape=jax.ShapeDtypeStruct((p_rows * SUBLANES, LANES), F32),
        compiler_params=pltpu.CompilerParams(
            dimension_semantics=("arbitrary",), vmem_limit_bytes=VMEM_LIMIT),
        name="dispatch",
    )(dest_flat, pad_pos, pad_len, n_used, xp, xs)


def _experts_body(be_ref, nu_ref, nblk_ref, xs_ref, wg_hbm, wu_hbm, wd_hbm, o_ref,
                  sg, su, sd, wg_ref, wu_ref, wd_ref, wsem, slot_ref, *, layer):
    b = pl.program_id(0)
    nu = nu_ref[0]
    e = be_ref[b]

    def fetch(ex, slot):
        return (pltpu.make_async_copy(wg_hbm.at[layer, ex], sg.at[slot], wsem.at[slot, 0]),
                pltpu.make_async_copy(wu_hbm.at[layer, ex], su.at[slot], wsem.at[slot, 1]),
                pltpu.make_async_copy(wd_hbm.at[layer, ex], sd.at[slot], wsem.at[slot, 2]))

    @pl.when(b == 0)
    def _():
        slot_ref[0] = 0
        for cp in fetch(e, 0):
            cp.start()

    first_of_expert = jnp.logical_or(b == 0, e != be_ref[jnp.maximum(b - 1, 0)])

    @pl.when(jnp.logical_and(b < nu, first_of_expert))
    def _():
        slot = slot_ref[0]
        nxt = b + nblk_ref[e]

        @pl.when(nxt < nu)
        def _():
            for cp in fetch(be_ref[nxt], 1 - slot):
                cp.start()

        for cp in fetch(e, slot):
            cp.wait()
        wg_ref[...] = sg[slot].astype(BF16)
        wu_ref[...] = su[slot].astype(BF16)
        wd_ref[...] = sd[slot].astype(BF16)
        slot_ref[0] = 1 - slot

    @pl.when(b < nu)
    def _():
        xb = _load_rows(xs_ref, BLK).astype(BF16)
        h = jax.nn.silu(_dot(xb, wg_ref[...])) * _dot(xb, wu_ref[...])
        _store_rows(o_ref, _dot(h.astype(BF16), wd_ref[...]))

    @pl.when(b >= nu)
    def _():
        o_ref[...] = jnp.zeros_like(o_ref)


def _experts(blk_e, n_used, nblk, xs, wg, wu, wd, layer):
    p_rows = xs.shape[0] // SUBLANES
    d, de = wg.shape[2], wg.shape[3]
    nb = p_rows // BLK
    grid_spec = pltpu.PrefetchScalarGridSpec(
        num_scalar_prefetch=3,
        grid=(nb,),
        in_specs=[
            pl.BlockSpec((BLK * SUBLANES, LANES),
                         lambda b, be, nu, nk: (jnp.minimum(b, nu[0] - 1), 0)),
            pl.BlockSpec(memory_space=pl.ANY),
            pl.BlockSpec(memory_space=pl.ANY),
            pl.BlockSpec(memory_space=pl.ANY),
        ],
        out_specs=pl.BlockSpec((BLK * SUBLANES, LANES), lambda b, be, nu, nk: (b, 0)),
        scratch_shapes=[
            pltpu.VMEM((2, d, de), F32), pltpu.VMEM((2, d, de), F32), pltpu.VMEM((2, de, d), F32),
            pltpu.VMEM((d, de), BF16), pltpu.VMEM((d, de), BF16), pltpu.VMEM((de, d), BF16),
            pltpu.SemaphoreType.DMA((2, 3)), pltpu.SMEM((1,), I32),
        ],
    )
    return pl.pallas_call(
        functools.partial(_experts_body, layer=layer),
        grid_spec=grid_spec,
        out_shape=jax.ShapeDtypeStruct((p_rows * SUBLANES, LANES), F32),
        compiler_params=pltpu.CompilerParams(
            dimension_semantics=("arbitrary",), vmem_limit_bytes=VMEM_LIMIT),
        name="experts",
    )(blk_e, n_used, nblk, xs, wg, wu, wd)


def _combine_body(dest_ref, x_ref, w_ref, g_ref, b_ref, ob_ref, o_ref, buf, sem,
                  *, tok0, t_total, tc):
    i = pl.program_id(0)
    n = pl.num_programs(0)

    def for_rows(tile, slot, fn):
        base = tok0 + tile * tc

        def body(j, c):
            rows = [j * ROW_UNROLL + u for u in range(ROW_UNROLL)]
            srcs = [[dest_ref[k * t_total + base + r] for k in range(2)] for r in rows]
            for r, rs in zip(rows, srcs):
                for k in range(2):
                    fn(pltpu.make_async_copy(_tile_of_row(ob_ref, rs[k]),
                                             _tile_of_row(buf.at[slot, k], r), sem.at[slot]), k)
            return c

        lax.fori_loop(0, tc // ROW_UNROLL, body, 0)

    def start(cp, k):
        cp.start(priority=k)

    def wait(cp, k):
        cp.wait()

    @pl.when(i == 0)
    def _():
        for_rows(0, 0, start)

    @pl.when(i + 1 < n)
    def _():
        for_rows(i + 1, (i + 1) % 2, start)

    slot = i % 2
    for_rows(i, slot, wait)
    w = w_ref[...]
    y = w[:, 0:1] * _load_rows(buf, tc, (slot, 0)) + w[:, 1:2] * _load_rows(buf, tc, (slot, 1))
    o_ref[...] = _layer_norm(ALPHA * _load_rows(x_ref, tc) + y, g_ref[...], b_ref[...])


def _combine(dest_flat, x, w_cols, g, b, ob, tok0, t_total):
    t = x.shape[0] // SUBLANES
    d = g.shape[1]
    tc = min(TC, t)
    grid_spec = pltpu.PrefetchScalarGridSpec(
        num_scalar_prefetch=1,
        grid=(t // tc,),
        in_specs=[
            pl.BlockSpec((tc * SUBLANES, LANES), lambda i, dr: (i, 0)),
            pl.BlockSpec((tc, 2), lambda i, dr: (i, 0)),
            pl.BlockSpec((1, d), lambda i, dr: (0, 0)),
            pl.BlockSpec((1, d), lambda i, dr: (0, 0)),
            pl.BlockSpec(memory_space=pl.ANY),
        ],
        out_specs=pl.BlockSpec((tc, d), lambda i, dr: (i, 0)),
        scratch_shapes=[pltpu.VMEM((2, 2, tc * SUBLANES, LANES), F32),
                        pltpu.SemaphoreType.DMA((2,))],
    )
    return pl.pallas_call(
        functools.partial(_combine_body, tok0=tok0, t_total=t_total, tc=tc),
        grid_spec=grid_spec,
        out_shape=jax.ShapeDtypeStruct((t, d), F32),
        compiler_params=pltpu.CompilerParams(
            dimension_semantics=("arbitrary",), vmem_limit_bytes=VMEM_LIMIT),
        name="combine",
    )(dest_flat, x, w_cols, g, b, ob)


def _moe_layer(xp, xs, w_group, w_expert, wg, wu, wd, layer, g, b):
    tp, ts = xp.shape[0] // SUBLANES, xs.shape[0] // SUBLANES
    d = g.shape[1]
    t_total = tp + ts
    wrt = jnp.zeros((ROUTE_ROWS, d), F32)
    wrt = wrt.at[0:N_EXPERTS].set(w_expert.T).at[GROUP_ROW0:GROUP_ROW0 + N_GROUPS].set(w_group.T)
    wrt = wrt.astype(BF16)

    zero_cnt = jnp.zeros((N_EXPERTS, LANES), F32)
    ints_p, wts_p, cnt_p = _route(xp, wrt, zero_cnt)
    ints_s, wts_s, cnt = _route(xs, wrt, cnt_p)
    ints = jnp.concatenate([ints_p, ints_s], axis=1)
    wts = jnp.concatenate([wts_p, wts_s], axis=1)

    counts = cnt[:, 0].astype(I32)
    pcounts = (counts + BLK - 1) // BLK * BLK
    pend = jnp.cumsum(pcounts)
    pstart = pend - pcounts
    nb = (2 * t_total + N_EXPERTS * (BLK - 1) + BLK - 1) // BLK
    p_rows = nb * BLK
    n_used = (pend[-1] // BLK).astype(I32).reshape(1)
    blk_first = jnp.minimum(jnp.arange(nb, dtype=I32), n_used[0] - 1) * BLK
    blk_e = jnp.sum((pend[None, :] <= blk_first[:, None]).astype(I32), axis=1)
    nblk = pcounts // BLK

    pstart_f = jnp.broadcast_to(pstart.astype(F32)[:, None], (N_EXPERTS, LANES))
    dest_flat = _dest(ints, pstart_f).reshape(2 * t_total)

    xsorted = _dispatch(dest_flat, pstart + counts, pcounts - counts, n_used, xp, xs, p_rows)
    ob = _experts(blk_e, n_used, nblk, xsorted, wg, wu, wd, layer)
    w_cols = wts.T
    yp = _combine(dest_flat, xp, w_cols[:tp], g, b, ob, 0, t_total)
    ys = _combine(dest_flat, xs, w_cols[tp:], g, b, ob, tp, t_total)
    return yp, ys


def kernel(x_prompt, x_sample, state_conv_a, state_conv_b, state_h, a_w_in, a_conv_w, a_w_out,
           b_w_in, b_conv_w, b_conv_b, b_gate_a_w, b_gate_a_b, b_gate_x_w, b_gate_x_b, b_lambda,
           b_w_out, ln1_g, ln1_b, ln2_g, ln2_b, moe_w_group, moe_w_expert, moe_w_gate, moe_w_up,
           moe_w_down):
    bsz, seq, d = x_prompt.shape
    n_s = x_sample.shape[0]
    row = lambda v: v.reshape(1, d)

    win, wout = a_w_in[0], a_w_out[0]
    xp, conv_a_p = _conv_a_prompt(x_prompt, win, a_conv_w[0], wout, row(ln1_g[0]), row(ln1_b[0]))
    sa = state_conv_a[0]
    xs, u_s = _conv_a_sample(x_sample.reshape(n_s, d), sa[:, 0], sa[:, 1], win, a_conv_w[0], wout,
                             row(ln1_g[0]), row(ln1_b[0]))
    conv_a_s = jnp.stack([sa[:, 1], u_s], axis=1)

    xp, xs = _moe_layer(xp.reshape(bsz * seq * SUBLANES, LANES), xs, moe_w_group[0], moe_w_expert[0],
                        moe_w_gate, moe_w_up, moe_w_down, 0, row(ln2_g[0]), row(ln2_b[0]))

    win, wout = b_w_in[0], b_w_out[0]
    wcat = jnp.concatenate([b_gate_a_w[0], b_gate_x_w[0]], axis=-1).astype(BF16)
    args = (win, b_conv_w[0], row(b_conv_b[0]), wcat, row(b_gate_a_b[0]), row(b_gate_x_b[0]),
            row(b_lambda[0]), wout, row(ln1_g[1]), row(ln1_b[1]))
    xp, conv_b_p, h_p = _rglru_prompt(xp.reshape(bsz, seq, d), *args)
    sb = state_conv_b[0]
    xs, xr_s, h_s = _rglru_sample(xs, sb[:, 0], sb[:, 1], sb[:, 2], state_h[0], *args)
    conv_b_s = jnp.stack([sb[:, 1], sb[:, 2], xr_s], axis=1)

    xp, xs = _moe_layer(xp.reshape(bsz * seq * SUBLANES, LANES), xs, moe_w_group[1], moe_w_expert[1],
                        moe_w_gate, moe_w_up, moe_w_down, 1, row(ln2_g[1]), row(ln2_b[1]))

    return (xp.reshape(bsz, seq, d), xs.reshape(n_s, 1, d),
            conv_a_p[None], conv_a_s[None], conv_b_p[None], conv_b_s[None],
            h_p.reshape(1, bsz, d), h_s[None])
```

```python
import functools

import jax
import jax.numpy as jnp
from jax import lax
from jax.experimental import pallas as pl
from jax.experimental.pallas import tpu as pltpu

F32 = jnp.float32
BF16 = jnp.bfloat16
I32 = jnp.int32

DEPTH = 2
N_RG_BLOCKS = 8
RG_C = 8.0
N_GROUPS = 4
EXP_PER_GROUP = 8
N_EXPERTS = N_GROUPS * EXP_PER_GROUP
ALPHA = (2.0 * DEPTH) ** 0.25
LN_EPS = 1e-5

LANES = 128
SUBLANES = 8
VMEM_LIMIT = 56 * 1024 * 1024

TS_A = 512
TS_B = 256
TT_ROUTE = 1024
TC = 256
BLK = 256
ROUTE_ROWS = 128
GROUP_ROW0 = N_EXPERTS
W_CHUNK = 512
ROW_UNROLL = 8

PLANE_BITS = 15
TOKEN_MASK = (1 << PLANE_BITS) - 1
PAD_BASE = 2 << PLANE_BITS
PRIME_BLOCKS = 2
SPARE_ROW0 = PAD_BASE + 2 * BLK
Y_ROWS = SPARE_ROW0 + PRIME_BLOCKS * BLK


def _dot(a, b):
    return jnp.dot(a, b, preferred_element_type=F32)


def _load_rows(ref, m, idx=()):
    return jnp.concatenate(
        [ref[idx + (pl.ds(s, m, stride=SUBLANES), slice(None))] for s in range(SUBLANES)], axis=1)


def _store_rows(ref, v, idx=()):
    m = v.shape[0]
    for s in range(SUBLANES):
        ref[idx + (pl.ds(s, m, stride=SUBLANES), slice(None))] = v[:, s * LANES:(s + 1) * LANES]


def _tile_of_row(ref, r):
    return ref.at[pl.ds(pl.multiple_of(r * SUBLANES, SUBLANES), SUBLANES)]


def _load_weight_bf16(w_hbm, w_bf, stage, sem):
    nch = w_hbm.shape[1] // W_CHUNK

    def chunk_copy(c):
        return pltpu.make_async_copy(w_hbm.at[:, pl.ds(c * W_CHUNK, W_CHUNK)],
                                     stage.at[c % 2], sem.at[c % 2])

    chunk_copy(0).start()
    for c in range(nch):
        if c + 1 < nch:
            chunk_copy(c + 1).start()
        chunk_copy(c).wait()
        w_bf[:, c * W_CHUNK:(c + 1) * W_CHUNK] = stage[c % 2].astype(BF16)


def _weight_scratch(k, *ns):
    return ([pltpu.VMEM((k, n), BF16) for n in ns]
            + [pltpu.VMEM((2, k, W_CHUNK), F32), pltpu.SemaphoreType.DMA((2,))])


def _layer_norm(r, g, b):
    mu = jnp.mean(r, axis=-1, keepdims=True)
    d = r - mu
    var = jnp.mean(d * d, axis=-1, keepdims=True)
    return d * lax.rsqrt(var + LN_EPS) * g + b


def _shift_rows(v, k, prev8):
    rolled = pltpu.roll(v, k, axis=0)
    rows8 = lax.broadcasted_iota(I32, (SUBLANES, v.shape[1]), 0)
    first = jnp.where(rows8 < k, pltpu.roll(prev8, k, axis=0), rolled[0:SUBLANES])
    return jnp.concatenate([first, rolled[SUBLANES:]], axis=0)


def _softplus(v):
    return jnp.maximum(v, 0.0) + jnp.log1p(jnp.exp(-jnp.abs(v)))


def _rglru_coeffs(xc, wcat_ref, gab, gxb, lam):
    d = xc.shape[1]
    blk = d // N_RG_BLOCKS
    xcb = xc.astype(BF16)
    rs, is_ = [], []
    for n in range(N_RG_BLOCKS):
        o = _dot(xcb[:, n * blk:(n + 1) * blk], wcat_ref[n])
        rs.append(o[:, :blk])
        is_.append(o[:, blk:])
    r = jax.nn.sigmoid(jnp.concatenate(rs, axis=1) + gab)
    i = jax.nn.sigmoid(jnp.concatenate(is_, axis=1) + gxb)
    log_a = -RG_C * r * _softplus(-lam)
    a = jnp.exp(log_a)
    mult = jnp.sqrt(-jnp.tanh(log_a) * (a * a + 1.0))
    return a, mult * (i * xc)


def _scan_rows(a, b, h0):
    m = a.shape[0]
    sub = lax.broadcasted_iota(I32, a.shape, 0) % SUBLANES
    for k in (1, 2, 4):
        a_sh = pltpu.roll(a, k, axis=0)
        b_sh = pltpu.roll(b, k, axis=0)
        keep = sub >= k
        b = jnp.where(keep, a * b_sh + b, b)
        a = jnp.where(keep, a * a_sh, a)
    outs = []
    h = h0
    for g in range(m // SUBLANES):
        hg = a[g * SUBLANES:(g + 1) * SUBLANES] * h + b[g * SUBLANES:(g + 1) * SUBLANES]
        outs.append(hg)
        h = hg[SUBLANES - 1:SUBLANES]
    return jnp.concatenate(outs, axis=0), h


def _conv_a_prompt_body(x_ref, win_hbm, cw_ref, wout_hbm, g_ref, b_ref,
                        o_ref, buf_ref, carry, win_ref, wout_ref, stage, wsem):
    s = pl.program_id(1)

    @pl.when(jnp.logical_and(pl.program_id(0) == 0, s == 0))
    def _():
        _load_weight_bf16(win_hbm, win_ref, stage, wsem)
        _load_weight_bf16(wout_hbm, wout_ref, stage, wsem)

    @pl.when(s == 0)
    def _():
        carry[...] = jnp.zeros_like(carry)

    x = x_ref[0]
    d = x.shape[1]
    bcx = _dot(x.astype(BF16), win_ref[...])
    gb, gc, xh = bcx[:, :d], bcx[:, d:2 * d], bcx[:, 2 * d:]
    u = gc * xh
    prev = carry[...]
    cw = cw_ref[...]
    conv = (cw[0:1] * _shift_rows(u, 2, prev) + cw[1:2] * _shift_rows(u, 1, prev)
            + cw[2:3] * u)
    y = _dot((gb * conv).astype(BF16), wout_ref[...])
    _store_rows(o_ref, _layer_norm(ALPHA * x + y, g_ref[...], b_ref[...]))
    ts = u.shape[0]
    carry[...] = u[ts - SUBLANES:ts]

    @pl.when(s == pl.num_programs(1) - 1)
    def _():
        buf_ref[0] = u[ts - 2:ts]


def _conv_a_prompt(x, win, cw, wout, g, b, n_extra):
    bsz, seq, d = x.shape
    ts = min(TS_A, seq)
    grid = (bsz, seq // ts)
    nj = seq // ts
    return pl.pallas_call(
        _conv_a_prompt_body,
        grid=grid,
        in_specs=[
            pl.BlockSpec((1, ts, d), lambda i, j: (i, j, 0)),
            pl.BlockSpec(memory_space=pl.ANY),
            pl.BlockSpec((3, d), lambda i, j: (0, 0)),
            pl.BlockSpec(memory_space=pl.ANY),
            pl.BlockSpec((1, d), lambda i, j: (0, 0)),
            pl.BlockSpec((1, d), lambda i, j: (0, 0)),
        ],
        out_specs=[
            pl.BlockSpec((ts * SUBLANES, LANES), lambda i, j: (i * nj + j, 0)),
            pl.BlockSpec((1, 2, d), lambda i, j: (i, 0, 0)),
        ],
        out_shape=[
            jax.ShapeDtypeStruct(((bsz * seq + n_extra) * SUBLANES, LANES), F32),
            jax.ShapeDtypeStruct((bsz, 2, d), F32),
        ],
        scratch_shapes=[pltpu.VMEM((SUBLANES, d), F32)] + _weight_scratch(d, 3 * d, d),
        compiler_params=pltpu.CompilerParams(
            dimension_semantics=("arbitrary", "arbitrary"), vmem_limit_bytes=VMEM_LIMIT),
        name="conv_a_prompt",
    )(x, win, cw, wout, g, b)


def _conv_a_sample_body(x_ref, s0_ref, s1_ref, win_hbm, cw_ref, wout_hbm, g_ref, b_ref, joint_ref,
                        o_ref, u_ref, win_ref, wout_ref, stage, wsem):
    del joint_ref
    _load_weight_bf16(win_hbm, win_ref, stage, wsem)
    _load_weight_bf16(wout_hbm, wout_ref, stage, wsem)
    x = x_ref[...]
    d = x.shape[1]
    bcx = _dot(x.astype(BF16), win_ref[...])
    gb, gc, xh = bcx[:, :d], bcx[:, d:2 * d], bcx[:, 2 * d:]
    u = gc * xh
    cw = cw_ref[...]
    conv = cw[0:1] * s0_ref[...] + cw[1:2] * s1_ref[...] + cw[2:3] * u
    y = _dot((gb * conv).astype(BF16), wout_ref[...])
    _store_rows(o_ref, _layer_norm(ALPHA * x + y, g_ref[...], b_ref[...]))
    u_ref[...] = u


def _whole(a):
    return pl.BlockSpec(a.shape, lambda i: (0,) * a.ndim)


def _conv_a_sample(x, s0, s1, win, cw, wout, g, b, joint):
    n, d = x.shape
    hbm = pl.BlockSpec(memory_space=pl.ANY)
    first_block = joint.shape[0] // (n * SUBLANES) - 1
    return pl.pallas_call(
        _conv_a_sample_body,
        grid=(1,),
        in_specs=[_whole(x), _whole(s0), _whole(s1), hbm, _whole(cw), hbm, _whole(g), _whole(b),
                  hbm],
        out_specs=[pl.BlockSpec((n * SUBLANES, LANES), lambda i: (first_block, 0)),
                   pl.BlockSpec((n, d), lambda i: (0, 0))],
        out_shape=[jax.ShapeDtypeStruct(joint.shape, F32), jax.ShapeDtypeStruct((n, d), F32)],
        scratch_shapes=_weight_scratch(d, 3 * d, d),
        input_output_aliases={8: 0},
        compiler_params=pltpu.CompilerParams(
            dimension_semantics=("arbitrary",), vmem_limit_bytes=VMEM_LIMIT),
        name="conv_a_sample",
    )(x, s0, s1, win, cw, wout, g, b, joint)


def _rglru_prompt_body(x_ref, win_hbm, cw_ref, cb_ref, wcat_ref, gab_ref, gxb_ref, lam_ref,
                       wout_hbm, g_ref, b_ref, o_ref, buf_ref, hl_ref, xcarry, hcarry,
                       win_ref, wout_ref, stage, wsem):
    s = pl.program_id(1)

    @pl.when(jnp.logical_and(pl.program_id(0) == 0, s == 0))
    def _():
        _load_weight_bf16(win_hbm, win_ref, stage, wsem)
        _load_weight_bf16(wout_hbm, wout_ref, stage, wsem)

    @pl.when(s == 0)
    def _():
        xcarry[...] = jnp.zeros_like(xcarry)
        hcarry[...] = jnp.zeros_like(hcarry)

    x = x_ref[0]
    d = x.shape[1]
    gx = _dot(x.astype(BF16), win_ref[...])
    gate, xr = gx[:, :d], gx[:, d:]
    prev = xcarry[...]
    cw = cw_ref[...]
    xc = (cw[0:1] * _shift_rows(xr, 3, prev) + cw[1:2] * _shift_rows(xr, 2, prev)
          + cw[2:3] * _shift_rows(xr, 1, prev) + cw[3:4] * xr) + cb_ref[...]
    a, bt = _rglru_coeffs(xc, wcat_ref, gab_ref[...], gxb_ref[...], lam_ref[...])
    hs, hlast = _scan_rows(a, bt, hcarry[0:1])
    y = _dot((jax.nn.gelu(gate, approximate=True) * hs).astype(BF16), wout_ref[...])
    _store_rows(o_ref, _layer_norm(ALPHA * x + y, g_ref[...], b_ref[...]))
    ts = xr.shape[0]
    xcarry[...] = xr[ts - SUBLANES:ts]
    hcarry[...] = jnp.broadcast_to(hlast, hcarry.shape)

    @pl.when(s == pl.num_programs(1) - 1)
    def _():
        buf_ref[0] = xr[ts - 3:ts]
        hl_ref[0] = hlast


def _rglru_prompt(x, win, cw, cb, wcat, gab, gxb, lam, wout, g, b, n_extra):
    bsz, seq, d = x.shape
    ts = min(TS_B, seq)
    grid = (bsz, seq // ts)
    nj = seq // ts
    blk = d // N_RG_BLOCKS
    const2 = lambda i, j: (0, 0)
    return pl.pallas_call(
        _rglru_prompt_body,
        grid=grid,
        in_specs=[
            pl.BlockSpec((1, ts, d), lambda i, j: (i, j, 0)),
            pl.BlockSpec(memory_space=pl.ANY),
            pl.BlockSpec((4, d), const2),
            pl.BlockSpec((1, d), const2),
            pl.BlockSpec((N_RG_BLOCKS, blk, 2 * blk), lambda i, j: (0, 0, 0)),
            pl.BlockSpec((1, d), const2),
            pl.BlockSpec((1, d), const2),
            pl.BlockSpec((1, d), const2),
            pl.BlockSpec(memory_space=pl.ANY),
            pl.BlockSpec((1, d), const2),
            pl.BlockSpec((1, d), const2),
        ],
        out_specs=[
            pl.BlockSpec((ts * SUBLANES, LANES), lambda i, j: (i * nj + j, 0)),
            pl.BlockSpec((1, 3, d), lambda i, j: (i, 0, 0)),
            pl.BlockSpec((1, 1, d), lambda i, j: (i, 0, 0)),
        ],
        out_shape=[
            jax.ShapeDtypeStruct(((bsz * seq + n_extra) * SUBLANES, LANES), F32),
            jax.ShapeDtypeStruct((bsz, 3, d), F32),
            jax.ShapeDtypeStruct((bsz, 1, d), F32),
        ],
        scratch_shapes=([pltpu.VMEM((SUBLANES, d), F32), pltpu.VMEM((SUBLANES, d), F32)]
                        + _weight_scratch(d, 2 * d, d)),
        compiler_params=pltpu.CompilerParams(
            dimension_semantics=("arbitrary", "arbitrary"), vmem_limit_bytes=VMEM_LIMIT),
        name="rglru_prompt",
    )(x, win, cw, cb, wcat, gab, gxb, lam, wout, g, b)


def _rglru_sample_body(x_ref, s0_ref, s1_ref, s2_ref, h0_ref, win_hbm, cw_ref, cb_ref, wcat_ref,
                       gab_ref, gxb_ref, lam_ref, wout_hbm, g_ref, b_ref, joint_ref,
                       o_ref, xr_ref, h_ref, win_ref, wout_ref, stage, wsem):
    del joint_ref
    _load_weight_bf16(win_hbm, win_ref, stage, wsem)
    _load_weight_bf16(wout_hbm, wout_ref, stage, wsem)
    x = x_ref[...]
    d = x.shape[1]
    gx = _dot(x.astype(BF16), win_ref[...])
    gate, xr = gx[:, :d], gx[:, d:]
    cw = cw_ref[...]
    xc = (cw[0:1] * s0_ref[...] + cw[1:2] * s1_ref[...] + cw[2:3] * s2_ref[...]
          + cw[3:4] * xr) + cb_ref[...]
    a, bt = _rglru_coeffs(xc, wcat_ref, gab_ref[...], gxb_ref[...], lam_ref[...])
    h = a * h0_ref[...] + bt
    y = _dot((jax.nn.gelu(gate, approximate=True) * h).astype(BF16), wout_ref[...])
    _store_rows(o_ref, _layer_norm(ALPHA * x + y, g_ref[...], b_ref[...]))
    xr_ref[...] = xr
    h_ref[...] = h


def _rglru_sample(x, s0, s1, s2, h0, win, cw, cb, wcat, gab, gxb, lam, wout, g, b, joint):
    n, d = x.shape
    hbm = pl.BlockSpec(memory_space=pl.ANY)
    first_block = joint.shape[0] // (n * SUBLANES) - 1
    vec = pl.BlockSpec((n, d), lambda i: (0, 0))
    return pl.pallas_call(
        _rglru_sample_body,
        grid=(1,),
        in_specs=[_whole(x), _whole(s0), _whole(s1), _whole(s2), _whole(h0), hbm, _whole(cw),
                  _whole(cb), _whole(wcat), _whole(gab), _whole(gxb), _whole(lam), hbm,
                  _whole(g), _whole(b), hbm],
        out_specs=[pl.BlockSpec((n * SUBLANES, LANES), lambda i: (first_block, 0)), vec, vec],
        out_shape=[jax.ShapeDtypeStruct(joint.shape, F32),
                   jax.ShapeDtypeStruct((n, d), F32), jax.ShapeDtypeStruct((n, d), F32)],
        scratch_shapes=_weight_scratch(d, 2 * d, d),
        input_output_aliases={15: 0},
        compiler_params=pltpu.CompilerParams(
            dimension_semantics=("arbitrary",), vmem_limit_bytes=VMEM_LIMIT),
        name="rglru_sample",
    )(x, s0, s1, s2, h0, win, cw, cb, wcat, gab, gxb, lam, wout, g, b, joint)


def _first_argmax(v, rows):
    m = jnp.max(v, axis=0, keepdims=True)
    idx = jnp.min(jnp.where(v == m, rows, v.shape[0]), axis=0, keepdims=True)
    return m, idx


def _route_body(x_ref, wrt_ref, tri_ref, cin_ref, ints_ref, wts_ref, cnt_ref, carry):
    @pl.when(pl.program_id(0) == 0)
    def _():
        carry[...] = cin_ref[...]

    tt = x_ref.shape[0] // SUBLANES
    xb = _load_rows(x_ref, tt).astype(BF16)
    lt = lax.dot_general(wrt_ref[...], xb, (((1,), (1,)), ((), ())), preferred_element_type=F32)
    rows8 = lax.broadcasted_iota(I32, (SUBLANES, tt), 0)
    neg_inf = jnp.float32(-jnp.inf)

    gl = jnp.where(rows8 < N_GROUPS, lt[GROUP_ROW0:GROUP_ROW0 + SUBLANES], neg_inf)
    gmax, gidx = _first_argmax(gl, rows8)
    gw = 1.0 / jnp.sum(jnp.exp(gl - gmax), axis=0, keepdims=True)

    el = lt[0:EXP_PER_GROUP]
    for g in range(1, N_GROUPS):
        el = jnp.where(gidx == g, lt[g * EXP_PER_GROUP:(g + 1) * EXP_PER_GROUP], el)
    emax, i1 = _first_argmax(el, rows8)
    el2 = jnp.where(rows8 == i1, neg_inf, el)
    m2, i2 = _first_argmax(el2, rows8)
    psum = jnp.sum(jnp.exp(el - emax), axis=0, keepdims=True)
    ep1 = 1.0 / psum
    ep2 = jnp.exp(m2 - emax) / psum
    tot = ep1 + ep2
    wa = gw * (ep1 / tot)
    wb = gw * (ep2 / tot)
    ea = gidx * EXP_PER_GROUP + i1
    eb = gidx * EXP_PER_GROUP + i2

    rows_e = lax.broadcasted_iota(I32, (N_EXPERTS, tt), 0)
    oha = rows_e == ea
    ohb = rows_e == eb
    oh = jnp.where(oha | ohb, 1.0, 0.0)
    base = carry[...][:, 0:1]
    excl = _dot(oh.astype(BF16), tri_ref[...]) + base
    ra = jnp.sum(jnp.where(oha, excl, 0.0), axis=0, keepdims=True)
    rb = jnp.sum(jnp.where(ohb, excl, 0.0), axis=0, keepdims=True)
    new = carry[...] + jnp.sum(oh, axis=1, keepdims=True)
    carry[...] = new
    cnt_ref[...] = new

    ints_ref[0:1, :] = ea
    ints_ref[1:2, :] = eb
    ints_ref[2:3, :] = ra.astype(I32)
    ints_ref[3:4, :] = rb.astype(I32)
    wts_ref[0:1, :] = wa
    wts_ref[1:2, :] = wb


def _route(x, wrt, cin, tok0, t):
    d = wrt.shape[1]
    tt = min(TT_ROUTE, t)
    blk0 = tok0 // tt
    tri = (jnp.arange(tt)[:, None] < jnp.arange(tt)[None, :]).astype(BF16)
    return pl.pallas_call(
        _route_body,
        grid=(t // tt,),
        in_specs=[
            pl.BlockSpec((tt * SUBLANES, LANES), lambda i: (blk0 + i, 0)),
            pl.BlockSpec((ROUTE_ROWS, d), lambda i: (0, 0)),
            pl.BlockSpec((tt, tt), lambda i: (0, 0)),
            pl.BlockSpec((N_EXPERTS, LANES), lambda i: (0, 0)),
        ],
        out_specs=[
            pl.BlockSpec((4, tt), lambda i: (0, i)),
            pl.BlockSpec((2, tt), lambda i: (0, i)),
            pl.BlockSpec((N_EXPERTS, LANES), lambda i: (0, 0)),
        ],
        out_shape=[
            jax.ShapeDtypeStruct((4, t), I32),
            jax.ShapeDtypeStruct((2, t), F32),
            jax.ShapeDtypeStruct((N_EXPERTS, LANES), F32),
        ],
        scratch_shapes=[pltpu.VMEM((N_EXPERTS, LANES), F32)],
        compiler_params=pltpu.CompilerParams(
            dimension_semantics=("arbitrary",), vmem_limit_bytes=VMEM_LIMIT),
        name="route",
    )(x, wrt, tri, cin)


def _dest_body(ints_ref, pst_ref, dest_ref):
    ints = ints_ref[...]
    tt = ints.shape[1]
    rows_e = lax.broadcasted_iota(I32, (N_EXPERTS, tt), 0)
    pst = pst_ref[...][:, 0:1]
    for k in range(2):
        start = jnp.sum(jnp.where(rows_e == ints[k:k + 1], pst, 0.0), axis=0, keepdims=True)
        dest_ref[k:k + 1, :] = start.astype(I32) + ints[2 + k:3 + k]


def _dest(ints, pstart_f):
    t = ints.shape[1]
    return pl.pallas_call(
        _dest_body,
        out_shape=jax.ShapeDtypeStruct((2, t), I32),
        compiler_params=pltpu.CompilerParams(vmem_limit_bytes=VMEM_LIMIT),
        name="dest",
    )(ints, pstart_f)


def _invert_body(dest_ref, padpos_ref, padlen_ref, inv_ref, *, t_total):
    first = PRIME_BLOCKS * BLK

    def prime_body(i, c):
        inv_ref[i] = SPARE_ROW0 + i
        return c

    lax.fori_loop(0, first, prime_body, 0, unroll=8)

    def pad_body(e, c):
        def one(r, c2):
            p = padpos_ref[e] + r
            inv_ref[first + p] = PAD_BASE + (p & (2 * BLK - 1))
            return c2

        return lax.fori_loop(0, padlen_ref[e], one, c)

    lax.fori_loop(0, N_EXPERTS, pad_body, 0)

    def tok_body(j, c):
        toks = [j * ROW_UNROLL + u for u in range(ROW_UNROLL)]
        rows = [[dest_ref[k * t_total + t] for k in range(2)] for t in toks]
        for t, ps in zip(toks, rows):
            for k in range(2):
                inv_ref[first + ps[k]] = t + (k << PLANE_BITS)
        return c

    lax.fori_loop(0, t_total // ROW_UNROLL, tok_body, 0)


def _invert(dest_flat, pad_pos, pad_len, p_rows):
    t_total = dest_flat.shape[0] // 2
    grid_spec = pltpu.PrefetchScalarGridSpec(
        num_scalar_prefetch=3,
        grid=(1,),
        in_specs=[],
        out_specs=pl.BlockSpec(memory_space=pltpu.SMEM),
    )
    return pl.pallas_call(
        functools.partial(_invert_body, t_total=t_total),
        grid_spec=grid_spec,
        out_shape=jax.ShapeDtypeStruct((PRIME_BLOCKS * BLK + p_rows,), I32),
        compiler_params=pltpu.CompilerParams(dimension_semantics=("arbitrary",)),
        name="invert",
    )(dest_flat, pad_pos, pad_len)


def _experts_body(be_ref, nu_ref, nblk_ref, inv_ref, x_hbm, wg_hbm, wu_hbm, wd_hbm, y_hbm,
                  xbuf, obuf, sg, su, sd, wg_ref, wu_ref, wd_ref, wsem, gsem, ssem, slot_ref,
                  *, layer):
    b = pl.program_id(0)
    nu = nu_ref[0]
    e = be_ref[jnp.minimum(b, nu - 1)]

    def gather(blk, slot, fn):
        vs = [inv_ref[(blk + PRIME_BLOCKS) * BLK + r] for r in range(BLK)]
        for r, v in enumerate(vs):
            fn(pltpu.make_async_copy(_tile_of_row(x_hbm, v & TOKEN_MASK),
                                     _tile_of_row(xbuf.at[slot], r), gsem.at[slot]))

    def scatter(blk, slot, fn):
        vs = [inv_ref[(blk + PRIME_BLOCKS) * BLK + r] for r in range(BLK)]
        for r, v in enumerate(vs):
            fn(pltpu.make_async_copy(_tile_of_row(obuf.at[slot], r),
                                     _tile_of_row(y_hbm, v), ssem.at[slot]))

    def start(cp):
        cp.start()

    def wait(cp):
        cp.wait()

    def fetch(ex, slot):
        return (pltpu.make_async_copy(wg_hbm.at[layer, ex], sg.at[slot], wsem.at[slot, 0]),
                pltpu.make_async_copy(wu_hbm.at[layer, ex], su.at[slot], wsem.at[slot, 1]),
                pltpu.make_async_copy(wd_hbm.at[layer, ex], sd.at[slot], wsem.at[slot, 2]))

    @pl.when(b == 0)
    def _():
        slot_ref[0] = 0
        for cp in fetch(e, 0):
            cp.start()
        obuf[...] = jnp.zeros_like(obuf)
        scatter(-2, 0, start)
        gather(0, 0, start)

    first_of_expert = jnp.logical_or(b == 0, e != be_ref[jnp.maximum(b - 1, 0)])

    @pl.when(jnp.logical_and(b < nu, first_of_expert))
    def _():
        slot = slot_ref[0]
        nxt = b + nblk_ref[e]

        @pl.when(nxt < nu)
        def _():
            for cp in fetch(be_ref[nxt], 1 - slot):
                cp.start()

        for cp in fetch(e, slot):
            cp.wait()
        wg_ref[...] = sg[slot].astype(BF16)
        wu_ref[...] = su[slot].astype(BF16)
        wd_ref[...] = sd[slot].astype(BF16)
        slot_ref[0] = 1 - slot

    for slot in range(2):
        @pl.when(jnp.logical_and(b < nu, b % 2 == slot))
        def _(slot=slot):
            gather(b, slot, wait)
            xb = _load_rows(xbuf, BLK, (slot,)).astype(BF16)
            scatter(b - 1, 1 - slot, start)
            gather(jnp.minimum(b + 1, nu - 1), 1 - slot, start)
            h = jax.nn.silu(_dot(xb, wg_ref[...])) * _dot(xb, wu_ref[...])
            o = _dot(h.astype(BF16), wd_ref[...])
            scatter(b - 2, slot, wait)
            _store_rows(obuf, o, (slot,))

        @pl.when(jnp.logical_and(b == nu, b % 2 == slot))
        def _(slot=slot):
            scatter(b - 1, 1 - slot, start)
            scatter(b - 2, slot, wait)
            scatter(b - 1, 1 - slot, wait)
            gather(nu - 1, slot, wait)


def _experts(blk_e, n_used, nblk, inv, x, wg, wu, wd, layer):
    d, de = wg.shape[2], wg.shape[3]
    nb = blk_e.shape[0]
    hbm = pl.BlockSpec(memory_space=pl.ANY)
    grid_spec = pltpu.PrefetchScalarGridSpec(
        num_scalar_prefetch=4,
        grid=(nb + 1,),
        in_specs=[hbm, hbm, hbm, hbm],
        out_specs=hbm,
        scratch_shapes=[
            pltpu.VMEM((2, BLK * SUBLANES, LANES), F32), pltpu.VMEM((2, BLK * SUBLANES, LANES), F32),
            pltpu.VMEM((2, d, de), F32), pltpu.VMEM((2, d, de), F32), pltpu.VMEM((2, de, d), F32),
            pltpu.VMEM((d, de), BF16), pltpu.VMEM((d, de), BF16), pltpu.VMEM((de, d), BF16),
            pltpu.SemaphoreType.DMA((2, 3)), pltpu.SemaphoreType.DMA((2,)),
            pltpu.SemaphoreType.DMA((2,)), pltpu.SMEM((1,), I32),
        ],
    )
    return pl.pallas_call(
        functools.partial(_experts_body, layer=layer),
        grid_spec=grid_spec,
        out_shape=jax.ShapeDtypeStruct((Y_ROWS * SUBLANES, LANES), F32),
        compiler_params=pltpu.CompilerParams(
            dimension_semantics=("arbitrary",), vmem_limit_bytes=VMEM_LIMIT),
        name="experts",
    )(blk_e, n_used, nblk, inv, x, wg, wu, wd)


def _combine_body(x_ref, ya_ref, yb_ref, w_ref, g_ref, b_ref, o_ref):
    tc = o_ref.shape[0]
    w = w_ref[...]
    y = w[:, 0:1] * _load_rows(ya_ref, tc) + w[:, 1:2] * _load_rows(yb_ref, tc)
    o_ref[...] = _layer_norm(ALPHA * _load_rows(x_ref, tc) + y, g_ref[...], b_ref[...])


def _combine(x, y, w_cols, g, b, tok0, t):
    d = g.shape[1]
    tc = min(TC, t)
    blk0 = tok0 // tc
    plane = (1 << PLANE_BITS) // tc
    row_tiled = lambda first: pl.BlockSpec((tc * SUBLANES, LANES), lambda i: (first + i, 0))
    return pl.pallas_call(
        _combine_body,
        grid=(t // tc,),
        in_specs=[
            row_tiled(blk0), row_tiled(blk0), row_tiled(plane + blk0),
            pl.BlockSpec((tc, 2), lambda i: (blk0 + i, 0)),
            pl.BlockSpec((1, d), lambda i: (0, 0)),
            pl.BlockSpec((1, d), lambda i: (0, 0)),
        ],
        out_specs=pl.BlockSpec((tc, d), lambda i: (i, 0)),
        out_shape=jax.ShapeDtypeStruct((t, d), F32),
        compiler_params=pltpu.CompilerParams(
            dimension_semantics=("arbitrary",), vmem_limit_bytes=VMEM_LIMIT),
        name="combine",
    )(x, y, y, w_cols, g, b)


def _moe_layer(x, tp, ts, w_group, w_expert, wg, wu, wd, layer, g, b):
    d = g.shape[1]
    t_total = tp + ts
    assert t_total <= 1 << PLANE_BITS and tp % ts == 0
    wrt = jnp.zeros((ROUTE_ROWS, d), F32)
    wrt = wrt.at[0:N_EXPERTS].set(w_expert.T).at[GROUP_ROW0:GROUP_ROW0 + N_GROUPS].set(w_group.T)
    wrt = wrt.astype(BF16)

    zero_cnt = jnp.zeros((N_EXPERTS, LANES), F32)
    ints_p, wts_p, cnt_p = _route(x, wrt, zero_cnt, 0, tp)
    ints_s, wts_s, cnt = _route(x, wrt, cnt_p, tp, ts)
    ints = jnp.concatenate([ints_p, ints_s], axis=1)
    wts = jnp.concatenate([wts_p, wts_s], axis=1)

    counts = cnt[:, 0].astype(I32)
    pcounts = (counts + BLK - 1) // BLK * BLK
    pend = jnp.cumsum(pcounts)
    pstart = pend - pcounts
    nb = (2 * t_total + N_EXPERTS * (BLK - 1) + BLK - 1) // BLK
    p_rows = nb * BLK
    n_used = (pend[-1] // BLK).astype(I32).reshape(1)
    blk_first = jnp.minimum(jnp.arange(nb, dtype=I32), n_used[0] - 1) * BLK
    blk_e = jnp.sum((pend[None, :] <= blk_first[:, None]).astype(I32), axis=1)
    nblk = pcounts // BLK

    pstart_f = jnp.broadcast_to(pstart.astype(F32)[:, None], (N_EXPERTS, LANES))
    dest_flat = _dest(ints, pstart_f).reshape(2 * t_total)

    inv = _invert(dest_flat, pstart + counts, pcounts - counts, p_rows)
    y = _experts(blk_e, n_used, nblk, inv, x, wg, wu, wd, layer)
    w_cols = wts.T
    return (_combine(x, y, w_cols, g, b, 0, tp), _combine(x, y, w_cols, g, b, tp, ts))


def kernel(x_prompt, x_sample, state_conv_a, state_conv_b, state_h, a_w_in, a_conv_w, a_w_out,
           b_w_in, b_conv_w, b_conv_b, b_gate_a_w, b_gate_a_b, b_gate_x_w, b_gate_x_b, b_lambda,
           b_w_out, ln1_g, ln1_b, ln2_g, ln2_b, moe_w_group, moe_w_expert, moe_w_gate, moe_w_up,
           moe_w_down):
    bsz, seq, d = x_prompt.shape
    n_s = x_sample.shape[0]
    row = lambda v: v.reshape(1, d)

    tp = bsz * seq
    win, wout = a_w_in[0], a_w_out[0]
    x1, conv_a_p = _conv_a_prompt(x_prompt, win, a_conv_w[0], wout, row(ln1_g[0]), row(ln1_b[0]),
                                  n_s)
    sa = state_conv_a[0]
    x1, u_s = _conv_a_sample(x_sample.reshape(n_s, d), sa[:, 0], sa[:, 1], win, a_conv_w[0], wout,
                             row(ln1_g[0]), row(ln1_b[0]), x1)
    conv_a_s = jnp.stack([sa[:, 1], u_s], axis=1)

    xp, xs = _moe_layer(x1, tp, n_s, moe_w_group[0], moe_w_expert[0],
                        moe_w_gate, moe_w_up, moe_w_down, 0, row(ln2_g[0]), row(ln2_b[0]))

    win, wout = b_w_in[0], b_w_out[0]
    wcat = jnp.concatenate([b_gate_a_w[0], b_gate_x_w[0]], axis=-1).astype(BF16)
    args = (win, b_conv_w[0], row(b_conv_b[0]), wcat, row(b_gate_a_b[0]), row(b_gate_x_b[0]),
            row(b_lambda[0]), wout, row(ln1_g[1]), row(ln1_b[1]))
    x1, conv_b_p, h_p = _rglru_prompt(xp.reshape(bsz, seq, d), *args, n_s)
    sb = state_conv_b[0]
    x1, xr_s, h_s = _rglru_sample(xs, sb[:, 0], sb[:, 1], sb[:, 2], state_h[0], *args, x1)
    conv_b_s = jnp.stack([sb[:, 1], sb[:, 2], xr_s], axis=1)

    xp, xs = _moe_layer(x1, tp, n_s, moe_w_group[1], moe_w_expert[1],
                        moe_w_gate, moe_w_up, moe_w_down, 1, row(ln2_g[1]), row(ln2_b[1]))

    return (xp.reshape(bsz, seq, d), xs.reshape(n_s, 1, d),
            conv_a_p[None], conv_a_s[None], conv_b_p[None], conv_b_s[None],
            h_p.reshape(1, bsz, d), h_s[None])
```

```python
import functools

import jax
import jax.numpy as jnp
from jax import lax
from jax.experimental import pallas as pl
from jax.experimental.pallas import tpu as pltpu

F32 = jnp.float32
BF16 = jnp.bfloat16
I32 = jnp.int32

DEPTH = 2
N_RG_BLOCKS = 8
RG_C = 8.0
N_GROUPS = 4
EXP_PER_GROUP = 8
N_EXPERTS = N_GROUPS * EXP_PER_GROUP
ALPHA = (2.0 * DEPTH) ** 0.25
LN_EPS = 1e-5

LANES = 128
SUBLANES = 8
VMEM_LIMIT = 56 * 1024 * 1024

TS_A = 512
TS_B = 256
TT_ROUTE = 1024
TC = 256
BLK = 256
ROUTE_ROWS = 128
GROUP_ROW0 = N_EXPERTS
W_CHUNK = 512
ROW_UNROLL = 8

PLANE_BITS = 15
TOKEN_MASK = (1 << PLANE_BITS) - 1
PAD_BASE = 2 << PLANE_BITS
N_SLOTS = 3
PRIME_BLOCKS = N_SLOTS
INV_ROW0 = PRIME_BLOCKS * BLK
PAD_SPAN = 4 * BLK
SPARE_ROW0 = PAD_BASE + PAD_SPAN
Y_ROWS = SPARE_ROW0 + PRIME_BLOCKS * BLK


def _dot(a, b):
    return jnp.dot(a, b, preferred_element_type=F32)


def _load_rows(ref, m, idx=()):
    return jnp.concatenate(
        [ref[idx + (pl.ds(s, m, stride=SUBLANES), slice(None))] for s in range(SUBLANES)], axis=1)


def _store_rows(ref, v, idx=()):
    m = v.shape[0]
    for s in range(SUBLANES):
        ref[idx + (pl.ds(s, m, stride=SUBLANES), slice(None))] = v[:, s * LANES:(s + 1) * LANES]


def _tile_of_row(ref, r):
    return ref.at[pl.ds(pl.multiple_of(r * SUBLANES, SUBLANES), SUBLANES)]


def _load_weight_bf16(w_hbm, w_bf, stage, sem):
    nch = w_hbm.shape[1] // W_CHUNK

    def chunk_copy(c):
        return pltpu.make_async_copy(w_hbm.at[:, pl.ds(c * W_CHUNK, W_CHUNK)],
                                     stage.at[c % 2], sem.at[c % 2])

    chunk_copy(0).start()
    for c in range(nch):
        if c + 1 < nch:
            chunk_copy(c + 1).start()
        chunk_copy(c).wait()
        w_bf[:, c * W_CHUNK:(c + 1) * W_CHUNK] = stage[c % 2].astype(BF16)


def _weight_scratch(k, *ns):
    return ([pltpu.VMEM((k, n), BF16) for n in ns]
            + [pltpu.VMEM((2, k, W_CHUNK), F32), pltpu.SemaphoreType.DMA((2,))])


def _layer_norm(r, g, b):
    mu = jnp.mean(r, axis=-1, keepdims=True)
    d = r - mu
    var = jnp.mean(d * d, axis=-1, keepdims=True)
    return d * lax.rsqrt(var + LN_EPS) * g + b


def _shift_rows(v, k, prev8):
    rolled = pltpu.roll(v, k, axis=0)
    rows8 = lax.broadcasted_iota(I32, (SUBLANES, v.shape[1]), 0)
    first = jnp.where(rows8 < k, pltpu.roll(prev8, k, axis=0), rolled[0:SUBLANES])
    return jnp.concatenate([first, rolled[SUBLANES:]], axis=0)


def _softplus(v):
    return jnp.maximum(v, 0.0) + jnp.log1p(jnp.exp(-jnp.abs(v)))


def _sigmoid(z):
    return 0.5 * jnp.tanh(0.5 * z) + 0.5


def _rglru_coeffs(xc, wcat_ref, gab, gxb, lam):
    d = xc.shape[1]
    blk = d // N_RG_BLOCKS
    xcb = xc.astype(BF16)
    rs, is_ = [], []
    for n in range(N_RG_BLOCKS):
        o = _dot(xcb[:, n * blk:(n + 1) * blk], wcat_ref[n])
        rs.append(o[:, :blk])
        is_.append(o[:, blk:])
    r = _sigmoid(jnp.concatenate(rs, axis=1) + gab)
    i = _sigmoid(jnp.concatenate(is_, axis=1) + gxb)
    log_a = -RG_C * r * _softplus(-lam)
    a = jnp.exp(log_a)
    mult = jnp.sqrt(-jnp.tanh(log_a) * (a * a + 1.0))
    return a, mult * (i * xc)


def _scan_rows(a, b, h0):
    m, d = a.shape
    groups = m // SUBLANES
    a = a.reshape(groups, SUBLANES, d)
    b = b.reshape(groups, SUBLANES, d)
    sub = lax.broadcasted_iota(I32, a.shape, 1)
    for k in (1, 2, 4):
        keep = sub >= k
        a_sh = jnp.where(keep, pltpu.roll(a, k, axis=1), 1.0)
        b_sh = jnp.where(keep, pltpu.roll(b, k, axis=1), 0.0)
        b = a * b_sh + b
        a = a * a_sh
    outs = []
    h = h0
    for g in range(groups):
        hg = a[g] * h + b[g]
        outs.append(hg)
        h = hg[SUBLANES - 1:SUBLANES]
    return jnp.concatenate(outs, axis=0), h


def _conv_a_prompt_body(x_ref, win_hbm, cw_ref, wout_hbm, g_ref, b_ref,
                        o_ref, buf_ref, carry, win_ref, wout_ref, stage, wsem):
    s = pl.program_id(1)

    @pl.when(jnp.logical_and(pl.program_id(0) == 0, s == 0))
    def _():
        _load_weight_bf16(win_hbm, win_ref, stage, wsem)
        _load_weight_bf16(wout_hbm, wout_ref, stage, wsem)

    @pl.when(s == 0)
    def _():
        carry[...] = jnp.zeros_like(carry)

    x = x_ref[0]
    d = x.shape[1]
    bcx = _dot(x.astype(BF16), win_ref[...])
    gb, gc, xh = bcx[:, :d], bcx[:, d:2 * d], bcx[:, 2 * d:]
    u = gc * xh
    prev = carry[...]
    cw = cw_ref[...]
    conv = (cw[0:1] * _shift_rows(u, 2, prev) + cw[1:2] * _shift_rows(u, 1, prev)
            + cw[2:3] * u)
    y = _dot((gb * conv).astype(BF16), wout_ref[...])
    _store_rows(o_ref, _layer_norm(ALPHA * x + y, g_ref[...], b_ref[...]))
    ts = u.shape[0]
    carry[...] = u[ts - SUBLANES:ts]

    @pl.when(s == pl.num_programs(1) - 1)
    def _():
        buf_ref[0] = u[ts - 2:ts]


def _conv_a_prompt(x, win, cw, wout, g, b, n_extra):
    bsz, seq, d = x.shape
    ts = min(TS_A, seq)
    grid = (bsz, seq // ts)
    nj = seq // ts
    return pl.pallas_call(
        _conv_a_prompt_body,
        grid=grid,
        in_specs=[
            pl.BlockSpec((1, ts, d), lambda i, j: (i, j, 0)),
            pl.BlockSpec(memory_space=pl.ANY),
            pl.BlockSpec((3, d), lambda i, j: (0, 0)),
            pl.BlockSpec(memory_space=pl.ANY),
            pl.BlockSpec((1, d), lambda i, j: (0, 0)),
            pl.BlockSpec((1, d), lambda i, j: (0, 0)),
        ],
        out_specs=[
            pl.BlockSpec((ts * SUBLANES, LANES), lambda i, j: (i * nj + j, 0)),
            pl.BlockSpec((1, 2, d), lambda i, j: (i, 0, 0)),
        ],
        out_shape=[
            jax.ShapeDtypeStruct(((bsz * seq + n_extra) * SUBLANES, LANES), F32),
            jax.ShapeDtypeStruct((bsz, 2, d), F32),
        ],
        scratch_shapes=[pltpu.VMEM((SUBLANES, d), F32)] + _weight_scratch(d, 3 * d, d),
        compiler_params=pltpu.CompilerParams(
            dimension_semantics=("arbitrary", "arbitrary"), vmem_limit_bytes=VMEM_LIMIT),
        name="conv_a_prompt",
    )(x, win, cw, wout, g, b)


def _conv_a_sample_body(x_ref, s0_ref, s1_ref, win_hbm, cw_ref, wout_hbm, g_ref, b_ref, joint_ref,
                        o_ref, u_ref, win_ref, wout_ref, stage, wsem):
    del joint_ref
    _load_weight_bf16(win_hbm, win_ref, stage, wsem)
    _load_weight_bf16(wout_hbm, wout_ref, stage, wsem)
    x = x_ref[...]
    d = x.shape[1]
    bcx = _dot(x.astype(BF16), win_ref[...])
    gb, gc, xh = bcx[:, :d], bcx[:, d:2 * d], bcx[:, 2 * d:]
    u = gc * xh
    cw = cw_ref[...]
    conv = cw[0:1] * s0_ref[...] + cw[1:2] * s1_ref[...] + cw[2:3] * u
    y = _dot((gb * conv).astype(BF16), wout_ref[...])
    _store_rows(o_ref, _layer_norm(ALPHA * x + y, g_ref[...], b_ref[...]))
    u_ref[...] = u


def _whole(a):
    return pl.BlockSpec(a.shape, lambda i: (0,) * a.ndim)


def _conv_a_sample(x, s0, s1, win, cw, wout, g, b, joint):
    n, d = x.shape
    hbm = pl.BlockSpec(memory_space=pl.ANY)
    first_block = joint.shape[0] // (n * SUBLANES) - 1
    return pl.pallas_call(
        _conv_a_sample_body,
        grid=(1,),
        in_specs=[_whole(x), _whole(s0), _whole(s1), hbm, _whole(cw), hbm, _whole(g), _whole(b),
                  hbm],
        out_specs=[pl.BlockSpec((n * SUBLANES, LANES), lambda i: (first_block, 0)),
                   pl.BlockSpec((n, d), lambda i: (0, 0))],
        out_shape=[jax.ShapeDtypeStruct(joint.shape, F32), jax.ShapeDtypeStruct((n, d), F32)],
        scratch_shapes=_weight_scratch(d, 3 * d, d),
        input_output_aliases={8: 0},
        compiler_params=pltpu.CompilerParams(
            dimension_semantics=("arbitrary",), vmem_limit_bytes=VMEM_LIMIT),
        name="conv_a_sample",
    )(x, s0, s1, win, cw, wout, g, b, joint)


def _rglru_prompt_body(x_ref, win_hbm, cw_ref, cb_ref, wcat_ref, gab_ref, gxb_ref, lam_ref,
                       wout_hbm, g_ref, b_ref, o_ref, buf_ref, hl_ref, xcarry, hcarry,
                       win_ref, wout_ref, stage, wsem):
    s = pl.program_id(1)

    @pl.when(jnp.logical_and(pl.program_id(0) == 0, s == 0))
    def _():
        _load_weight_bf16(win_hbm, win_ref, stage, wsem)
        _load_weight_bf16(wout_hbm, wout_ref, stage, wsem)

    @pl.when(s == 0)
    def _():
        xcarry[...] = jnp.zeros_like(xcarry)
        hcarry[...] = jnp.zeros_like(hcarry)

    x = x_ref[0]
    d = x.shape[1]
    gx = _dot(x.astype(BF16), win_ref[...])
    gate, xr = gx[:, :d], gx[:, d:]
    prev = xcarry[...]
    cw = cw_ref[...]
    xc = (cw[0:1] * _shift_rows(xr, 3, prev) + cw[1:2] * _shift_rows(xr, 2, prev)
          + cw[2:3] * _shift_rows(xr, 1, prev) + cw[3:4] * xr) + cb_ref[...]
    a, bt = _rglru_coeffs(xc, wcat_ref, gab_ref[...], gxb_ref[...], lam_ref[...])
    hs, hlast = _scan_rows(a, bt, hcarry[0:1])
    y = _dot((jax.nn.gelu(gate, approximate=True) * hs).astype(BF16), wout_ref[...])
    _store_rows(o_ref, _layer_norm(ALPHA * x + y, g_ref[...], b_ref[...]))
    ts = xr.shape[0]
    xcarry[...] = xr[ts - SUBLANES:ts]
    hcarry[...] = jnp.broadcast_to(hlast, hcarry.shape)

    @pl.when(s == pl.num_programs(1) - 1)
    def _():
        buf_ref[0] = xr[ts - 3:ts]
        hl_ref[0] = hlast


def _rglru_prompt(x, win, cw, cb, wcat, gab, gxb, lam, wout, g, b, n_extra):
    bsz, seq, d = x.shape
    ts = min(TS_B, seq)
    grid = (bsz, seq // ts)
    nj = seq // ts
    blk = d // N_RG_BLOCKS
    const2 = lambda i, j: (0, 0)
    return pl.pallas_call(
        _rglru_prompt_body,
        grid=grid,
        in_specs=[
            pl.BlockSpec((1, ts, d), lambda i, j: (i, j, 0)),
            pl.BlockSpec(memory_space=pl.ANY),
            pl.BlockSpec((4, d), const2),
            pl.BlockSpec((1, d), const2),
            pl.BlockSpec((N_RG_BLOCKS, blk, 2 * blk), lambda i, j: (0, 0, 0)),
            pl.BlockSpec((1, d), const2),
            pl.BlockSpec((1, d), const2),
            pl.BlockSpec((1, d), const2),
            pl.BlockSpec(memory_space=pl.ANY),
            pl.BlockSpec((1, d), const2),
            pl.BlockSpec((1, d), const2),
        ],
        out_specs=[
            pl.BlockSpec((ts * SUBLANES, LANES), lambda i, j: (i * nj + j, 0)),
            pl.BlockSpec((1, 3, d), lambda i, j: (i, 0, 0)),
            pl.BlockSpec((1, 1, d), lambda i, j: (i, 0, 0)),
        ],
        out_shape=[
            jax.ShapeDtypeStruct(((bsz * seq + n_extra) * SUBLANES, LANES), F32),
            jax.ShapeDtypeStruct((bsz, 3, d), F32),
            jax.ShapeDtypeStruct((bsz, 1, d), F32),
        ],
        scratch_shapes=([pltpu.VMEM((SUBLANES, d), F32), pltpu.VMEM((SUBLANES, d), F32)]
                        + _weight_scratch(d, 2 * d, d)),
        compiler_params=pltpu.CompilerParams(
            dimension_semantics=("arbitrary", "arbitrary"), vmem_limit_bytes=VMEM_LIMIT),
        name="rglru_prompt",
    )(x, win, cw, cb, wcat, gab, gxb, lam, wout, g, b)


def _rglru_sample_body(x_ref, s0_ref, s1_ref, s2_ref, h0_ref, win_hbm, cw_ref, cb_ref, wcat_ref,
                       gab_ref, gxb_ref, lam_ref, wout_hbm, g_ref, b_ref, joint_ref,
                       o_ref, xr_ref, h_ref, win_ref, wout_ref, stage, wsem):
    del joint_ref
    _load_weight_bf16(win_hbm, win_ref, stage, wsem)
    _load_weight_bf16(wout_hbm, wout_ref, stage, wsem)
    x = x_ref[...]
    d = x.shape[1]
    gx = _dot(x.astype(BF16), win_ref[...])
    gate, xr = gx[:, :d], gx[:, d:]
    cw = cw_ref[...]
    xc = (cw[0:1] * s0_ref[...] + cw[1:2] * s1_ref[...] + cw[2:3] * s2_ref[...]
          + cw[3:4] * xr) + cb_ref[...]
    a, bt = _rglru_coeffs(xc, wcat_ref, gab_ref[...], gxb_ref[...], lam_ref[...])
    h = a * h0_ref[...] + bt
    y = _dot((jax.nn.gelu(gate, approximate=True) * h).astype(BF16), wout_ref[...])
    _store_rows(o_ref, _layer_norm(ALPHA * x + y, g_ref[...], b_ref[...]))
    xr_ref[...] = xr
    h_ref[...] = h


def _rglru_sample(x, s0, s1, s2, h0, win, cw, cb, wcat, gab, gxb, lam, wout, g, b, joint):
    n, d = x.shape
    hbm = pl.BlockSpec(memory_space=pl.ANY)
    first_block = joint.shape[0] // (n * SUBLANES) - 1
    vec = pl.BlockSpec((n, d), lambda i: (0, 0))
    return pl.pallas_call(
        _rglru_sample_body,
        grid=(1,),
        in_specs=[_whole(x), _whole(s0), _whole(s1), _whole(s2), _whole(h0), hbm, _whole(cw),
                  _whole(cb), _whole(wcat), _whole(gab), _whole(gxb), _whole(lam), hbm,
                  _whole(g), _whole(b), hbm],
        out_specs=[pl.BlockSpec((n * SUBLANES, LANES), lambda i: (first_block, 0)), vec, vec],
        out_shape=[jax.ShapeDtypeStruct(joint.shape, F32),
                   jax.ShapeDtypeStruct((n, d), F32), jax.ShapeDtypeStruct((n, d), F32)],
        scratch_shapes=_weight_scratch(d, 2 * d, d),
        input_output_aliases={15: 0},
        compiler_params=pltpu.CompilerParams(
            dimension_semantics=("arbitrary",), vmem_limit_bytes=VMEM_LIMIT),
        name="rglru_sample",
    )(x, s0, s1, s2, h0, win, cw, cb, wcat, gab, gxb, lam, wout, g, b, joint)


def _first_argmax(v, rows):
    m = jnp.max(v, axis=0, keepdims=True)
    idx = jnp.min(jnp.where(v == m, rows, v.shape[0]), axis=0, keepdims=True)
    return m, idx


def _route_body(x_ref, wrt_ref, tri_ref, cin_ref, ints_ref, wts_ref, cnt_ref, carry):
    @pl.when(pl.program_id(0) == 0)
    def _():
        carry[...] = cin_ref[...]

    tt = x_ref.shape[0] // SUBLANES
    xb = _load_rows(x_ref, tt).astype(BF16)
    lt = lax.dot_general(wrt_ref[...], xb, (((1,), (1,)), ((), ())), preferred_element_type=F32)
    rows8 = lax.broadcasted_iota(I32, (SUBLANES, tt), 0)
    neg_inf = jnp.float32(-jnp.inf)

    gl = jnp.where(rows8 < N_GROUPS, lt[GROUP_ROW0:GROUP_ROW0 + SUBLANES], neg_inf)
    gmax, gidx = _first_argmax(gl, rows8)
    gw = 1.0 / jnp.sum(jnp.exp(gl - gmax), axis=0, keepdims=True)

    el = lt[0:EXP_PER_GROUP]
    for g in range(1, N_GROUPS):
        el = jnp.where(gidx == g, lt[g * EXP_PER_GROUP:(g + 1) * EXP_PER_GROUP], el)
    emax, i1 = _first_argmax(el, rows8)
    el2 = jnp.where(rows8 == i1, neg_inf, el)
    m2, i2 = _first_argmax(el2, rows8)
    psum = jnp.sum(jnp.exp(el - emax), axis=0, keepdims=True)
    ep1 = 1.0 / psum
    ep2 = jnp.exp(m2 - emax) / psum
    tot = ep1 + ep2
    wa = gw * (ep1 / tot)
    wb = gw * (ep2 / tot)
    ea = gidx * EXP_PER_GROUP + i1
    eb = gidx * EXP_PER_GROUP + i2

    rows_e = lax.broadcasted_iota(I32, (N_EXPERTS, tt), 0)
    oha = rows_e == ea
    ohb = rows_e == eb
    oh = jnp.where(oha | ohb, 1.0, 0.0)
    base = carry[...][:, 0:1]
    excl = _dot(oh.astype(BF16), tri_ref[...]) + base
    ra = jnp.sum(jnp.where(oha, excl, 0.0), axis=0, keepdims=True)
    rb = jnp.sum(jnp.where(ohb, excl, 0.0), axis=0, keepdims=True)
    new = carry[...] + jnp.sum(oh, axis=1, keepdims=True)
    carry[...] = new
    cnt_ref[...] = new

    ints_ref[0:1, :] = ea
    ints_ref[1:2, :] = eb
    ints_ref[2:3, :] = ra.astype(I32)
    ints_ref[3:4, :] = rb.astype(I32)
    wts_ref[0:1, :] = wa
    wts_ref[1:2, :] = wb


def _route(x, wrt, cin, tok0, t):
    d = wrt.shape[1]
    tt = min(TT_ROUTE, t)
    blk0 = tok0 // tt
    tri = (jnp.arange(tt)[:, None] < jnp.arange(tt)[None, :]).astype(BF16)
    return pl.pallas_call(
        _route_body,
        grid=(t // tt,),
        in_specs=[
            pl.BlockSpec((tt * SUBLANES, LANES), lambda i: (blk0 + i, 0)),
            pl.BlockSpec((ROUTE_ROWS, d), lambda i: (0, 0)),
            pl.BlockSpec((tt, tt), lambda i: (0, 0)),
            pl.BlockSpec((N_EXPERTS, LANES), lambda i: (0, 0)),
        ],
        out_specs=[
            pl.BlockSpec((4, tt), lambda i: (0, i)),
            pl.BlockSpec((2, tt), lambda i: (0, i)),
            pl.BlockSpec((N_EXPERTS, LANES), lambda i: (0, 0)),
        ],
        out_shape=[
            jax.ShapeDtypeStruct((4, t), I32),
            jax.ShapeDtypeStruct((2, t), F32),
            jax.ShapeDtypeStruct((N_EXPERTS, LANES), F32),
        ],
        scratch_shapes=[pltpu.VMEM((N_EXPERTS, LANES), F32)],
        compiler_params=pltpu.CompilerParams(
            dimension_semantics=("arbitrary",), vmem_limit_bytes=VMEM_LIMIT),
        name="route",
    )(x, wrt, tri, cin)


def _dest_body(ints_ref, pst_ref, dest_ref):
    ints = ints_ref[...]
    tt = ints.shape[1]
    rows_e = lax.broadcasted_iota(I32, (N_EXPERTS, tt), 0)
    pst = pst_ref[...][:, 0:1]
    for k in range(2):
        start = jnp.sum(jnp.where(rows_e == ints[k:k + 1], pst, 0.0), axis=0, keepdims=True)
        dest_ref[k:k + 1, :] = start.astype(I32) + ints[2 + k:3 + k]


def _dest(ints, pstart_f):
    t = ints.shape[1]
    return pl.pallas_call(
        _dest_body,
        out_shape=jax.ShapeDtypeStruct((2, t), I32),
        compiler_params=pltpu.CompilerParams(vmem_limit_bytes=VMEM_LIMIT),
        name="dest",
    )(ints, pstart_f)


def _invert_body(dest_ref, padpos_ref, inv_ref, *, t_total):
    def prime_body(i, c):
        inv_ref[i] = SPARE_ROW0 + i
        return c

    lax.fori_loop(0, INV_ROW0, prime_body, 0, unroll=8)

    def pad_body(e, c):
        q0 = padpos_ref[e]
        for r in range(BLK):
            inv_ref[q0 + r] = PAD_BASE + ((q0 + r) & (PAD_SPAN - 1))
        return c

    lax.fori_loop(0, N_EXPERTS, pad_body, 0)

    def tok_body(j, c):
        toks = [j * ROW_UNROLL + u for u in range(ROW_UNROLL)]
        rows = [[dest_ref[k * t_total + t] for k in range(2)] for t in toks]
        for t, qs in zip(toks, rows):
            for k in range(2):
                inv_ref[qs[k]] = t + (k << PLANE_BITS)
        return c

    lax.fori_loop(0, t_total // ROW_UNROLL, tok_body, 0)


def _invert(dest_flat, pad_pos, p_rows):
    t_total = dest_flat.shape[0] // 2
    grid_spec = pltpu.PrefetchScalarGridSpec(
        num_scalar_prefetch=2,
        grid=(1,),
        in_specs=[],
        out_specs=pl.BlockSpec(memory_space=pltpu.SMEM),
    )
    return pl.pallas_call(
        functools.partial(_invert_body, t_total=t_total),
        grid_spec=grid_spec,
        out_shape=jax.ShapeDtypeStruct((INV_ROW0 + p_rows + BLK,), I32),
        compiler_params=pltpu.CompilerParams(dimension_semantics=("arbitrary",)),
        name="invert",
    )(dest_flat, pad_pos)


def _experts_body(be_ref, nu_ref, nblk_ref, inv_ref, x_hbm, wg_hbm, wu_hbm, wd_hbm, y_hbm,
                  xbuf, obuf, sg, su, sd, wg_ref, wu_ref, wd_ref, wsem, gsem, ssem, slot_ref,
                  *, layer):
    b = pl.program_id(0)
    nu = nu_ref[0]
    e = be_ref[jnp.minimum(b, nu - 1)]

    def gather(blk, slot, fn):
        vs = [inv_ref[(blk + PRIME_BLOCKS) * BLK + r] for r in range(BLK)]
        for r, v in enumerate(vs):
            fn(pltpu.make_async_copy(_tile_of_row(x_hbm, v & TOKEN_MASK),
                                     _tile_of_row(xbuf.at[slot], r), gsem.at[slot]))

    def scatter(blk, slot, fn):
        vs = [inv_ref[(blk + PRIME_BLOCKS) * BLK + r] for r in range(BLK)]
        for r, v in enumerate(vs):
            fn(pltpu.make_async_copy(_tile_of_row(obuf.at[slot], r),
                                     _tile_of_row(y_hbm, v), ssem.at[slot]))

    def start(cp):
        cp.start()

    def wait(cp):
        cp.wait()

    def fetch(ex, slot):
        return (pltpu.make_async_copy(wg_hbm.at[layer, ex], sg.at[slot], wsem.at[slot, 0]),
                pltpu.make_async_copy(wu_hbm.at[layer, ex], su.at[slot], wsem.at[slot, 1]),
                pltpu.make_async_copy(wd_hbm.at[layer, ex], sd.at[slot], wsem.at[slot, 2]))

    @pl.when(b == 0)
    def _():
        slot_ref[0] = 0
        for cp in fetch(e, 0):
            cp.start()
        obuf[...] = jnp.zeros_like(obuf)
        scatter(-3, 0, start)
        scatter(-2, 1, start)
        gather(0, 0, start)
        gather(jnp.minimum(1, nu - 1), 1, start)

    first_of_expert = jnp.logical_or(b == 0, e != be_ref[jnp.maximum(b - 1, 0)])

    @pl.when(jnp.logical_and(b < nu, first_of_expert))
    def _():
        slot = slot_ref[0]
        nxt = b + nblk_ref[e]

        @pl.when(nxt < nu)
        def _():
            for cp in fetch(be_ref[nxt], 1 - slot):
                cp.start()

        for cp in fetch(e, slot):
            cp.wait()
        wg_ref[...] = sg[slot].astype(BF16)
        wu_ref[...] = su[slot].astype(BF16)
        wd_ref[...] = sd[slot].astype(BF16)
        slot_ref[0] = 1 - slot

    for slot in range(N_SLOTS):
        prev, nxt = (slot - 1) % N_SLOTS, (slot + 1) % N_SLOTS

        @pl.when(jnp.logical_and(b < nu, b % N_SLOTS == slot))
        def _(slot=slot, prev=prev):
            gather(b, slot, wait)
            xb = _load_rows(xbuf, BLK, (slot,)).astype(BF16)
            scatter(b - 1, prev, start)
            gather(jnp.minimum(b + 2, nu - 1), prev, start)
            h = jax.nn.silu(_dot(xb, wg_ref[...])) * _dot(xb, wu_ref[...])
            o = _dot(h.astype(BF16), wd_ref[...])
            scatter(b - 3, slot, wait)
            _store_rows(obuf, o, (slot,))

        @pl.when(jnp.logical_and(b == nu, b % N_SLOTS == slot))
        def _(slot=slot, prev=prev, nxt=nxt):
            scatter(b - 1, prev, start)
            scatter(b - 3, slot, wait)
            scatter(b - 2, nxt, wait)
            scatter(b - 1, prev, wait)
            gather(nu - 1, slot, wait)
            gather(nu - 1, nxt, wait)


def _experts(blk_e, n_used, nblk, inv, x, wg, wu, wd, layer):
    d, de = wg.shape[2], wg.shape[3]
    nb = blk_e.shape[0]
    hbm = pl.BlockSpec(memory_space=pl.ANY)
    grid_spec = pltpu.PrefetchScalarGridSpec(
        num_scalar_prefetch=4,
        grid=(nb + 1,),
        in_specs=[hbm, hbm, hbm, hbm],
        out_specs=hbm,
        scratch_shapes=[
            pltpu.VMEM((N_SLOTS, BLK * SUBLANES, LANES), F32),
            pltpu.VMEM((N_SLOTS, BLK * SUBLANES, LANES), F32),
            pltpu.VMEM((2, d, de), F32), pltpu.VMEM((2, d, de), F32), pltpu.VMEM((2, de, d), F32),
            pltpu.VMEM((d, de), BF16), pltpu.VMEM((d, de), BF16), pltpu.VMEM((de, d), BF16),
            pltpu.SemaphoreType.DMA((2, 3)), pltpu.SemaphoreType.DMA((N_SLOTS,)),
            pltpu.SemaphoreType.DMA((N_SLOTS,)), pltpu.SMEM((1,), I32),
        ],
    )
    return pl.pallas_call(
        functools.partial(_experts_body, layer=layer),
        grid_spec=grid_spec,
        out_shape=jax.ShapeDtypeStruct((Y_ROWS * SUBLANES, LANES), F32),
        compiler_params=pltpu.CompilerParams(
            dimension_semantics=("arbitrary",), vmem_limit_bytes=VMEM_LIMIT),
        name="experts",
    )(blk_e, n_used, nblk, inv, x, wg, wu, wd)


def _combine_body(x_ref, ya_ref, yb_ref, w_ref, g_ref, b_ref, o_ref):
    tc = o_ref.shape[0]
    w = w_ref[...]
    y = w[:, 0:1] * _load_rows(ya_ref, tc) + w[:, 1:2] * _load_rows(yb_ref, tc)
    o_ref[...] = _layer_norm(ALPHA * _load_rows(x_ref, tc) + y, g_ref[...], b_ref[...])


def _combine(x, y, w_cols, g, b, tok0, t):
    d = g.shape[1]
    tc = min(TC, t)
    blk0 = tok0 // tc
    plane = (1 << PLANE_BITS) // tc
    row_tiled = lambda first: pl.BlockSpec((tc * SUBLANES, LANES), lambda i: (first + i, 0))
    return pl.pallas_call(
        _combine_body,
        grid=(t // tc,),
        in_specs=[
            row_tiled(blk0), row_tiled(blk0), row_tiled(plane + blk0),
            pl.BlockSpec((tc, 2), lambda i: (blk0 + i, 0)),
            pl.BlockSpec((1, d), lambda i: (0, 0)),
            pl.BlockSpec((1, d), lambda i: (0, 0)),
        ],
        out_specs=pl.BlockSpec((tc, d), lambda i: (i, 0)),
        out_shape=jax.ShapeDtypeStruct((t, d), F32),
        compiler_params=pltpu.CompilerParams(
            dimension_semantics=("arbitrary",), vmem_limit_bytes=VMEM_LIMIT),
        name="combine",
    )(x, y, y, w_cols, g, b)


def _moe_layer(x, tp, ts, w_group, w_expert, wg, wu, wd, layer, g, b):
    d = g.shape[1]
    t_total = tp + ts
    assert PAD_SPAN <= t_total <= 1 << PLANE_BITS and tp % ts == 0
    wrt = jnp.zeros((ROUTE_ROWS, d), F32)
    wrt = wrt.at[0:N_EXPERTS].set(w_expert.T).at[GROUP_ROW0:GROUP_ROW0 + N_GROUPS].set(w_group.T)
    wrt = wrt.astype(BF16)

    zero_cnt = jnp.zeros((N_EXPERTS, LANES), F32)
    ints_p, wts_p, cnt_p = _route(x, wrt, zero_cnt, 0, tp)
    ints_s, wts_s, cnt = _route(x, wrt, cnt_p, tp, ts)
    ints = jnp.concatenate([ints_p, ints_s], axis=1)
    wts = jnp.concatenate([wts_p, wts_s], axis=1)

    counts = cnt[:, 0].astype(I32)
    pcounts = (counts + BLK - 1) // BLK * BLK
    pend = jnp.cumsum(pcounts)
    pstart = pend - pcounts
    nb = (2 * t_total + N_EXPERTS * (BLK - 1) + BLK - 1) // BLK
    p_rows = nb * BLK
    n_used = (pend[-1] // BLK).astype(I32).reshape(1)
    blk_first = jnp.minimum(jnp.arange(nb, dtype=I32), n_used[0] - 1) * BLK
    blk_e = jnp.sum((pend[None, :] <= blk_first[:, None]).astype(I32), axis=1)
    nblk = pcounts // BLK

    pstart_f = jnp.broadcast_to((pstart + INV_ROW0).astype(F32)[:, None], (N_EXPERTS, LANES))
    dest_flat = _dest(ints, pstart_f).reshape(2 * t_total)

    inv = _invert(dest_flat, pstart + counts + INV_ROW0, p_rows)
    y = _experts(blk_e, n_used, nblk, inv, x, wg, wu, wd, layer)
    w_cols = wts.T
    return (_combine(x, y, w_cols, g, b, 0, tp), _combine(x, y, w_cols, g, b, tp, ts))


def kernel(x_prompt, x_sample, state_conv_a, state_conv_b, state_h, a_w_in, a_conv_w, a_w_out,
           b_w_in, b_conv_w, b_conv_b, b_gate_a_w, b_gate_a_b, b_gate_x_w, b_gate_x_b, b_lambda,
           b_w_out, ln1_g, ln1_b, ln2_g, ln2_b, moe_w_group, moe_w_expert, moe_w_gate, moe_w_up,
           moe_w_down):
    bsz, seq, d = x_prompt.shape
    n_s = x_sample.shape[0]
    row = lambda v: v.reshape(1, d)

    tp = bsz * seq
    win, wout = a_w_in[0], a_w_out[0]
    x1, conv_a_p = _conv_a_prompt(x_prompt, win, a_conv_w[0], wout, row(ln1_g[0]), row(ln1_b[0]),
                                  n_s)
    sa = state_conv_a[0]
    x1, u_s = _conv_a_sample(x_sample.reshape(n_s, d), sa[:, 0], sa[:, 1], win, a_conv_w[0], wout,
                             row(ln1_g[0]), row(ln1_b[0]), x1)
    conv_a_s = jnp.stack([sa[:, 1], u_s], axis=1)

    xp, xs = _moe_layer(x1, tp, n_s, moe_w_group[0], moe_w_expert[0],
                        moe_w_gate, moe_w_up, moe_w_down, 0, row(ln2_g[0]), row(ln2_b[0]))

    win, wout = b_w_in[0], b_w_out[0]
    wcat = jnp.concatenate([b_gate_a_w[0], b_gate_x_w[0]], axis=-1).astype(BF16)
    args = (win, b_conv_w[0], row(b_conv_b[0]), wcat, row(b_gate_a_b[0]), row(b_gate_x_b[0]),
            row(b_lambda[0]), wout, row(ln1_g[1]), row(ln1_b[1]))
    x1, conv_b_p, h_p = _rglru_prompt(xp.reshape(bsz, seq, d), *args, n_s)
    sb = state_conv_b[0]
    x1, xr_s, h_s = _rglru_sample(xs, sb[:, 0], sb[:, 1], sb[:, 2], state_h[0], *args, x1)
    conv_b_s = jnp.stack([sb[:, 1], sb[:, 2], xr_s], axis=1)

    xp, xs = _moe_layer(x1, tp, n_s, moe_w_group[1], moe_w_expert[1],
                        moe_w_gate, moe_w_up, moe_w_down, 1, row(ln2_g[1]), row(ln2_b[1]))

    return (xp.reshape(bsz, seq, d), xs.reshape(n_s, 1, d),
            conv_a_p[None], conv_a_s[None], conv_b_p[None], conv_b_s[None],
            h_p.reshape(1, bsz, d), h_s[None])
```

```python
import functools

import jax
import jax.numpy as jnp
from jax import lax
from jax.experimental import pallas as pl
from jax.experimental.pallas import tpu as pltpu

F32 = jnp.float32
BF16 = jnp.bfloat16
I32 = jnp.int32

DEPTH = 2
N_RG_BLOCKS = 8
RG_C = 8.0
N_GROUPS = 4
EXP_PER_GROUP = 8
N_EXPERTS = N_GROUPS * EXP_PER_GROUP
ALPHA = (2.0 * DEPTH) ** 0.25
LN_EPS = 1e-5

LANES = 128
SUBLANES = 8
VMEM_LIMIT = 56 * 1024 * 1024

TS_A = 512
TS_B = 256
TT_ROUTE = 1024
TC = 256
BLK = 256
ROUTE_ROWS = 128
GROUP_ROW0 = N_EXPERTS
W_CHUNK = 512
RG_UNIT = 2
ROW_UNROLL = 8

PLANE_BITS = 15
TOKEN_MASK = (1 << PLANE_BITS) - 1
PAD_BASE = 2 << PLANE_BITS
N_SLOTS = 3
PRIME_BLOCKS = N_SLOTS
INV_ROW0 = PRIME_BLOCKS * BLK
PAD_SPAN = 4 * BLK
SPARE_ROW0 = PAD_BASE + PAD_SPAN
Y_ROWS = SPARE_ROW0 + PRIME_BLOCKS * BLK


def _dot(a, b):
    return jnp.dot(a, b, preferred_element_type=F32)


def _load_rows(ref, m, idx=()):
    return jnp.concatenate(
        [ref[idx + (pl.ds(s, m, stride=SUBLANES), slice(None))] for s in range(SUBLANES)], axis=1)


def _store_rows(ref, v, idx=()):
    m = v.shape[0]
    for s in range(SUBLANES):
        ref[idx + (pl.ds(s, m, stride=SUBLANES), slice(None))] = v[:, s * LANES:(s + 1) * LANES]


def _tile_of_row(ref, r):
    return ref.at[pl.ds(pl.multiple_of(r * SUBLANES, SUBLANES), SUBLANES)]


def _load_weight_bf16(w_hbm, w_bf, stage, sem):
    nch = w_hbm.shape[1] // W_CHUNK

    def chunk_copy(c):
        return pltpu.make_async_copy(w_hbm.at[:, pl.ds(c * W_CHUNK, W_CHUNK)],
                                     stage.at[c % 2], sem.at[c % 2])

    chunk_copy(0).start()
    for c in range(nch):
        if c + 1 < nch:
            chunk_copy(c + 1).start()
        chunk_copy(c).wait()
        w_bf[:, c * W_CHUNK:(c + 1) * W_CHUNK] = stage[c % 2].astype(BF16)


def _weight_scratch(k, *ns):
    return ([pltpu.VMEM((k, n), BF16) for n in ns]
            + [pltpu.VMEM((2, k, W_CHUNK), F32), pltpu.SemaphoreType.DMA((2,))])


def _layer_norm(r, g, b):
    mu = jnp.mean(r, axis=-1, keepdims=True)
    d = r - mu
    var = jnp.mean(d * d, axis=-1, keepdims=True)
    return d * lax.rsqrt(var + LN_EPS) * g + b


def _shift_rows(v, k, prev8):
    rolled = pltpu.roll(v, k, axis=0)
    rows8 = lax.broadcasted_iota(I32, (SUBLANES, v.shape[1]), 0)
    first = jnp.where(rows8 < k, pltpu.roll(prev8, k, axis=0), rolled[0:SUBLANES])
    return jnp.concatenate([first, rolled[SUBLANES:]], axis=0)


def _softplus(v):
    return jnp.maximum(v, 0.0) + jnp.log1p(jnp.exp(-jnp.abs(v)))


def _sigmoid(z):
    return 0.5 * jnp.tanh(0.5 * z) + 0.5


def _rglru_coeffs(xc, wcat_ref, gab, gxb, lam, blk0=0):
    blk = wcat_ref.shape[1]
    xcb = xc.astype(BF16)
    rs, is_ = [], []
    for n in range(xc.shape[1] // blk):
        o = _dot(xcb[:, n * blk:(n + 1) * blk], wcat_ref[blk0 + n])
        rs.append(o[:, :blk])
        is_.append(o[:, blk:])
    r = _sigmoid(jnp.concatenate(rs, axis=1) + gab)
    i = _sigmoid(jnp.concatenate(is_, axis=1) + gxb)
    neg_log_a = RG_C * r * _softplus(-lam)
    a = jnp.exp(-neg_log_a)
    v = jnp.tanh(neg_log_a) * (a * a + 1.0)
    mult = jnp.where(v > 0.0, v * lax.rsqrt(v), 0.0)
    return a, mult * (i * xc)


def _scan_rows(a, b, h0):
    m, d = a.shape
    groups = m // SUBLANES
    a = a.reshape(groups, SUBLANES, d)
    b = b.reshape(groups, SUBLANES, d)
    sub = lax.broadcasted_iota(I32, a.shape, 1)
    for k in (1, 2, 4):
        keep = sub >= k
        a_sh = jnp.where(keep, pltpu.roll(a, k, axis=1), 1.0)
        b_sh = jnp.where(keep, pltpu.roll(b, k, axis=1), 0.0)
        b = a * b_sh + b
        a = a * a_sh
    outs = []
    h = h0
    for g in range(groups):
        hg = a[g] * h + b[g]
        outs.append(hg)
        h = hg[SUBLANES - 1:SUBLANES]
    return jnp.concatenate(outs, axis=0), h


def _conv_a_prompt_body(x_ref, win_hbm, cw_ref, wout_hbm, g_ref, b_ref,
                        o_ref, buf_ref, carry, win_ref, wout_ref, stage, wsem):
    s = pl.program_id(1)

    @pl.when(jnp.logical_and(pl.program_id(0) == 0, s == 0))
    def _():
        _load_weight_bf16(win_hbm, win_ref, stage, wsem)
        _load_weight_bf16(wout_hbm, wout_ref, stage, wsem)

    @pl.when(s == 0)
    def _():
        carry[...] = jnp.zeros_like(carry)

    x = x_ref[0]
    d = x.shape[1]
    bcx = _dot(x.astype(BF16), win_ref[...])
    gb, gc, xh = bcx[:, :d], bcx[:, d:2 * d], bcx[:, 2 * d:]
    u = gc * xh
    prev = carry[...]
    cw = cw_ref[...]
    conv = (cw[0:1] * _shift_rows(u, 2, prev) + cw[1:2] * _shift_rows(u, 1, prev)
            + cw[2:3] * u)
    y = _dot((gb * conv).astype(BF16), wout_ref[...])
    _store_rows(o_ref, _layer_norm(ALPHA * x + y, g_ref[...], b_ref[...]))
    ts = u.shape[0]
    carry[...] = u[ts - SUBLANES:ts]

    @pl.when(s == pl.num_programs(1) - 1)
    def _():
        buf_ref[0] = u[ts - 2:ts]


def _conv_a_prompt(x, win, cw, wout, g, b, n_extra):
    bsz, seq, d = x.shape
    ts = min(TS_A, seq)
    grid = (bsz, seq // ts)
    nj = seq // ts
    return pl.pallas_call(
        _conv_a_prompt_body,
        grid=grid,
        in_specs=[
            pl.BlockSpec((1, ts, d), lambda i, j: (i, j, 0)),
            pl.BlockSpec(memory_space=pl.ANY),
            pl.BlockSpec((3, d), lambda i, j: (0, 0)),
            pl.BlockSpec(memory_space=pl.ANY),
            pl.BlockSpec((1, d), lambda i, j: (0, 0)),
            pl.BlockSpec((1, d), lambda i, j: (0, 0)),
        ],
        out_specs=[
            pl.BlockSpec((ts * SUBLANES, LANES), lambda i, j: (i * nj + j, 0)),
            pl.BlockSpec((1, 2, d), lambda i, j: (i, 0, 0)),
        ],
        out_shape=[
            jax.ShapeDtypeStruct(((bsz * seq + n_extra) * SUBLANES, LANES), F32),
            jax.ShapeDtypeStruct((bsz, 2, d), F32),
        ],
        scratch_shapes=[pltpu.VMEM((SUBLANES, d), F32)] + _weight_scratch(d, 3 * d, d),
        compiler_params=pltpu.CompilerParams(
            dimension_semantics=("arbitrary", "arbitrary"), vmem_limit_bytes=VMEM_LIMIT),
        name="conv_a_prompt",
    )(x, win, cw, wout, g, b)


def _conv_a_sample_body(x_ref, s0_ref, s1_ref, win_hbm, cw_ref, wout_hbm, g_ref, b_ref, joint_ref,
                        o_ref, u_ref, win_ref, wout_ref, stage, wsem):
    del joint_ref
    _load_weight_bf16(win_hbm, win_ref, stage, wsem)
    _load_weight_bf16(wout_hbm, wout_ref, stage, wsem)
    x = x_ref[...]
    d = x.shape[1]
    bcx = _dot(x.astype(BF16), win_ref[...])
    gb, gc, xh = bcx[:, :d], bcx[:, d:2 * d], bcx[:, 2 * d:]
    u = gc * xh
    cw = cw_ref[...]
    conv = cw[0:1] * s0_ref[...] + cw[1:2] * s1_ref[...] + cw[2:3] * u
    y = _dot((gb * conv).astype(BF16), wout_ref[...])
    _store_rows(o_ref, _layer_norm(ALPHA * x + y, g_ref[...], b_ref[...]))
    u_ref[...] = u


def _whole(a):
    return pl.BlockSpec(a.shape, lambda i: (0,) * a.ndim)


def _conv_a_sample(x, s0, s1, win, cw, wout, g, b, joint):
    n, d = x.shape
    hbm = pl.BlockSpec(memory_space=pl.ANY)
    first_block = joint.shape[0] // (n * SUBLANES) - 1
    return pl.pallas_call(
        _conv_a_sample_body,
        grid=(1,),
        in_specs=[_whole(x), _whole(s0), _whole(s1), hbm, _whole(cw), hbm, _whole(g), _whole(b),
                  hbm],
        out_specs=[pl.BlockSpec((n * SUBLANES, LANES), lambda i: (first_block, 0)),
                   pl.BlockSpec((n, d), lambda i: (0, 0))],
        out_shape=[jax.ShapeDtypeStruct(joint.shape, F32), jax.ShapeDtypeStruct((n, d), F32)],
        scratch_shapes=_weight_scratch(d, 3 * d, d),
        input_output_aliases={8: 0},
        compiler_params=pltpu.CompilerParams(
            dimension_semantics=("arbitrary",), vmem_limit_bytes=VMEM_LIMIT),
        name="conv_a_sample",
    )(x, s0, s1, win, cw, wout, g, b, joint)


def _rglru_prompt_body(x_ref, xn_ref, win_hbm, cw_ref, cb_ref, wcat_ref, gab_ref, gxb_ref, lam_ref,
                       wout_hbm, g_ref, b_ref, o_ref, buf_ref, hl_ref, xcarry, hcarry, gx_scr,
                       win_ref, wout_ref, stage, wsem, *, nj):
    i = pl.program_id(0)
    s = i % nj

    @pl.when(i == 0)
    def _():
        _load_weight_bf16(win_hbm, win_ref, stage, wsem)
        _load_weight_bf16(wout_hbm, wout_ref, stage, wsem)
        gx_scr[0] = _dot(x_ref[0].astype(BF16), win_ref[...])

    @pl.when(s == 0)
    def _():
        xcarry[...] = jnp.zeros_like(xcarry)
        hcarry[...] = jnp.zeros_like(hcarry)

    for par in range(2):
        @pl.when(i % 2 == par)
        def _(par=par):
            x = x_ref[0]
            ts, d = x.shape
            xn = xn_ref[0].astype(BF16)
            blk = d // N_RG_BLOCKS
            zs = []
            for u in range(N_RG_BLOCKS // RG_UNIT):
                c0, c1 = u * RG_UNIT * blk, (u + 1) * RG_UNIT * blk
                for half in (0, d):
                    gx_scr[1 - par, :, half + c0:half + c1] = _dot(
                        xn, win_ref[:, half + c0:half + c1])
                gate, xr = gx_scr[par, :, c0:c1], gx_scr[par, :, d + c0:d + c1]
                prev = xcarry[:, c0:c1]
                cw = cw_ref[:, c0:c1]
                xc = (cw[0:1] * _shift_rows(xr, 3, prev) + cw[1:2] * _shift_rows(xr, 2, prev)
                      + cw[2:3] * _shift_rows(xr, 1, prev) + cw[3:4] * xr) + cb_ref[:, c0:c1]
                a, bt = _rglru_coeffs(xc, wcat_ref, gab_ref[:, c0:c1], gxb_ref[:, c0:c1],
                                      lam_ref[:, c0:c1], u * RG_UNIT)
                hs, hlast = _scan_rows(a, bt, hcarry[0:1, c0:c1])
                zs.append((jax.nn.gelu(gate, approximate=True) * hs).astype(BF16))
                xcarry[:, c0:c1] = xr[ts - SUBLANES:ts]
                hcarry[:, c0:c1] = jnp.broadcast_to(hlast, (SUBLANES, c1 - c0))

            y = _dot(jnp.concatenate(zs, axis=1), wout_ref[...])
            _store_rows(o_ref, _layer_norm(ALPHA * x + y, g_ref[...], b_ref[...]))

    @pl.when(s == nj - 1)
    def _():
        buf_ref[0] = xcarry[SUBLANES - 3:SUBLANES]
        hl_ref[0] = hcarry[0:1]


def _rglru_prompt(x, win, cw, cb, wcat, gab, gxb, lam, wout, g, b, n_extra):
    bsz, seq, d = x.shape
    ts = min(TS_B, seq)
    nj = seq // ts
    n = bsz * nj
    blk = d // N_RG_BLOCKS
    const2 = lambda i: (0, 0)
    tile = lambda i: (i // nj, i % nj, 0)
    return pl.pallas_call(
        functools.partial(_rglru_prompt_body, nj=nj),
        grid=(n,),
        in_specs=[
            pl.BlockSpec((1, ts, d), tile),
            pl.BlockSpec((1, ts, d), lambda i: tile(jnp.minimum(i + 1, n - 1))),
            pl.BlockSpec(memory_space=pl.ANY),
            pl.BlockSpec((4, d), const2),
            pl.BlockSpec((1, d), const2),
            pl.BlockSpec((N_RG_BLOCKS, blk, 2 * blk), lambda i: (0, 0, 0)),
            pl.BlockSpec((1, d), const2),
            pl.BlockSpec((1, d), const2),
            pl.BlockSpec((1, d), const2),
            pl.BlockSpec(memory_space=pl.ANY),
            pl.BlockSpec((1, d), const2),
            pl.BlockSpec((1, d), const2),
        ],
        out_specs=[
            pl.BlockSpec((ts * SUBLANES, LANES), lambda i: (i, 0)),
            pl.BlockSpec((1, 3, d), lambda i: (i // nj, 0, 0)),
            pl.BlockSpec((1, 1, d), lambda i: (i // nj, 0, 0)),
        ],
        out_shape=[
            jax.ShapeDtypeStruct(((bsz * seq + n_extra) * SUBLANES, LANES), F32),
            jax.ShapeDtypeStruct((bsz, 3, d), F32),
            jax.ShapeDtypeStruct((bsz, 1, d), F32),
        ],
        scratch_shapes=([pltpu.VMEM((SUBLANES, d), F32), pltpu.VMEM((SUBLANES, d), F32),
                         pltpu.VMEM((2, ts, 2 * d), F32)]
                        + _weight_scratch(d, 2 * d, d)),
        compiler_params=pltpu.CompilerParams(
            dimension_semantics=("arbitrary",), vmem_limit_bytes=VMEM_LIMIT),
        name="rglru_prompt",
    )(x, x, win, cw, cb, wcat, gab, gxb, lam, wout, g, b)


def _rglru_sample_body(x_ref, s0_ref, s1_ref, s2_ref, h0_ref, win_hbm, cw_ref, cb_ref, wcat_ref,
                       gab_ref, gxb_ref, lam_ref, wout_hbm, g_ref, b_ref, joint_ref,
                       o_ref, xr_ref, h_ref, win_ref, wout_ref, stage, wsem):
    del joint_ref
    _load_weight_bf16(win_hbm, win_ref, stage, wsem)
    _load_weight_bf16(wout_hbm, wout_ref, stage, wsem)
    x = x_ref[...]
    d = x.shape[1]
    gx = _dot(x.astype(BF16), win_ref[...])
    gate, xr = gx[:, :d], gx[:, d:]
    cw = cw_ref[...]
    xc = (cw[0:1] * s0_ref[...] + cw[1:2] * s1_ref[...] + cw[2:3] * s2_ref[...]
          + cw[3:4] * xr) + cb_ref[...]
    a, bt = _rglru_coeffs(xc, wcat_ref, gab_ref[...], gxb_ref[...], lam_ref[...])
    h = a * h0_ref[...] + bt
    y = _dot((jax.nn.gelu(gate, approximate=True) * h).astype(BF16), wout_ref[...])
    _store_rows(o_ref, _layer_norm(ALPHA * x + y, g_ref[...], b_ref[...]))
    xr_ref[...] = xr
    h_ref[...] = h


def _rglru_sample(x, s0, s1, s2, h0, win, cw, cb, wcat, gab, gxb, lam, wout, g, b, joint):
    n, d = x.shape
    hbm = pl.BlockSpec(memory_space=pl.ANY)
    first_block = joint.shape[0] // (n * SUBLANES) - 1
    vec = pl.BlockSpec((n, d), lambda i: (0, 0))
    return pl.pallas_call(
        _rglru_sample_body,
        grid=(1,),
        in_specs=[_whole(x), _whole(s0), _whole(s1), _whole(s2), _whole(h0), hbm, _whole(cw),
                  _whole(cb), _whole(wcat), _whole(gab), _whole(gxb), _whole(lam), hbm,
                  _whole(g), _whole(b), hbm],
        out_specs=[pl.BlockSpec((n * SUBLANES, LANES), lambda i: (first_block, 0)), vec, vec],
        out_shape=[jax.ShapeDtypeStruct(joint.shape, F32),
                   jax.ShapeDtypeStruct((n, d), F32), jax.ShapeDtypeStruct((n, d), F32)],
        scratch_shapes=_weight_scratch(d, 2 * d, d),
        input_output_aliases={15: 0},
        compiler_params=pltpu.CompilerParams(
            dimension_semantics=("arbitrary",), vmem_limit_bytes=VMEM_LIMIT),
        name="rglru_sample",
    )(x, s0, s1, s2, h0, win, cw, cb, wcat, gab, gxb, lam, wout, g, b, joint)


def _first_argmax(v, rows):
    m = jnp.max(v, axis=0, keepdims=True)
    idx = jnp.min(jnp.where(v == m, rows, v.shape[0]), axis=0, keepdims=True)
    return m, idx


def _route_body(x_ref, wrt_ref, tri_ref, cin_ref, ints_ref, wts_ref, cnt_ref, carry):
    @pl.when(pl.program_id(0) == 0)
    def _():
        carry[...] = cin_ref[...]

    tt = x_ref.shape[0] // SUBLANES
    xb = _load_rows(x_ref, tt).astype(BF16)
    lt = lax.dot_general(wrt_ref[...], xb, (((1,), (1,)), ((), ())), preferred_element_type=F32)
    rows8 = lax.broadcasted_iota(I32, (SUBLANES, tt), 0)
    neg_inf = jnp.float32(-jnp.inf)

    gl = jnp.where(rows8 < N_GROUPS, lt[GROUP_ROW0:GROUP_ROW0 + SUBLANES], neg_inf)
    gmax, gidx = _first_argmax(gl, rows8)
    gw = 1.0 / jnp.sum(jnp.exp(gl - gmax), axis=0, keepdims=True)

    el = lt[0:EXP_PER_GROUP]
    for g in range(1, N_GROUPS):
        el = jnp.where(gidx == g, lt[g * EXP_PER_GROUP:(g + 1) * EXP_PER_GROUP], el)
    emax, i1 = _first_argmax(el, rows8)
    el2 = jnp.where(rows8 == i1, neg_inf, el)
    m2, i2 = _first_argmax(el2, rows8)
    psum = jnp.sum(jnp.exp(el - emax), axis=0, keepdims=True)
    ep1 = 1.0 / psum
    ep2 = jnp.exp(m2 - emax) / psum
    tot = ep1 + ep2
    wa = gw * (ep1 / tot)
    wb = gw * (ep2 / tot)
    ea = gidx * EXP_PER_GROUP + i1
    eb = gidx * EXP_PER_GROUP + i2

    rows_e = lax.broadcasted_iota(I32, (N_EXPERTS, tt), 0)
    oha = rows_e == ea
    ohb = rows_e == eb
    oh = jnp.where(oha | ohb, 1.0, 0.0)
    base = carry[...][:, 0:1]
    excl = _dot(oh.astype(BF16), tri_ref[...]) + base
    ra = jnp.sum(jnp.where(oha, excl, 0.0), axis=0, keepdims=True)
    rb = jnp.sum(jnp.where(ohb, excl, 0.0), axis=0, keepdims=True)
    new = carry[...] + jnp.sum(oh, axis=1, keepdims=True)
    carry[...] = new
    cnt_ref[...] = new

    ints_ref[0:1, :] = ea
    ints_ref[1:2, :] = eb
    ints_ref[2:3, :] = ra.astype(I32)
    ints_ref[3:4, :] = rb.astype(I32)
    wts_ref[0:1, :] = wa
    wts_ref[1:2, :] = wb


def _route(x, wrt, cin, tok0, t):
    d = wrt.shape[1]
    tt = min(TT_ROUTE, t)
    blk0 = tok0 // tt
    tri = (jnp.arange(tt)[:, None] < jnp.arange(tt)[None, :]).astype(BF16)
    return pl.pallas_call(
        _route_body,
        grid=(t // tt,),
        in_specs=[
            pl.BlockSpec((tt * SUBLANES, LANES), lambda i: (blk0 + i, 0)),
            pl.BlockSpec((ROUTE_ROWS, d), lambda i: (0, 0)),
            pl.BlockSpec((tt, tt), lambda i: (0, 0)),
            pl.BlockSpec((N_EXPERTS, LANES), lambda i: (0, 0)),
        ],
        out_specs=[
            pl.BlockSpec((4, tt), lambda i: (0, i)),
            pl.BlockSpec((2, tt), lambda i: (0, i)),
            pl.BlockSpec((N_EXPERTS, LANES), lambda i: (0, 0)),
        ],
        out_shape=[
            jax.ShapeDtypeStruct((4, t), I32),
            jax.ShapeDtypeStruct((2, t), F32),
            jax.ShapeDtypeStruct((N_EXPERTS, LANES), F32),
        ],
        scratch_shapes=[pltpu.VMEM((N_EXPERTS, LANES), F32)],
        compiler_params=pltpu.CompilerParams(
            dimension_semantics=("arbitrary",), vmem_limit_bytes=VMEM_LIMIT),
        name="route",
    )(x, wrt, tri, cin)


def _dest_body(ints_ref, pst_ref, dest_ref):
    ints = ints_ref[...]
    tt = ints.shape[1]
    rows_e = lax.broadcasted_iota(I32, (N_EXPERTS, tt), 0)
    pst = pst_ref[...][:, 0:1]
    for k in range(2):
        start = jnp.sum(jnp.where(rows_e == ints[k:k + 1], pst, 0.0), axis=0, keepdims=True)
        dest_ref[k:k + 1, :] = start.astype(I32) + ints[2 + k:3 + k]


def _dest(ints, pstart_f):
    t = ints.shape[1]
    return pl.pallas_call(
        _dest_body,
        out_shape=jax.ShapeDtypeStruct((2, t), I32),
        compiler_params=pltpu.CompilerParams(vmem_limit_bytes=VMEM_LIMIT),
        name="dest",
    )(ints, pstart_f)


def _invert_body(dest_ref, padpos_ref, inv_ref, *, t_total):
    def prime_body(i, c):
        inv_ref[i] = SPARE_ROW0 + i
        return c

    lax.fori_loop(0, INV_ROW0, prime_body, 0, unroll=8)

    def pad_body(e, c):
        q0 = padpos_ref[e]
        for r in range(BLK):
            inv_ref[q0 + r] = PAD_BASE + ((q0 + r) & (PAD_SPAN - 1))
        return c

    lax.fori_loop(0, N_EXPERTS, pad_body, 0)

    def tok_body(j, c):
        toks = [j * ROW_UNROLL + u for u in range(ROW_UNROLL)]
        rows = [[dest_ref[k * t_total + t] for k in range(2)] for t in toks]
        for t, qs in zip(toks, rows):
            for k in range(2):
                inv_ref[qs[k]] = t + (k << PLANE_BITS)
        return c

    lax.fori_loop(0, t_total // ROW_UNROLL, tok_body, 0)


def _invert(dest_flat, pad_pos, p_rows):
    t_total = dest_flat.shape[0] // 2
    grid_spec = pltpu.PrefetchScalarGridSpec(
        num_scalar_prefetch=2,
        grid=(1,),
        in_specs=[],
        out_specs=pl.BlockSpec(memory_space=pltpu.SMEM),
    )
    return pl.pallas_call(
        functools.partial(_invert_body, t_total=t_total),
        grid_spec=grid_spec,
        out_shape=jax.ShapeDtypeStruct((INV_ROW0 + p_rows + BLK,), I32),
        compiler_params=pltpu.CompilerParams(dimension_semantics=("arbitrary",)),
        name="invert",
    )(dest_flat, pad_pos)


def _experts_body(be_ref, nu_ref, nblk_ref, inv_ref, x_hbm, wg_hbm, wu_hbm, wd_hbm, y_hbm,
                  xbuf, obuf, sg, su, sd, wg_ref, wu_ref, wd_ref, wsem, gsem, ssem, slot_ref,
                  *, layer):
    b = pl.program_id(0)
    nu = nu_ref[0]
    e = be_ref[jnp.minimum(b, nu - 1)]

    def gather(blk, slot, fn):
        vs = [inv_ref[(blk + PRIME_BLOCKS) * BLK + r] for r in range(BLK)]
        for r, v in enumerate(vs):
            fn(pltpu.make_async_copy(_tile_of_row(x_hbm, v & TOKEN_MASK),
                                     _tile_of_row(xbuf.at[slot], r), gsem.at[slot]))

    def scatter(blk, slot, fn):
        vs = [inv_ref[(blk + PRIME_BLOCKS) * BLK + r] for r in range(BLK)]
        for r, v in enumerate(vs):
            fn(pltpu.make_async_copy(_tile_of_row(obuf.at[slot], r),
                                     _tile_of_row(y_hbm, v), ssem.at[slot]))

    def start(cp):
        cp.start()

    def wait(cp):
        cp.wait()

    def fetch(ex, slot):
        return (pltpu.make_async_copy(wg_hbm.at[layer, ex], sg.at[slot], wsem.at[slot, 0]),
                pltpu.make_async_copy(wu_hbm.at[layer, ex], su.at[slot], wsem.at[slot, 1]),
                pltpu.make_async_copy(wd_hbm.at[layer, ex], sd.at[slot], wsem.at[slot, 2]))

    @pl.when(b == 0)
    def _():
        slot_ref[0] = 0
        for cp in fetch(e, 0):
            cp.start()
        obuf[...] = jnp.zeros_like(obuf)
        scatter(-3, 0, start)
        scatter(-2, 1, start)
        gather(0, 0, start)
        gather(jnp.minimum(1, nu - 1), 1, start)

    first_of_expert = jnp.logical_or(b == 0, e != be_ref[jnp.maximum(b - 1, 0)])

    @pl.when(jnp.logical_and(b < nu, first_of_expert))
    def _():
        slot = slot_ref[0]
        nxt = b + nblk_ref[e]

        @pl.when(nxt < nu)
        def _():
            for cp in fetch(be_ref[nxt], 1 - slot):
                cp.start()

        for cp in fetch(e, slot):
            cp.wait()
        wg_ref[...] = sg[slot].astype(BF16)
        wu_ref[...] = su[slot].astype(BF16)
        wd_ref[...] = sd[slot].astype(BF16)
        slot_ref[0] = 1 - slot

    for slot in range(N_SLOTS):
        prev, nxt = (slot - 1) % N_SLOTS, (slot + 1) % N_SLOTS

        @pl.when(jnp.logical_and(b < nu, b % N_SLOTS == slot))
        def _(slot=slot, prev=prev):
            gather(b, slot, wait)
            scatter(b - 3, slot, wait)
            xb = _load_rows(xbuf, BLK, (slot,)).astype(BF16)
            scatter(b - 1, prev, start)
            gather(jnp.minimum(b + 2, nu - 1), prev, start)
            h = jax.nn.silu(_dot(xb, wg_ref[...])) * _dot(xb, wu_ref[...])
            o = _dot(h.astype(BF16), wd_ref[...])
            _store_rows(obuf, o, (slot,))

        @pl.when(jnp.logical_and(b == nu, b % N_SLOTS == slot))
        def _(slot=slot, prev=prev, nxt=nxt):
            scatter(b - 1, prev, start)
            scatter(b - 3, slot, wait)
            scatter(b - 2, nxt, wait)
            scatter(b - 1, prev, wait)
            gather(nu - 1, slot, wait)
            gather(nu - 1, nxt, wait)


def _experts(blk_e, n_used, nblk, inv, x, wg, wu, wd, layer):
    d, de = wg.shape[2], wg.shape[3]
    nb = blk_e.shape[0]
    hbm = pl.BlockSpec(memory_space=pl.ANY)
    grid_spec = pltpu.PrefetchScalarGridSpec(
        num_scalar_prefetch=4,
        grid=(nb + 1,),
        in_specs=[hbm, hbm, hbm, hbm],
        out_specs=hbm,
        scratch_shapes=[
            pltpu.VMEM((N_SLOTS, BLK * SUBLANES, LANES), F32),
            pltpu.VMEM((N_SLOTS, BLK * SUBLANES, LANES), F32),
            pltpu.VMEM((2, d, de), F32), pltpu.VMEM((2, d, de), F32), pltpu.VMEM((2, de, d), F32),
            pltpu.VMEM((d, de), BF16), pltpu.VMEM((d, de), BF16), pltpu.VMEM((de, d), BF16),
            pltpu.SemaphoreType.DMA((2, 3)), pltpu.SemaphoreType.DMA((N_SLOTS,)),
            pltpu.SemaphoreType.DMA((N_SLOTS,)), pltpu.SMEM((1,), I32),
        ],
    )
    return pl.pallas_call(
        functools.partial(_experts_body, layer=layer),
        grid_spec=grid_spec,
        out_shape=jax.ShapeDtypeStruct((Y_ROWS * SUBLANES, LANES), F32),
        compiler_params=pltpu.CompilerParams(
            dimension_semantics=("arbitrary",), vmem_limit_bytes=VMEM_LIMIT),
        name="experts",
    )(blk_e, n_used, nblk, inv, x, wg, wu, wd)


def _combine_body(x_ref, ya_ref, yb_ref, w_ref, g_ref, b_ref, o_ref):
    tc = o_ref.shape[0]
    w = w_ref[...]
    y = w[:, 0:1] * _load_rows(ya_ref, tc) + w[:, 1:2] * _load_rows(yb_ref, tc)
    o_ref[...] = _layer_norm(ALPHA * _load_rows(x_ref, tc) + y, g_ref[...], b_ref[...])


def _combine(x, y, w_cols, g, b, tok0, t):
    d = g.shape[1]
    tc = min(TC, t)
    blk0 = tok0 // tc
    plane = (1 << PLANE_BITS) // tc
    row_tiled = lambda first: pl.BlockSpec((tc * SUBLANES, LANES), lambda i: (first + i, 0))
    return pl.pallas_call(
        _combine_body,
        grid=(t // tc,),
        in_specs=[
            row_tiled(blk0), row_tiled(blk0), row_tiled(plane + blk0),
            pl.BlockSpec((tc, 2), lambda i: (blk0 + i, 0)),
            pl.BlockSpec((1, d), lambda i: (0, 0)),
            pl.BlockSpec((1, d), lambda i: (0, 0)),
        ],
        out_specs=pl.BlockSpec((tc, d), lambda i: (i, 0)),
        out_shape=jax.ShapeDtypeStruct((t, d), F32),
        compiler_params=pltpu.CompilerParams(
            dimension_semantics=("arbitrary",), vmem_limit_bytes=VMEM_LIMIT),
        name="combine",
    )(x, y, y, w_cols, g, b)


def _moe_layer(x, tp, ts, w_group, w_expert, wg, wu, wd, layer, g, b):
    d = g.shape[1]
    t_total = tp + ts
    assert PAD_SPAN <= t_total <= 1 << PLANE_BITS and tp % ts == 0
    wrt = jnp.zeros((ROUTE_ROWS, d), F32)
    wrt = wrt.at[0:N_EXPERTS].set(w_expert.T).at[GROUP_ROW0:GROUP_ROW0 + N_GROUPS].set(w_group.T)
    wrt = wrt.astype(BF16)

    zero_cnt = jnp.zeros((N_EXPERTS, LANES), F32)
    ints_p, wts_p, cnt_p = _route(x, wrt, zero_cnt, 0, tp)
    ints_s, wts_s, cnt = _route(x, wrt, cnt_p, tp, ts)
    ints = jnp.concatenate([ints_p, ints_s], axis=1)
    wts = jnp.concatenate([wts_p, wts_s], axis=1)

    counts = cnt[:, 0].astype(I32)
    pcounts = (counts + BLK - 1) // BLK * BLK
    pend = jnp.cumsum(pcounts)
    pstart = pend - pcounts
    nb = (2 * t_total + N_EXPERTS * (BLK - 1) + BLK - 1) // BLK
    p_rows = nb * BLK
    n_used = (pend[-1] // BLK).astype(I32).reshape(1)
    blk_first = jnp.minimum(jnp.arange(nb, dtype=I32), n_used[0] - 1) * BLK
    blk_e = jnp.sum((pend[None, :] <= blk_first[:, None]).astype(I32), axis=1)
    nblk = pcounts // BLK

    pstart_f = jnp.broadcast_to((pstart + INV_ROW0).astype(F32)[:, None], (N_EXPERTS, LANES))
    dest_flat = _dest(ints, pstart_f).reshape(2 * t_total)

    inv = _invert(dest_flat, pstart + counts + INV_ROW0, p_rows)
    y = _experts(blk_e, n_used, nblk, inv, x, wg, wu, wd, layer)
    w_cols = wts.T
    return (_combine(x, y, w_cols, g, b, 0, tp), _combine(x, y, w_cols, g, b, tp, ts))


def kernel(x_prompt, x_sample, state_conv_a, state_conv_b, state_h, a_w_in, a_conv_w, a_w_out,
           b_w_in, b_conv_w, b_conv_b, b_gate_a_w, b_gate_a_b, b_gate_x_w, b_gate_x_b, b_lambda,
           b_w_out, ln1_g, ln1_b, ln2_g, ln2_b, moe_w_group, moe_w_expert, moe_w_gate, moe_w_up,
           moe_w_down):
    bsz, seq, d = x_prompt.shape
    n_s = x_sample.shape[0]
    row = lambda v: v.reshape(1, d)

    tp = bsz * seq
    win, wout = a_w_in[0], a_w_out[0]
    x1, conv_a_p = _conv_a_prompt(x_prompt, win, a_conv_w[0], wout, row(ln1_g[0]), row(ln1_b[0]),
                                  n_s)
    sa = state_conv_a[0]
    x1, u_s = _conv_a_sample(x_sample.reshape(n_s, d), sa[:, 0], sa[:, 1], win, a_conv_w[0], wout,
                             row(ln1_g[0]), row(ln1_b[0]), x1)
    conv_a_s = jnp.stack([sa[:, 1], u_s], axis=1)

    xp, xs = _moe_layer(x1, tp, n_s, moe_w_group[0], moe_w_expert[0],
                        moe_w_gate, moe_w_up, moe_w_down, 0, row(ln2_g[0]), row(ln2_b[0]))

    win, wout = b_w_in[0], b_w_out[0]
    wcat = jnp.concatenate([b_gate_a_w[0], b_gate_x_w[0]], axis=-1).astype(BF16)
    args = (win, b_conv_w[0], row(b_conv_b[0]), wcat, row(b_gate_a_b[0]), row(b_gate_x_b[0]),
            row(b_lambda[0]), wout, row(ln1_g[1]), row(ln1_b[1]))
    x1, conv_b_p, h_p = _rglru_prompt(xp.reshape(bsz, seq, d), *args, n_s)
    sb = state_conv_b[0]
    x1, xr_s, h_s = _rglru_sample(xs, sb[:, 0], sb[:, 1], sb[:, 2], state_h[0], *args, x1)
    conv_b_s = jnp.stack([sb[:, 1], sb[:, 2], xr_s], axis=1)

    xp, xs = _moe_layer(x1, tp, n_s, moe_w_group[1], moe_w_expert[1],
                        moe_w_gate, moe_w_up, moe_w_down, 1, row(ln2_g[1]), row(ln2_b[1]))

    return (xp.reshape(bsz, seq, d), xs.reshape(n_s, 1, d),
            conv_a_p[None], conv_a_s[None], conv_b_p[None], conv_b_s[None],
            h_p.reshape(1, bsz, d), h_s[None])
```

```python
import functools

import jax
import jax.numpy as jnp
from jax import lax
from jax.experimental import pallas as pl
from jax.experimental.pallas import tpu as pltpu

F32 = jnp.float32
BF16 = jnp.bfloat16
I32 = jnp.int32

DEPTH = 2
N_RG_BLOCKS = 8
RG_C = 8.0
N_GROUPS = 4
EXP_PER_GROUP = 8
N_EXPERTS = N_GROUPS * EXP_PER_GROUP
ALPHA = (2.0 * DEPTH) ** 0.25
LN_EPS = 1e-5

LANES = 128
SUBLANES = 8
VMEM_LIMIT = 56 * 1024 * 1024

TS_A = 512
TS_B = 512
TT_ROUTE = 1024
TC = 512
BLK = 256
ROUTE_ROWS = 128
GROUP_ROW0 = N_EXPERTS
W_CHUNK = 512
RG_UNIT = 2
ROW_UNROLL = 8

PLANE_BITS = 15
TOKEN_MASK = (1 << PLANE_BITS) - 1
PAD_BASE = 2 << PLANE_BITS
N_SLOTS = 3
PRIME_BLOCKS = N_SLOTS
INV_ROW0 = PRIME_BLOCKS * BLK
PAD_SPAN = 4 * BLK
SPARE_ROW0 = PAD_BASE + PAD_SPAN
Y_ROWS = SPARE_ROW0 + PRIME_BLOCKS * BLK


def _dot(a, b):
    return jnp.dot(a, b, preferred_element_type=F32)


def _load_rows(ref, m, idx=()):
    return jnp.concatenate(
        [ref[idx + (pl.ds(s, m, stride=SUBLANES), slice(None))] for s in range(SUBLANES)], axis=1)


def _store_rows(ref, v, idx=()):
    m = v.shape[0]
    for s in range(SUBLANES):
        ref[idx + (pl.ds(s, m, stride=SUBLANES), slice(None))] = v[:, s * LANES:(s + 1) * LANES]


def _tile_of_row(ref, r):
    return ref.at[pl.ds(pl.multiple_of(r * SUBLANES, SUBLANES), SUBLANES)]


def _load_weight_bf16(w_hbm, w_bf, stage, sem):
    nch = w_hbm.shape[1] // W_CHUNK

    def chunk_copy(c):
        return pltpu.make_async_copy(w_hbm.at[:, pl.ds(c * W_CHUNK, W_CHUNK)],
                                     stage.at[c % 2], sem.at[c % 2])

    chunk_copy(0).start()
    for c in range(nch):
        if c + 1 < nch:
            chunk_copy(c + 1).start()
        chunk_copy(c).wait()
        w_bf[:, c * W_CHUNK:(c + 1) * W_CHUNK] = stage[c % 2].astype(BF16)


def _weight_scratch(k, *ns):
    return ([pltpu.VMEM((k, n), BF16) for n in ns]
            + [pltpu.VMEM((2, k, W_CHUNK), F32), pltpu.SemaphoreType.DMA((2,))])


def _layer_norm(r, g, b):
    mu = jnp.mean(r, axis=-1, keepdims=True)
    d = r - mu
    var = jnp.mean(d * d, axis=-1, keepdims=True)
    return d * lax.rsqrt(var + LN_EPS) * g + b


def _shift_rows(v, k, prev8):
    rolled = pltpu.roll(v, k, axis=0)
    rows8 = lax.broadcasted_iota(I32, (SUBLANES, v.shape[1]), 0)
    first = jnp.where(rows8 < k, pltpu.roll(prev8, k, axis=0), rolled[0:SUBLANES])
    return jnp.concatenate([first, rolled[SUBLANES:]], axis=0)


def _softplus(v):
    return jnp.maximum(v, 0.0) + jnp.log1p(jnp.exp(-jnp.abs(v)))


def _sigmoid(z):
    return 0.5 * jnp.tanh(0.5 * z) + 0.5


def _rglru_coeffs(xc, wcat_ref, gab, gxb, lam, blk0=0):
    blk = wcat_ref.shape[1]
    xcb = xc.astype(BF16)
    rs, is_ = [], []
    for n in range(xc.shape[1] // blk):
        o = _dot(xcb[:, n * blk:(n + 1) * blk], wcat_ref[blk0 + n])
        rs.append(o[:, :blk])
        is_.append(o[:, blk:])
    r = _sigmoid(jnp.concatenate(rs, axis=1) + gab)
    i = _sigmoid(jnp.concatenate(is_, axis=1) + gxb)
    neg_log_a = RG_C * r * _softplus(-lam)
    a = jnp.exp(-neg_log_a)
    v = jnp.tanh(neg_log_a) * (a * a + 1.0)
    mult = jnp.where(v > 0.0, v * lax.rsqrt(v), 0.0)
    return a, mult * (i * xc)


def _scan_rows(a, b, h0):
    m, d = a.shape
    groups = m // SUBLANES
    a = a.reshape(groups, SUBLANES, d)
    b = b.reshape(groups, SUBLANES, d)
    sub = lax.broadcasted_iota(I32, a.shape, 1)
    for k in (1, 2, 4):
        keep = sub >= k
        a_sh = jnp.where(keep, pltpu.roll(a, k, axis=1), 1.0)
        b_sh = jnp.where(keep, pltpu.roll(b, k, axis=1), 0.0)
        b = a * b_sh + b
        a = a * a_sh
    outs = []
    h = h0
    for g in range(groups):
        hg = a[g] * h + b[g]
        outs.append(hg)
        h = hg[SUBLANES - 1:SUBLANES]
    return jnp.concatenate(outs, axis=0), h


def _conv_a_prompt_body(x_ref, win_hbm, cw_ref, wout_hbm, g_ref, b_ref,
                        o_ref, buf_ref, carry, win_ref, wout_ref, stage, wsem):
    s = pl.program_id(1)

    @pl.when(jnp.logical_and(pl.program_id(0) == 0, s == 0))
    def _():
        _load_weight_bf16(win_hbm, win_ref, stage, wsem)
        _load_weight_bf16(wout_hbm, wout_ref, stage, wsem)

    @pl.when(s == 0)
    def _():
        carry[...] = jnp.zeros_like(carry)

    x = x_ref[0]
    d = x.shape[1]
    bcx = _dot(x.astype(BF16), win_ref[...])
    gb, gc, xh = bcx[:, :d], bcx[:, d:2 * d], bcx[:, 2 * d:]
    u = gc * xh
    prev = carry[...]
    cw = cw_ref[...]
    conv = (cw[0:1] * _shift_rows(u, 2, prev) + cw[1:2] * _shift_rows(u, 1, prev)
            + cw[2:3] * u)
    y = _dot((gb * conv).astype(BF16), wout_ref[...])
    _store_rows(o_ref, _layer_norm(ALPHA * x + y, g_ref[...], b_ref[...]))
    ts = u.shape[0]
    carry[...] = u[ts - SUBLANES:ts]

    @pl.when(s == pl.num_programs(1) - 1)
    def _():
        buf_ref[0] = u[ts - 2:ts]


def _conv_a_prompt(x, win, cw, wout, g, b, n_extra):
    bsz, seq, d = x.shape
    ts = min(TS_A, seq)
    grid = (bsz, seq // ts)
    nj = seq // ts
    return pl.pallas_call(
        _conv_a_prompt_body,
        grid=grid,
        in_specs=[
            pl.BlockSpec((1, ts, d), lambda i, j: (i, j, 0)),
            pl.BlockSpec(memory_space=pl.ANY),
            pl.BlockSpec((3, d), lambda i, j: (0, 0)),
            pl.BlockSpec(memory_space=pl.ANY),
            pl.BlockSpec((1, d), lambda i, j: (0, 0)),
            pl.BlockSpec((1, d), lambda i, j: (0, 0)),
        ],
        out_specs=[
            pl.BlockSpec((ts * SUBLANES, LANES), lambda i, j: (i * nj + j, 0)),
            pl.BlockSpec((1, 2, d), lambda i, j: (i, 0, 0)),
        ],
        out_shape=[
            jax.ShapeDtypeStruct(((bsz * seq + n_extra) * SUBLANES, LANES), F32),
            jax.ShapeDtypeStruct((bsz, 2, d), F32),
        ],
        scratch_shapes=[pltpu.VMEM((SUBLANES, d), F32)] + _weight_scratch(d, 3 * d, d),
        compiler_params=pltpu.CompilerParams(
            dimension_semantics=("arbitrary", "arbitrary"), vmem_limit_bytes=VMEM_LIMIT),
        name="conv_a_prompt",
    )(x, win, cw, wout, g, b)


def _conv_a_sample_body(x_ref, s0_ref, s1_ref, win_hbm, cw_ref, wout_hbm, g_ref, b_ref, joint_ref,
                        o_ref, u_ref, win_ref, wout_ref, stage, wsem):
    del joint_ref
    _load_weight_bf16(win_hbm, win_ref, stage, wsem)
    _load_weight_bf16(wout_hbm, wout_ref, stage, wsem)
    x = x_ref[...]
    d = x.shape[1]
    bcx = _dot(x.astype(BF16), win_ref[...])
    gb, gc, xh = bcx[:, :d], bcx[:, d:2 * d], bcx[:, 2 * d:]
    u = gc * xh
    cw = cw_ref[...]
    conv = cw[0:1] * s0_ref[...] + cw[1:2] * s1_ref[...] + cw[2:3] * u
    y = _dot((gb * conv).astype(BF16), wout_ref[...])
    _store_rows(o_ref, _layer_norm(ALPHA * x + y, g_ref[...], b_ref[...]))
    u_ref[...] = u


def _whole(a):
    return pl.BlockSpec(a.shape, lambda i: (0,) * a.ndim)


def _conv_a_sample(x, s0, s1, win, cw, wout, g, b, joint):
    n, d = x.shape
    hbm = pl.BlockSpec(memory_space=pl.ANY)
    first_block = joint.shape[0] // (n * SUBLANES) - 1
    return pl.pallas_call(
        _conv_a_sample_body,
        grid=(1,),
        in_specs=[_whole(x), _whole(s0), _whole(s1), hbm, _whole(cw), hbm, _whole(g), _whole(b),
                  hbm],
        out_specs=[pl.BlockSpec((n * SUBLANES, LANES), lambda i: (first_block, 0)),
                   pl.BlockSpec((n, d), lambda i: (0, 0))],
        out_shape=[jax.ShapeDtypeStruct(joint.shape, F32), jax.ShapeDtypeStruct((n, d), F32)],
        scratch_shapes=_weight_scratch(d, 3 * d, d),
        input_output_aliases={8: 0},
        compiler_params=pltpu.CompilerParams(
            dimension_semantics=("arbitrary",), vmem_limit_bytes=VMEM_LIMIT),
        name="conv_a_sample",
    )(x, s0, s1, win, cw, wout, g, b, joint)


def _rglru_prompt_body(x_ref, xn_ref, win_hbm, cw_ref, cb_ref, wcat_ref, gab_ref, gxb_ref, lam_ref,
                       wout_hbm, g_ref, b_ref, o_ref, buf_ref, hl_ref, xcarry, hcarry, gx_scr,
                       win_ref, wout_ref, stage, wsem, *, nj):
    i = pl.program_id(0)
    s = i % nj

    @pl.when(i == 0)
    def _():
        _load_weight_bf16(win_hbm, win_ref, stage, wsem)
        _load_weight_bf16(wout_hbm, wout_ref, stage, wsem)
        gx_scr[0] = _dot(x_ref[0].astype(BF16), win_ref[...])

    @pl.when(s == 0)
    def _():
        xcarry[...] = jnp.zeros_like(xcarry)
        hcarry[...] = jnp.zeros_like(hcarry)

    for par in range(2):
        @pl.when(i % 2 == par)
        def _(par=par):
            x = x_ref[0]
            ts, d = x.shape
            xn = xn_ref[0].astype(BF16)
            blk = d // N_RG_BLOCKS
            zs = []
            for u in range(N_RG_BLOCKS // RG_UNIT):
                c0, c1 = u * RG_UNIT * blk, (u + 1) * RG_UNIT * blk
                for half in (0, d):
                    gx_scr[1 - par, :, half + c0:half + c1] = _dot(
                        xn, win_ref[:, half + c0:half + c1])
                gate, xr = gx_scr[par, :, c0:c1], gx_scr[par, :, d + c0:d + c1]
                prev = xcarry[:, c0:c1]
                cw = cw_ref[:, c0:c1]
                xc = (cw[0:1] * _shift_rows(xr, 3, prev) + cw[1:2] * _shift_rows(xr, 2, prev)
                      + cw[2:3] * _shift_rows(xr, 1, prev) + cw[3:4] * xr) + cb_ref[:, c0:c1]
                a, bt = _rglru_coeffs(xc, wcat_ref, gab_ref[:, c0:c1], gxb_ref[:, c0:c1],
                                      lam_ref[:, c0:c1], u * RG_UNIT)
                hs, hlast = _scan_rows(a, bt, hcarry[0:1, c0:c1])
                zs.append((jax.nn.gelu(gate, approximate=True) * hs).astype(BF16))
                xcarry[:, c0:c1] = xr[ts - SUBLANES:ts]
                hcarry[:, c0:c1] = jnp.broadcast_to(hlast, (SUBLANES, c1 - c0))

            y = _dot(jnp.concatenate(zs, axis=1), wout_ref[...])
            _store_rows(o_ref, _layer_norm(ALPHA * x + y, g_ref[...], b_ref[...]))

    @pl.when(s == nj - 1)
    def _():
        buf_ref[0] = xcarry[SUBLANES - 3:SUBLANES]
        hl_ref[0] = hcarry[0:1]


def _rglru_prompt(x, win, cw, cb, wcat, gab, gxb, lam, wout, g, b, n_extra):
    bsz, seq, d = x.shape
    ts = min(TS_B, seq)
    nj = seq // ts
    n = bsz * nj
    blk = d // N_RG_BLOCKS
    const2 = lambda i: (0, 0)
    tile = lambda i: (i // nj, i % nj, 0)
    return pl.pallas_call(
        functools.partial(_rglru_prompt_body, nj=nj),
        grid=(n,),
        in_specs=[
            pl.BlockSpec((1, ts, d), tile),
            pl.BlockSpec((1, ts, d), lambda i: tile(jnp.minimum(i + 1, n - 1))),
            pl.BlockSpec(memory_space=pl.ANY),
            pl.BlockSpec((4, d), const2),
            pl.BlockSpec((1, d), const2),
            pl.BlockSpec((N_RG_BLOCKS, blk, 2 * blk), lambda i: (0, 0, 0)),
            pl.BlockSpec((1, d), const2),
            pl.BlockSpec((1, d), const2),
            pl.BlockSpec((1, d), const2),
            pl.BlockSpec(memory_space=pl.ANY),
            pl.BlockSpec((1, d), const2),
            pl.BlockSpec((1, d), const2),
        ],
        out_specs=[
            pl.BlockSpec((ts * SUBLANES, LANES), lambda i: (i, 0)),
            pl.BlockSpec((1, 3, d), lambda i: (i // nj, 0, 0)),
            pl.BlockSpec((1, 1, d), lambda i: (i // nj, 0, 0)),
        ],
        out_shape=[
            jax.ShapeDtypeStruct(((bsz * seq + n_extra) * SUBLANES, LANES), F32),
            jax.ShapeDtypeStruct((bsz, 3, d), F32),
            jax.ShapeDtypeStruct((bsz, 1, d), F32),
        ],
        scratch_shapes=([pltpu.VMEM((SUBLANES, d), F32), pltpu.VMEM((SUBLANES, d), F32),
                         pltpu.VMEM((2, ts, 2 * d), F32)]
                        + _weight_scratch(d, 2 * d, d)),
        compiler_params=pltpu.CompilerParams(
            dimension_semantics=("arbitrary",), vmem_limit_bytes=VMEM_LIMIT),
        name="rglru_prompt",
    )(x, x, win, cw, cb, wcat, gab, gxb, lam, wout, g, b)


def _rglru_sample_body(x_ref, s0_ref, s1_ref, s2_ref, h0_ref, win_hbm, cw_ref, cb_ref, wcat_ref,
                       gab_ref, gxb_ref, lam_ref, wout_hbm, g_ref, b_ref, joint_ref,
                       o_ref, xr_ref, h_ref, win_ref, wout_ref, stage, wsem):
    del joint_ref
    _load_weight_bf16(win_hbm, win_ref, stage, wsem)
    _load_weight_bf16(wout_hbm, wout_ref, stage, wsem)
    x = x_ref[...]
    d = x.shape[1]
    gx = _dot(x.astype(BF16), win_ref[...])
    gate, xr = gx[:, :d], gx[:, d:]
    cw = cw_ref[...]
    xc = (cw[0:1] * s0_ref[...] + cw[1:2] * s1_ref[...] + cw[2:3] * s2_ref[...]
          + cw[3:4] * xr) + cb_ref[...]
    a, bt = _rglru_coeffs(xc, wcat_ref, gab_ref[...], gxb_ref[...], lam_ref[...])
    h = a * h0_ref[...] + bt
    y = _dot((jax.nn.gelu(gate, approximate=True) * h).astype(BF16), wout_ref[...])
    _store_rows(o_ref, _layer_norm(ALPHA * x + y, g_ref[...], b_ref[...]))
    xr_ref[...] = xr
    h_ref[...] = h


def _rglru_sample(x, s0, s1, s2, h0, win, cw, cb, wcat, gab, gxb, lam, wout, g, b, joint):
    n, d = x.shape
    hbm = pl.BlockSpec(memory_space=pl.ANY)
    first_block = joint.shape[0] // (n * SUBLANES) - 1
    vec = pl.BlockSpec((n, d), lambda i: (0, 0))
    return pl.pallas_call(
        _rglru_sample_body,
        grid=(1,),
        in_specs=[_whole(x), _whole(s0), _whole(s1), _whole(s2), _whole(h0), hbm, _whole(cw),
                  _whole(cb), _whole(wcat), _whole(gab), _whole(gxb), _whole(lam), hbm,
                  _whole(g), _whole(b), hbm],
        out_specs=[pl.BlockSpec((n * SUBLANES, LANES), lambda i: (first_block, 0)), vec, vec],
        out_shape=[jax.ShapeDtypeStruct(joint.shape, F32),
                   jax.ShapeDtypeStruct((n, d), F32), jax.ShapeDtypeStruct((n, d), F32)],
        scratch_shapes=_weight_scratch(d, 2 * d, d),
        input_output_aliases={15: 0},
        compiler_params=pltpu.CompilerParams(
            dimension_semantics=("arbitrary",), vmem_limit_bytes=VMEM_LIMIT),
        name="rglru_sample",
    )(x, s0, s1, s2, h0, win, cw, cb, wcat, gab, gxb, lam, wout, g, b, joint)


def _first_argmax(v, rows):
    m = jnp.max(v, axis=0, keepdims=True)
    idx = jnp.min(jnp.where(v == m, rows, v.shape[0]), axis=0, keepdims=True)
    return m, idx


def _route_body(x_ref, wrt_ref, tri_ref, cin_ref, ints_ref, wts_ref, cnt_ref, carry):
    @pl.when(pl.program_id(0) == 0)
    def _():
        carry[...] = cin_ref[...]

    tt = x_ref.shape[0] // SUBLANES
    xb = _load_rows(x_ref, tt).astype(BF16)
    lt = lax.dot_general(wrt_ref[...], xb, (((1,), (1,)), ((), ())), preferred_element_type=F32)
    rows8 = lax.broadcasted_iota(I32, (SUBLANES, tt), 0)
    neg_inf = jnp.float32(-jnp.inf)

    gl = jnp.where(rows8 < N_GROUPS, lt[GROUP_ROW0:GROUP_ROW0 + SUBLANES], neg_inf)
    gmax, gidx = _first_argmax(gl, rows8)
    gw = 1.0 / jnp.sum(jnp.exp(gl - gmax), axis=0, keepdims=True)

    el = lt[0:EXP_PER_GROUP]
    for g in range(1, N_GROUPS):
        el = jnp.where(gidx == g, lt[g * EXP_PER_GROUP:(g + 1) * EXP_PER_GROUP], el)
    emax, i1 = _first_argmax(el, rows8)
    el2 = jnp.where(rows8 == i1, neg_inf, el)
    m2, i2 = _first_argmax(el2, rows8)
    psum = jnp.sum(jnp.exp(el - emax), axis=0, keepdims=True)
    ep1 = 1.0 / psum
    ep2 = jnp.exp(m2 - emax) / psum
    tot = ep1 + ep2
    wa = gw * (ep1 / tot)
    wb = gw * (ep2 / tot)
    ea = gidx * EXP_PER_GROUP + i1
    eb = gidx * EXP_PER_GROUP + i2

    rows_e = lax.broadcasted_iota(I32, (N_EXPERTS, tt), 0)
    oha = rows_e == ea
    ohb = rows_e == eb
    oh = jnp.where(oha | ohb, 1.0, 0.0)
    base = carry[...][:, 0:1]
    excl = _dot(oh.astype(BF16), tri_ref[...]) + base
    ra = jnp.sum(jnp.where(oha, excl, 0.0), axis=0, keepdims=True)
    rb = jnp.sum(jnp.where(ohb, excl, 0.0), axis=0, keepdims=True)
    new = carry[...] + jnp.sum(oh, axis=1, keepdims=True)
    carry[...] = new
    cnt_ref[...] = new

    ints_ref[0:1, :] = ea
    ints_ref[1:2, :] = eb
    ints_ref[2:3, :] = ra.astype(I32)
    ints_ref[3:4, :] = rb.astype(I32)
    wts_ref[0:1, :] = wa
    wts_ref[1:2, :] = wb


def _route(x, wrt, cin, tok0, t):
    d = wrt.shape[1]
    tt = min(TT_ROUTE, t)
    blk0 = tok0 // tt
    tri = (jnp.arange(tt)[:, None] < jnp.arange(tt)[None, :]).astype(BF16)
    return pl.pallas_call(
        _route_body,
        grid=(t // tt,),
        in_specs=[
            pl.BlockSpec((tt * SUBLANES, LANES), lambda i: (blk0 + i, 0)),
            pl.BlockSpec((ROUTE_ROWS, d), lambda i: (0, 0)),
            pl.BlockSpec((tt, tt), lambda i: (0, 0)),
            pl.BlockSpec((N_EXPERTS, LANES), lambda i: (0, 0)),
        ],
        out_specs=[
            pl.BlockSpec((4, tt), lambda i: (0, i)),
            pl.BlockSpec((2, tt), lambda i: (0, i)),
            pl.BlockSpec((N_EXPERTS, LANES), lambda i: (0, 0)),
        ],
        out_shape=[
            jax.ShapeDtypeStruct((4, t), I32),
            jax.ShapeDtypeStruct((2, t), F32),
            jax.ShapeDtypeStruct((N_EXPERTS, LANES), F32),
        ],
        scratch_shapes=[pltpu.VMEM((N_EXPERTS, LANES), F32)],
        compiler_params=pltpu.CompilerParams(
            dimension_semantics=("arbitrary",), vmem_limit_bytes=VMEM_LIMIT),
        name="route",
    )(x, wrt, tri, cin)


def _dest_body(ints_ref, pst_ref, dest_ref):
    ints = ints_ref[...]
    tt = ints.shape[1]
    rows_e = lax.broadcasted_iota(I32, (N_EXPERTS, tt), 0)
    pst = pst_ref[...][:, 0:1]
    for k in range(2):
        start = jnp.sum(jnp.where(rows_e == ints[k:k + 1], pst, 0.0), axis=0, keepdims=True)
        dest_ref[k:k + 1, :] = start.astype(I32) + ints[2 + k:3 + k]


def _dest(ints, pstart_f):
    t = ints.shape[1]
    return pl.pallas_call(
        _dest_body,
        out_shape=jax.ShapeDtypeStruct((2, t), I32),
        compiler_params=pltpu.CompilerParams(vmem_limit_bytes=VMEM_LIMIT),
        name="dest",
    )(ints, pstart_f)


def _invert_body(dest_ref, padpos_ref, inv_ref, *, t_total):
    def prime_body(i, c):
        inv_ref[i] = SPARE_ROW0 + i
        return c

    lax.fori_loop(0, INV_ROW0, prime_body, 0, unroll=8)

    def pad_body(e, c):
        q0 = padpos_ref[e]
        for r in range(BLK):
            inv_ref[q0 + r] = PAD_BASE + ((q0 + r) & (PAD_SPAN - 1))
        return c

    lax.fori_loop(0, N_EXPERTS, pad_body, 0)

    def tok_body(j, c):
        toks = [j * ROW_UNROLL + u for u in range(ROW_UNROLL)]
        rows = [[dest_ref[k * t_total + t] for k in range(2)] for t in toks]
        for t, qs in zip(toks, rows):
            for k in range(2):
                inv_ref[qs[k]] = t + (k << PLANE_BITS)
        return c

    lax.fori_loop(0, t_total // ROW_UNROLL, tok_body, 0)


def _invert(dest_flat, pad_pos, p_rows):
    t_total = dest_flat.shape[0] // 2
    grid_spec = pltpu.PrefetchScalarGridSpec(
        num_scalar_prefetch=2,
        grid=(1,),
        in_specs=[],
        out_specs=pl.BlockSpec(memory_space=pltpu.SMEM),
    )
    return pl.pallas_call(
        functools.partial(_invert_body, t_total=t_total),
        grid_spec=grid_spec,
        out_shape=jax.ShapeDtypeStruct((INV_ROW0 + p_rows + BLK,), I32),
        compiler_params=pltpu.CompilerParams(dimension_semantics=("arbitrary",)),
        name="invert",
    )(dest_flat, pad_pos)


def _experts_body(be_ref, nu_ref, nblk_ref, inv_ref, x_hbm, wg_hbm, wu_hbm, wd_hbm, y_hbm,
                  xbuf, obuf, sg, su, sd, wg_ref, wu_ref, wd_ref, wsem, gsem, ssem, slot_ref,
                  *, layer):
    b = pl.program_id(0)
    nu = nu_ref[0]
    e = be_ref[jnp.minimum(b, nu - 1)]

    def gather(blk, slot, fn):
        vs = [inv_ref[(blk + PRIME_BLOCKS) * BLK + r] for r in range(BLK)]
        for r, v in enumerate(vs):
            fn(pltpu.make_async_copy(_tile_of_row(x_hbm, v & TOKEN_MASK),
                                     _tile_of_row(xbuf.at[slot], r), gsem.at[slot]), 0)

    def scatter(blk, slot, fn):
        vs = [inv_ref[(blk + PRIME_BLOCKS) * BLK + r] for r in range(BLK)]
        for r, v in enumerate(vs):
            fn(pltpu.make_async_copy(_tile_of_row(obuf.at[slot], r),
                                     _tile_of_row(y_hbm, v), ssem.at[slot]), 1)

    def start(cp, priority):
        cp.start(priority=priority)

    def wait(cp, priority):
        cp.wait()

    def fetch(ex, slot):
        return (pltpu.make_async_copy(wg_hbm.at[layer, ex], sg.at[slot], wsem.at[slot, 0]),
                pltpu.make_async_copy(wu_hbm.at[layer, ex], su.at[slot], wsem.at[slot, 1]),
                pltpu.make_async_copy(wd_hbm.at[layer, ex], sd.at[slot], wsem.at[slot, 2]))

    @pl.when(b == 0)
    def _():
        slot_ref[0] = 0
        for cp in fetch(e, 0):
            cp.start()
        obuf[...] = jnp.zeros_like(obuf)
        scatter(-3, 0, start)
        scatter(-2, 1, start)
        gather(0, 0, start)
        gather(jnp.minimum(1, nu - 1), 1, start)

    first_of_expert = jnp.logical_or(b == 0, e != be_ref[jnp.maximum(b - 1, 0)])

    @pl.when(jnp.logical_and(b < nu, first_of_expert))
    def _():
        slot = slot_ref[0]
        nxt = b + nblk_ref[e]

        @pl.when(nxt < nu)
        def _():
            for cp in fetch(be_ref[nxt], 1 - slot):
                cp.start()

        for cp in fetch(e, slot):
            cp.wait()
        wg_ref[...] = sg[slot].astype(BF16)
        wu_ref[...] = su[slot].astype(BF16)
        wd_ref[...] = sd[slot].astype(BF16)
        slot_ref[0] = 1 - slot

    for slot in range(N_SLOTS):
        prev, nxt = (slot - 1) % N_SLOTS, (slot + 1) % N_SLOTS

        @pl.when(jnp.logical_and(b < nu, b % N_SLOTS == slot))
        def _(slot=slot, prev=prev):
            gather(b, slot, wait)
            scatter(b - 3, slot, wait)
            xb = _load_rows(xbuf, BLK, (slot,)).astype(BF16)
            scatter(b - 1, prev, start)
            gather(jnp.minimum(b + 2, nu - 1), prev, start)
            h = jax.nn.silu(_dot(xb, wg_ref[...])) * _dot(xb, wu_ref[...])
            o = _dot(h.astype(BF16), wd_ref[...])
            _store_rows(obuf, o, (slot,))

        @pl.when(jnp.logical_and(b == nu, b % N_SLOTS == slot))
        def _(slot=slot, prev=prev, nxt=nxt):
            scatter(b - 1, prev, start)
            scatter(b - 3, slot, wait)
            scatter(b - 2, nxt, wait)
            scatter(b - 1, prev, wait)
            gather(nu - 1, slot, wait)
            gather(nu - 1, nxt, wait)


def _experts(blk_e, n_used, nblk, inv, x, wg, wu, wd, layer):
    d, de = wg.shape[2], wg.shape[3]
    nb = blk_e.shape[0]
    hbm = pl.BlockSpec(memory_space=pl.ANY)
    grid_spec = pltpu.PrefetchScalarGridSpec(
        num_scalar_prefetch=4,
        grid=(nb + 1,),
        in_specs=[hbm, hbm, hbm, hbm],
        out_specs=hbm,
        scratch_shapes=[
            pltpu.VMEM((N_SLOTS, BLK * SUBLANES, LANES), F32),
            pltpu.VMEM((N_SLOTS, BLK * SUBLANES, LANES), F32),
            pltpu.VMEM((2, d, de), F32), pltpu.VMEM((2, d, de), F32), pltpu.VMEM((2, de, d), F32),
            pltpu.VMEM((d, de), BF16), pltpu.VMEM((d, de), BF16), pltpu.VMEM((de, d), BF16),
            pltpu.SemaphoreType.DMA((2, 3)), pltpu.SemaphoreType.DMA((N_SLOTS,)),
            pltpu.SemaphoreType.DMA((N_SLOTS,)), pltpu.SMEM((1,), I32),
        ],
    )
    return pl.pallas_call(
        functools.partial(_experts_body, layer=layer),
        grid_spec=grid_spec,
        out_shape=jax.ShapeDtypeStruct((Y_ROWS * SUBLANES, LANES), F32),
        compiler_params=pltpu.CompilerParams(
            dimension_semantics=("arbitrary",), vmem_limit_bytes=VMEM_LIMIT),
        name="experts",
    )(blk_e, n_used, nblk, inv, x, wg, wu, wd)


def _combine_body(x_ref, ya_ref, yb_ref, w_ref, g_ref, b_ref, o_ref):
    tc = o_ref.shape[0]
    w = w_ref[...]
    y = w[:, 0:1] * _load_rows(ya_ref, tc) + w[:, 1:2] * _load_rows(yb_ref, tc)
    o_ref[...] = _layer_norm(ALPHA * _load_rows(x_ref, tc) + y, g_ref[...], b_ref[...])


def _combine(x, y, w_cols, g, b, tok0, t):
    d = g.shape[1]
    tc = min(TC, t)
    blk0 = tok0 // tc
    plane = (1 << PLANE_BITS) // tc
    row_tiled = lambda first: pl.BlockSpec((tc * SUBLANES, LANES), lambda i: (first + i, 0))
    return pl.pallas_call(
        _combine_body,
        grid=(t // tc,),
        in_specs=[
            row_tiled(blk0), row_tiled(blk0), row_tiled(plane + blk0),
            pl.BlockSpec((tc, 2), lambda i: (blk0 + i, 0)),
            pl.BlockSpec((1, d), lambda i: (0, 0)),
            pl.BlockSpec((1, d), lambda i: (0, 0)),
        ],
        out_specs=pl.BlockSpec((tc, d), lambda i: (i, 0)),
        out_shape=jax.ShapeDtypeStruct((t, d), F32),
        compiler_params=pltpu.CompilerParams(
            dimension_semantics=("arbitrary",), vmem_limit_bytes=VMEM_LIMIT),
        name="combine",
    )(x, y, y, w_cols, g, b)


def _moe_layer(x, tp, ts, w_group, w_expert, wg, wu, wd, layer, g, b):
    d = g.shape[1]
    t_total = tp + ts
    assert PAD_SPAN <= t_total <= 1 << PLANE_BITS and tp % ts == 0
    wrt = jnp.zeros((ROUTE_ROWS, d), F32)
    wrt = wrt.at[0:N_EXPERTS].set(w_expert.T).at[GROUP_ROW0:GROUP_ROW0 + N_GROUPS].set(w_group.T)
    wrt = wrt.astype(BF16)

    zero_cnt = jnp.zeros((N_EXPERTS, LANES), F32)
    ints_p, wts_p, cnt_p = _route(x, wrt, zero_cnt, 0, tp)
    ints_s, wts_s, cnt = _route(x, wrt, cnt_p, tp, ts)
    ints = jnp.concatenate([ints_p, ints_s], axis=1)
    wts = jnp.concatenate([wts_p, wts_s], axis=1)

    counts = cnt[:, 0].astype(I32)
    pcounts = (counts + BLK - 1) // BLK * BLK
    pend = jnp.cumsum(pcounts)
    pstart = pend - pcounts
    nb = (2 * t_total + N_EXPERTS * (BLK - 1) + BLK - 1) // BLK
    p_rows = nb * BLK
    n_used = (pend[-1] // BLK).astype(I32).reshape(1)
    blk_first = jnp.minimum(jnp.arange(nb, dtype=I32), n_used[0] - 1) * BLK
    blk_e = jnp.sum((pend[None, :] <= blk_first[:, None]).astype(I32), axis=1)
    nblk = pcounts // BLK

    pstart_f = jnp.broadcast_to((pstart + INV_ROW0).astype(F32)[:, None], (N_EXPERTS, LANES))
    dest_flat = _dest(ints, pstart_f).reshape(2 * t_total)

    inv = _invert(dest_flat, pstart + counts + INV_ROW0, p_rows)
    y = _experts(blk_e, n_used, nblk, inv, x, wg, wu, wd, layer)
    w_cols = wts.T
    return (_combine(x, y, w_cols, g, b, 0, tp), _combine(x, y, w_cols, g, b, tp, ts))


def kernel(x_prompt, x_sample, state_conv_a, state_conv_b, state_h, a_w_in, a_conv_w, a_w_out,
           b_w_in, b_conv_w, b_conv_b, b_gate_a_w, b_gate_a_b, b_gate_x_w, b_gate_x_b, b_lambda,
           b_w_out, ln1_g, ln1_b, ln2_g, ln2_b, moe_w_group, moe_w_expert, moe_w_gate, moe_w_up,
           moe_w_down):
    bsz, seq, d = x_prompt.shape
    n_s = x_sample.shape[0]
    row = lambda v: v.reshape(1, d)

    tp = bsz * seq
    win, wout = a_w_in[0], a_w_out[0]
    x1, conv_a_p = _conv_a_prompt(x_prompt, win, a_conv_w[0], wout, row(ln1_g[0]), row(ln1_b[0]),
                                  n_s)
    sa = state_conv_a[0]
    x1, u_s = _conv_a_sample(x_sample.reshape(n_s, d), sa[:, 0], sa[:, 1], win, a_conv_w[0], wout,
                             row(ln1_g[0]), row(ln1_b[0]), x1)
    conv_a_s = jnp.stack([sa[:, 1], u_s], axis=1)

    xp, xs = _moe_layer(x1, tp, n_s, moe_w_group[0], moe_w_expert[0],
                        moe_w_gate, moe_w_up, moe_w_down, 0, row(ln2_g[0]), row(ln2_b[0]))

    win, wout = b_w_in[0], b_w_out[0]
    wcat = jnp.concatenate([b_gate_a_w[0], b_gate_x_w[0]], axis=-1).astype(BF16)
    args = (win, b_conv_w[0], row(b_conv_b[0]), wcat, row(b_gate_a_b[0]), row(b_gate_x_b[0]),
            row(b_lambda[0]), wout, row(ln1_g[1]), row(ln1_b[1]))
    x1, conv_b_p, h_p = _rglru_prompt(xp.reshape(bsz, seq, d), *args, n_s)
    sb = state_conv_b[0]
    x1, xr_s, h_s = _rglru_sample(xs, sb[:, 0], sb[:, 1], sb[:, 2], state_h[0], *args, x1)
    conv_b_s = jnp.stack([sb[:, 1], sb[:, 2], xr_s], axis=1)

    xp, xs = _moe_layer(x1, tp, n_s, moe_w_group[1], moe_w_expert[1],
                        moe_w_gate, moe_w_up, moe_w_down, 1, row(ln2_g[1]), row(ln2_b[1]))

    return (xp.reshape(bsz, seq, d), xs.reshape(n_s, 1, d),
            conv_a_p[None], conv_a_s[None], conv_b_p[None], conv_b_s[None],
            h_p.reshape(1, bsz, d), h_s[None])
```

```python
import functools

import jax
import jax.numpy as jnp
from jax import lax
from jax.experimental import pallas as pl
from jax.experimental.pallas import tpu as pltpu

F32 = jnp.float32
BF16 = jnp.bfloat16
I32 = jnp.int32

DEPTH = 2
N_RG_BLOCKS = 8
RG_C = 8.0
N_GROUPS = 4
EXP_PER_GROUP = 8
N_EXPERTS = N_GROUPS * EXP_PER_GROUP
ALPHA = (2.0 * DEPTH) ** 0.25
LN_EPS = 1e-5

LANES = 128
SUBLANES = 8
VMEM_LIMIT = 56 * 1024 * 1024

TS_A = 512
TS_B = 512
TT_ROUTE = 1024
TC = 512
BLK = 256
ROUTE_ROWS = 128
GROUP_ROW0 = N_EXPERTS
W_CHUNK = 512
RG_UNIT = 2
ROW_UNROLL = 8

PLANE_BITS = 15
TOKEN_MASK = (1 << PLANE_BITS) - 1
PAD_BASE = 2 << PLANE_BITS
N_SLOTS = 3
PRIME_BLOCKS = N_SLOTS
INV_ROW0 = PRIME_BLOCKS * BLK
PAD_SPAN = 4 * BLK
SPARE_ROW0 = PAD_BASE + PAD_SPAN
Y_ROWS = SPARE_ROW0 + PRIME_BLOCKS * BLK


def _dot(a, b):
    return jnp.dot(a, b, preferred_element_type=F32)


def _load_rows(ref, m, idx=()):
    return jnp.concatenate(
        [ref[idx + (pl.ds(s, m, stride=SUBLANES), slice(None))] for s in range(SUBLANES)], axis=1)


def _store_rows(ref, v, idx=()):
    m = v.shape[0]
    for s in range(SUBLANES):
        ref[idx + (pl.ds(s, m, stride=SUBLANES), slice(None))] = v[:, s * LANES:(s + 1) * LANES]


def _tile_of_row(ref, r):
    return ref.at[pl.ds(pl.multiple_of(r * SUBLANES, SUBLANES), SUBLANES)]


def _load_weight_bf16(w_hbm, w_bf, stage, sem):
    nch = w_hbm.shape[1] // W_CHUNK

    def chunk_copy(c):
        return pltpu.make_async_copy(w_hbm.at[:, pl.ds(c * W_CHUNK, W_CHUNK)],
                                     stage.at[c % 2], sem.at[c % 2])

    chunk_copy(0).start()
    for c in range(nch):
        if c + 1 < nch:
            chunk_copy(c + 1).start()
        chunk_copy(c).wait()
        w_bf[:, c * W_CHUNK:(c + 1) * W_CHUNK] = stage[c % 2].astype(BF16)


def _weight_scratch(k, *ns):
    return ([pltpu.VMEM((k, n), BF16) for n in ns]
            + [pltpu.VMEM((2, k, W_CHUNK), F32), pltpu.SemaphoreType.DMA((2,))])


def _layer_norm(r, g, b):
    mu = jnp.mean(r, axis=-1, keepdims=True)
    d = r - mu
    var = jnp.mean(d * d, axis=-1, keepdims=True)
    return d * lax.rsqrt(var + LN_EPS) * g + b


def _shift_rows(v, k, prev8):
    rolled = pltpu.roll(v, k, axis=0)
    rows8 = lax.broadcasted_iota(I32, (SUBLANES, v.shape[1]), 0)
    first = jnp.where(rows8 < k, pltpu.roll(prev8, k, axis=0), rolled[0:SUBLANES])
    return jnp.concatenate([first, rolled[SUBLANES:]], axis=0)


def _softplus(v):
    return jnp.maximum(v, 0.0) + jnp.log1p(jnp.exp(-jnp.abs(v)))


def _sigmoid(z):
    return 0.5 * jnp.tanh(0.5 * z) + 0.5


def _rglru_coeffs(xc, wcat_ref, gab, gxb, lam, blk0=0):
    blk = wcat_ref.shape[1]
    xcb = xc.astype(BF16)
    rs, is_ = [], []
    for n in range(xc.shape[1] // blk):
        o = _dot(xcb[:, n * blk:(n + 1) * blk], wcat_ref[blk0 + n])
        rs.append(o[:, :blk])
        is_.append(o[:, blk:])
    r = _sigmoid(jnp.concatenate(rs, axis=1) + gab)
    i = _sigmoid(jnp.concatenate(is_, axis=1) + gxb)
    neg_log_a = RG_C * r * _softplus(-lam)
    a = jnp.exp(-neg_log_a)
    v = jnp.tanh(neg_log_a) * (a * a + 1.0)
    mult = jnp.where(v > 0.0, v * lax.rsqrt(v), 0.0)
    return a, mult * (i * xc)


def _scan_rows(a, b, h0):
    m, d = a.shape
    groups = m // SUBLANES
    a = a.reshape(groups, SUBLANES, d)
    b = b.reshape(groups, SUBLANES, d)
    sub = lax.broadcasted_iota(I32, a.shape, 1)
    for k in (1, 2, 4):
        keep = sub >= k
        a_sh = jnp.where(keep, pltpu.roll(a, k, axis=1), 1.0)
        b_sh = jnp.where(keep, pltpu.roll(b, k, axis=1), 0.0)
        b = a * b_sh + b
        a = a * a_sh
    outs = []
    h = h0
    for g in range(groups):
        hg = a[g] * h + b[g]
        outs.append(hg)
        h = hg[SUBLANES - 1:SUBLANES]
    return jnp.concatenate(outs, axis=0), h


def _conv_a_prompt_body(x_ref, win_hbm, cw_ref, wout_hbm, g_ref, b_ref, wrt_ref, tri_ref,
                        o_ref, buf_ref, ints_ref, wts_ref, cnt_ref,
                        carry, counts, win_ref, wout_ref, stage, wsem):
    s = pl.program_id(1)

    @pl.when(jnp.logical_and(pl.program_id(0) == 0, s == 0))
    def _():
        _load_weight_bf16(win_hbm, win_ref, stage, wsem)
        _load_weight_bf16(wout_hbm, wout_ref, stage, wsem)
        counts[...] = jnp.zeros_like(counts)

    @pl.when(s == 0)
    def _():
        carry[...] = jnp.zeros_like(carry)

    x = x_ref[0]
    d = x.shape[1]
    bcx = _dot(x.astype(BF16), win_ref[...])
    gb, gc, xh = bcx[:, :d], bcx[:, d:2 * d], bcx[:, 2 * d:]
    u = gc * xh
    prev = carry[...]
    cw = cw_ref[...]
    conv = (cw[0:1] * _shift_rows(u, 2, prev) + cw[1:2] * _shift_rows(u, 1, prev)
            + cw[2:3] * u)
    y = _dot((gb * conv).astype(BF16), wout_ref[...])
    x1 = _layer_norm(ALPHA * x + y, g_ref[...], b_ref[...])
    _store_rows(o_ref, x1)
    _route_tile(x1.astype(BF16), wrt_ref, tri_ref, counts, ints_ref, wts_ref, cnt_ref)
    ts = u.shape[0]
    carry[...] = u[ts - SUBLANES:ts]

    @pl.when(s == pl.num_programs(1) - 1)
    def _():
        buf_ref[0] = u[ts - 2:ts]


def _prefix_matrix(tt):
    return (jnp.arange(tt)[:, None] < jnp.arange(tt)[None, :]).astype(BF16)


def _conv_a_prompt(x, win, cw, wout, g, b, wrt, n_extra):
    bsz, seq, d = x.shape
    ts = min(TS_A, seq)
    grid = (bsz, seq // ts)
    nj = seq // ts
    tp = bsz * seq
    const2 = lambda i, j: (0, 0)
    return pl.pallas_call(
        _conv_a_prompt_body,
        grid=grid,
        in_specs=[
            pl.BlockSpec((1, ts, d), lambda i, j: (i, j, 0)),
            pl.BlockSpec(memory_space=pl.ANY),
            pl.BlockSpec((3, d), const2),
            pl.BlockSpec(memory_space=pl.ANY),
            pl.BlockSpec((1, d), const2),
            pl.BlockSpec((1, d), const2),
            pl.BlockSpec((ROUTE_ROWS, d), const2),
            pl.BlockSpec((ts, ts), const2),
        ],
        out_specs=[
            pl.BlockSpec((ts * SUBLANES, LANES), lambda i, j: (i * nj + j, 0)),
            pl.BlockSpec((1, 2, d), lambda i, j: (i, 0, 0)),
            pl.BlockSpec((4, ts), lambda i, j: (0, i * nj + j)),
            pl.BlockSpec((2, ts), lambda i, j: (0, i * nj + j)),
            pl.BlockSpec((N_EXPERTS, LANES), const2),
        ],
        out_shape=[
            jax.ShapeDtypeStruct(((tp + n_extra) * SUBLANES, LANES), F32),
            jax.ShapeDtypeStruct((bsz, 2, d), F32),
            jax.ShapeDtypeStruct((4, tp), I32),
            jax.ShapeDtypeStruct((2, tp), F32),
            jax.ShapeDtypeStruct((N_EXPERTS, LANES), F32),
        ],
        scratch_shapes=([pltpu.VMEM((SUBLANES, d), F32), pltpu.VMEM((N_EXPERTS, LANES), F32)]
                        + _weight_scratch(d, 3 * d, d)),
        compiler_params=pltpu.CompilerParams(
            dimension_semantics=("arbitrary", "arbitrary"), vmem_limit_bytes=VMEM_LIMIT),
        name="conv_a_prompt",
    )(x, win, cw, wout, g, b, wrt, _prefix_matrix(ts))


def _conv_a_sample_body(x_ref, s0_ref, s1_ref, win_hbm, cw_ref, wout_hbm, g_ref, b_ref, joint_ref,
                        o_ref, u_ref, win_ref, wout_ref, stage, wsem):
    del joint_ref
    _load_weight_bf16(win_hbm, win_ref, stage, wsem)
    _load_weight_bf16(wout_hbm, wout_ref, stage, wsem)
    x = x_ref[...]
    d = x.shape[1]
    bcx = _dot(x.astype(BF16), win_ref[...])
    gb, gc, xh = bcx[:, :d], bcx[:, d:2 * d], bcx[:, 2 * d:]
    u = gc * xh
    cw = cw_ref[...]
    conv = cw[0:1] * s0_ref[...] + cw[1:2] * s1_ref[...] + cw[2:3] * u
    y = _dot((gb * conv).astype(BF16), wout_ref[...])
    _store_rows(o_ref, _layer_norm(ALPHA * x + y, g_ref[...], b_ref[...]))
    u_ref[...] = u


def _whole(a):
    return pl.BlockSpec(a.shape, lambda i: (0,) * a.ndim)


def _conv_a_sample(x, s0, s1, win, cw, wout, g, b, joint):
    n, d = x.shape
    hbm = pl.BlockSpec(memory_space=pl.ANY)
    first_block = joint.shape[0] // (n * SUBLANES) - 1
    return pl.pallas_call(
        _conv_a_sample_body,
        grid=(1,),
        in_specs=[_whole(x), _whole(s0), _whole(s1), hbm, _whole(cw), hbm, _whole(g), _whole(b),
                  hbm],
        out_specs=[pl.BlockSpec((n * SUBLANES, LANES), lambda i: (first_block, 0)),
                   pl.BlockSpec((n, d), lambda i: (0, 0))],
        out_shape=[jax.ShapeDtypeStruct(joint.shape, F32), jax.ShapeDtypeStruct((n, d), F32)],
        scratch_shapes=_weight_scratch(d, 3 * d, d),
        input_output_aliases={8: 0},
        compiler_params=pltpu.CompilerParams(
            dimension_semantics=("arbitrary",), vmem_limit_bytes=VMEM_LIMIT),
        name="conv_a_sample",
    )(x, s0, s1, win, cw, wout, g, b, joint)


def _rglru_prompt_body(x_ref, xn_ref, win_hbm, cw_ref, cb_ref, wcat_ref, gab_ref, gxb_ref, lam_ref,
                       wout_hbm, g_ref, b_ref, wrt_ref, tri_ref,
                       o_ref, buf_ref, hl_ref, ints_ref, wts_ref, cnt_ref,
                       xcarry, hcarry, counts, gx_scr, win_ref, wout_ref, stage, wsem, *, nj):
    i = pl.program_id(0)
    s = i % nj

    @pl.when(i == 0)
    def _():
        _load_weight_bf16(win_hbm, win_ref, stage, wsem)
        _load_weight_bf16(wout_hbm, wout_ref, stage, wsem)
        gx_scr[0] = _dot(x_ref[0].astype(BF16), win_ref[...])
        counts[...] = jnp.zeros_like(counts)

    @pl.when(s == 0)
    def _():
        xcarry[...] = jnp.zeros_like(xcarry)
        hcarry[...] = jnp.zeros_like(hcarry)

    for par in range(2):
        @pl.when(i % 2 == par)
        def _(par=par):
            x = x_ref[0]
            ts, d = x.shape
            xn = xn_ref[0].astype(BF16)
            blk = d // N_RG_BLOCKS
            zs = []
            for u in range(N_RG_BLOCKS // RG_UNIT):
                c0, c1 = u * RG_UNIT * blk, (u + 1) * RG_UNIT * blk
                for half in (0, d):
                    gx_scr[1 - par, :, half + c0:half + c1] = _dot(
                        xn, win_ref[:, half + c0:half + c1])
                gate, xr = gx_scr[par, :, c0:c1], gx_scr[par, :, d + c0:d + c1]
                prev = xcarry[:, c0:c1]
                cw = cw_ref[:, c0:c1]
                xc = (cw[0:1] * _shift_rows(xr, 3, prev) + cw[1:2] * _shift_rows(xr, 2, prev)
                      + cw[2:3] * _shift_rows(xr, 1, prev) + cw[3:4] * xr) + cb_ref[:, c0:c1]
                a, bt = _rglru_coeffs(xc, wcat_ref, gab_ref[:, c0:c1], gxb_ref[:, c0:c1],
                                      lam_ref[:, c0:c1], u * RG_UNIT)
                hs, hlast = _scan_rows(a, bt, hcarry[0:1, c0:c1])
                zs.append((jax.nn.gelu(gate, approximate=True) * hs).astype(BF16))
                xcarry[:, c0:c1] = xr[ts - SUBLANES:ts]
                hcarry[:, c0:c1] = jnp.broadcast_to(hlast, (SUBLANES, c1 - c0))

            y = _dot(jnp.concatenate(zs, axis=1), wout_ref[...])
            x1 = _layer_norm(ALPHA * x + y, g_ref[...], b_ref[...])
            _store_rows(o_ref, x1)
            _route_tile(x1.astype(BF16), wrt_ref, tri_ref, counts, ints_ref, wts_ref, cnt_ref)

    @pl.when(s == nj - 1)
    def _():
        buf_ref[0] = xcarry[SUBLANES - 3:SUBLANES]
        hl_ref[0] = hcarry[0:1]


def _rglru_prompt(x, win, cw, cb, wcat, gab, gxb, lam, wout, g, b, wrt, n_extra):
    bsz, seq, d = x.shape
    ts = min(TS_B, seq)
    nj = seq // ts
    n = bsz * nj
    tp = bsz * seq
    blk = d // N_RG_BLOCKS
    const2 = lambda i: (0, 0)
    tile = lambda i: (i // nj, i % nj, 0)
    return pl.pallas_call(
        functools.partial(_rglru_prompt_body, nj=nj),
        grid=(n,),
        in_specs=[
            pl.BlockSpec((1, ts, d), tile),
            pl.BlockSpec((1, ts, d), lambda i: tile(jnp.minimum(i + 1, n - 1))),
            pl.BlockSpec(memory_space=pl.ANY),
            pl.BlockSpec((4, d), const2),
            pl.BlockSpec((1, d), const2),
            pl.BlockSpec((N_RG_BLOCKS, blk, 2 * blk), lambda i: (0, 0, 0)),
            pl.BlockSpec((1, d), const2),
            pl.BlockSpec((1, d), const2),
            pl.BlockSpec((1, d), const2),
            pl.BlockSpec(memory_space=pl.ANY),
            pl.BlockSpec((1, d), const2),
            pl.BlockSpec((1, d), const2),
            pl.BlockSpec((ROUTE_ROWS, d), const2),
            pl.BlockSpec((ts, ts), const2),
        ],
        out_specs=[
            pl.BlockSpec((ts * SUBLANES, LANES), lambda i: (i, 0)),
            pl.BlockSpec((1, 3, d), lambda i: (i // nj, 0, 0)),
            pl.BlockSpec((1, 1, d), lambda i: (i // nj, 0, 0)),
            pl.BlockSpec((4, ts), lambda i: (0, i)),
            pl.BlockSpec((2, ts), lambda i: (0, i)),
            pl.BlockSpec((N_EXPERTS, LANES), const2),
        ],
        out_shape=[
            jax.ShapeDtypeStruct(((tp + n_extra) * SUBLANES, LANES), F32),
            jax.ShapeDtypeStruct((bsz, 3, d), F32),
            jax.ShapeDtypeStruct((bsz, 1, d), F32),
            jax.ShapeDtypeStruct((4, tp), I32),
            jax.ShapeDtypeStruct((2, tp), F32),
            jax.ShapeDtypeStruct((N_EXPERTS, LANES), F32),
        ],
        scratch_shapes=([pltpu.VMEM((SUBLANES, d), F32), pltpu.VMEM((SUBLANES, d), F32),
                         pltpu.VMEM((N_EXPERTS, LANES), F32), pltpu.VMEM((2, ts, 2 * d), F32)]
                        + _weight_scratch(d, 2 * d, d)),
        compiler_params=pltpu.CompilerParams(
            dimension_semantics=("arbitrary",), vmem_limit_bytes=VMEM_LIMIT),
        name="rglru_prompt",
    )(x, x, win, cw, cb, wcat, gab, gxb, lam, wout, g, b, wrt, _prefix_matrix(ts))


def _rglru_sample_body(x_ref, s0_ref, s1_ref, s2_ref, h0_ref, win_hbm, cw_ref, cb_ref, wcat_ref,
                       gab_ref, gxb_ref, lam_ref, wout_hbm, g_ref, b_ref, joint_ref,
                       o_ref, xr_ref, h_ref, win_ref, wout_ref, stage, wsem):
    del joint_ref
    _load_weight_bf16(win_hbm, win_ref, stage, wsem)
    _load_weight_bf16(wout_hbm, wout_ref, stage, wsem)
    x = x_ref[...]
    d = x.shape[1]
    gx = _dot(x.astype(BF16), win_ref[...])
    gate, xr = gx[:, :d], gx[:, d:]
    cw = cw_ref[...]
    xc = (cw[0:1] * s0_ref[...] + cw[1:2] * s1_ref[...] + cw[2:3] * s2_ref[...]
          + cw[3:4] * xr) + cb_ref[...]
    a, bt = _rglru_coeffs(xc, wcat_ref, gab_ref[...], gxb_ref[...], lam_ref[...])
    h = a * h0_ref[...] + bt
    y = _dot((jax.nn.gelu(gate, approximate=True) * h).astype(BF16), wout_ref[...])
    _store_rows(o_ref, _layer_norm(ALPHA * x + y, g_ref[...], b_ref[...]))
    xr_ref[...] = xr
    h_ref[...] = h


def _rglru_sample(x, s0, s1, s2, h0, win, cw, cb, wcat, gab, gxb, lam, wout, g, b, joint):
    n, d = x.shape
    hbm = pl.BlockSpec(memory_space=pl.ANY)
    first_block = joint.shape[0] // (n * SUBLANES) - 1
    vec = pl.BlockSpec((n, d), lambda i: (0, 0))
    return pl.pallas_call(
        _rglru_sample_body,
        grid=(1,),
        in_specs=[_whole(x), _whole(s0), _whole(s1), _whole(s2), _whole(h0), hbm, _whole(cw),
                  _whole(cb), _whole(wcat), _whole(gab), _whole(gxb), _whole(lam), hbm,
                  _whole(g), _whole(b), hbm],
        out_specs=[pl.BlockSpec((n * SUBLANES, LANES), lambda i: (first_block, 0)), vec, vec],
        out_shape=[jax.ShapeDtypeStruct(joint.shape, F32),
                   jax.ShapeDtypeStruct((n, d), F32), jax.ShapeDtypeStruct((n, d), F32)],
        scratch_shapes=_weight_scratch(d, 2 * d, d),
        input_output_aliases={15: 0},
        compiler_params=pltpu.CompilerParams(
            dimension_semantics=("arbitrary",), vmem_limit_bytes=VMEM_LIMIT),
        name="rglru_sample",
    )(x, s0, s1, s2, h0, win, cw, cb, wcat, gab, gxb, lam, wout, g, b, joint)


def _first_argmax(v, rows):
    m = jnp.max(v, axis=0, keepdims=True)
    idx = jnp.min(jnp.where(v == m, rows, v.shape[0]), axis=0, keepdims=True)
    return m, idx


def _route_body(x_ref, wrt_ref, tri_ref, cin_ref, ints_ref, wts_ref, cnt_ref, carry):
    @pl.when(pl.program_id(0) == 0)
    def _():
        carry[...] = cin_ref[...]

    tt = x_ref.shape[0] // SUBLANES
    _route_tile(_load_rows(x_ref, tt).astype(BF16), wrt_ref, tri_ref, carry, ints_ref, wts_ref,
                cnt_ref)


def _route_tile(xb, wrt_ref, tri_ref, carry, ints_ref, wts_ref, cnt_ref):
    tt = xb.shape[0]
    lt = lax.dot_general(wrt_ref[...], xb, (((1,), (1,)), ((), ())), preferred_element_type=F32)
    rows8 = lax.broadcasted_iota(I32, (SUBLANES, tt), 0)
    neg_inf = jnp.float32(-jnp.inf)

    gl = jnp.where(rows8 < N_GROUPS, lt[GROUP_ROW0:GROUP_ROW0 + SUBLANES], neg_inf)
    gmax, gidx = _first_argmax(gl, rows8)
    gw = 1.0 / jnp.sum(jnp.exp(gl - gmax), axis=0, keepdims=True)

    el = lt[0:EXP_PER_GROUP]
    for g in range(1, N_GROUPS):
        el = jnp.where(gidx == g, lt[g * EXP_PER_GROUP:(g + 1) * EXP_PER_GROUP], el)
    emax, i1 = _first_argmax(el, rows8)
    el2 = jnp.where(rows8 == i1, neg_inf, el)
    m2, i2 = _first_argmax(el2, rows8)
    psum = jnp.sum(jnp.exp(el - emax), axis=0, keepdims=True)
    ep1 = 1.0 / psum
    ep2 = jnp.exp(m2 - emax) / psum
    tot = ep1 + ep2
    wa = gw * (ep1 / tot)
    wb = gw * (ep2 / tot)
    ea = gidx * EXP_PER_GROUP + i1
    eb = gidx * EXP_PER_GROUP + i2

    rows_e = lax.broadcasted_iota(I32, (N_EXPERTS, tt), 0)
    oha = rows_e == ea
    ohb = rows_e == eb
    oh = jnp.where(oha | ohb, 1.0, 0.0)
    base = carry[...][:, 0:1]
    excl = _dot(oh.astype(BF16), tri_ref[...]) + base
    ra = jnp.sum(jnp.where(oha, excl, 0.0), axis=0, keepdims=True)
    rb = jnp.sum(jnp.where(ohb, excl, 0.0), axis=0, keepdims=True)
    new = carry[...] + jnp.sum(oh, axis=1, keepdims=True)
    carry[...] = new
    cnt_ref[...] = new

    ints_ref[0:1, :] = ea
    ints_ref[1:2, :] = eb
    ints_ref[2:3, :] = ra.astype(I32)
    ints_ref[3:4, :] = rb.astype(I32)
    wts_ref[0:1, :] = wa
    wts_ref[1:2, :] = wb


def _route(x, wrt, cin, tok0, t):
    d = wrt.shape[1]
    tt = min(TT_ROUTE, t)
    blk0 = tok0 // tt
    tri = _prefix_matrix(tt)
    return pl.pallas_call(
        _route_body,
        grid=(t // tt,),
        in_specs=[
            pl.BlockSpec((tt * SUBLANES, LANES), lambda i: (blk0 + i, 0)),
            pl.BlockSpec((ROUTE_ROWS, d), lambda i: (0, 0)),
            pl.BlockSpec((tt, tt), lambda i: (0, 0)),
            pl.BlockSpec((N_EXPERTS, LANES), lambda i: (0, 0)),
        ],
        out_specs=[
            pl.BlockSpec((4, tt), lambda i: (0, i)),
            pl.BlockSpec((2, tt), lambda i: (0, i)),
            pl.BlockSpec((N_EXPERTS, LANES), lambda i: (0, 0)),
        ],
        out_shape=[
            jax.ShapeDtypeStruct((4, t), I32),
            jax.ShapeDtypeStruct((2, t), F32),
            jax.ShapeDtypeStruct((N_EXPERTS, LANES), F32),
        ],
        scratch_shapes=[pltpu.VMEM((N_EXPERTS, LANES), F32)],
        compiler_params=pltpu.CompilerParams(
            dimension_semantics=("arbitrary",), vmem_limit_bytes=VMEM_LIMIT),
        name="route",
    )(x, wrt, tri, cin)


def _dest_body(ints_ref, pst_ref, dest_ref):
    ints = ints_ref[...]
    tt = ints.shape[1]
    rows_e = lax.broadcasted_iota(I32, (N_EXPERTS, tt), 0)
    pst = pst_ref[...][:, 0:1]
    for k in range(2):
        start = jnp.sum(jnp.where(rows_e == ints[k:k + 1], pst, 0.0), axis=0, keepdims=True)
        dest_ref[k:k + 1, :] = start.astype(I32) + ints[2 + k:3 + k]


def _dest(ints, pstart_f):
    t = ints.shape[1]
    return pl.pallas_call(
        _dest_body,
        out_shape=jax.ShapeDtypeStruct((2, t), I32),
        compiler_params=pltpu.CompilerParams(vmem_limit_bytes=VMEM_LIMIT),
        name="dest",
    )(ints, pstart_f)


def _invert_body(dest_ref, padpos_ref, inv_ref, *, t_total):
    def prime_body(i, c):
        inv_ref[i] = SPARE_ROW0 + i
        return c

    lax.fori_loop(0, INV_ROW0, prime_body, 0, unroll=8)

    def pad_body(e, c):
        q0 = padpos_ref[e]
        for r in range(BLK):
            inv_ref[q0 + r] = PAD_BASE + ((q0 + r) & (PAD_SPAN - 1))
        return c

    lax.fori_loop(0, N_EXPERTS, pad_body, 0)

    def tok_body(j, c):
        toks = [j * ROW_UNROLL + u for u in range(ROW_UNROLL)]
        rows = [[dest_ref[k * t_total + t] for k in range(2)] for t in toks]
        for t, qs in zip(toks, rows):
            for k in range(2):
                inv_ref[qs[k]] = t + (k << PLANE_BITS)
        return c

    lax.fori_loop(0, t_total // ROW_UNROLL, tok_body, 0)


def _invert(dest_flat, pad_pos, p_rows):
    t_total = dest_flat.shape[0] // 2
    grid_spec = pltpu.PrefetchScalarGridSpec(
        num_scalar_prefetch=2,
        grid=(1,),
        in_specs=[],
        out_specs=pl.BlockSpec(memory_space=pltpu.SMEM),
    )
    return pl.pallas_call(
        functools.partial(_invert_body, t_total=t_total),
        grid_spec=grid_spec,
        out_shape=jax.ShapeDtypeStruct((INV_ROW0 + p_rows + BLK,), I32),
        compiler_params=pltpu.CompilerParams(dimension_semantics=("arbitrary",)),
        name="invert",
    )(dest_flat, pad_pos)


def _experts_body(be_ref, nu_ref, nblk_ref, inv_ref, x_hbm, wg_hbm, wu_hbm, wd_hbm, y_hbm,
                  xbuf, obuf, sg, su, sd, wg_ref, wu_ref, wd_ref, wsem, gsem, ssem, slot_ref,
                  *, layer):
    b = pl.program_id(0)
    nu = nu_ref[0]
    e = be_ref[jnp.minimum(b, nu - 1)]

    def gather(blk, slot, fn):
        vs = [inv_ref[(blk + PRIME_BLOCKS) * BLK + r] for r in range(BLK)]
        for r, v in enumerate(vs):
            fn(pltpu.make_async_copy(_tile_of_row(x_hbm, v & TOKEN_MASK),
                                     _tile_of_row(xbuf.at[slot], r), gsem.at[slot]), 0)

    def scatter(blk, slot, fn):
        vs = [inv_ref[(blk + PRIME_BLOCKS) * BLK + r] for r in range(BLK)]
        for r, v in enumerate(vs):
            fn(pltpu.make_async_copy(_tile_of_row(obuf.at[slot], r),
                                     _tile_of_row(y_hbm, v), ssem.at[slot]), 1)

    def start(cp, priority):
        cp.start(priority=priority)

    def wait(cp, priority):
        cp.wait()

    def fetch(ex, slot):
        return (pltpu.make_async_copy(wg_hbm.at[layer, ex], sg.at[slot], wsem.at[slot, 0]),
                pltpu.make_async_copy(wu_hbm.at[layer, ex], su.at[slot], wsem.at[slot, 1]),
                pltpu.make_async_copy(wd_hbm.at[layer, ex], sd.at[slot], wsem.at[slot, 2]))

    @pl.when(b == 0)
    def _():
        slot_ref[0] = 0
        for cp in fetch(e, 0):
            cp.start()
        obuf[...] = jnp.zeros_like(obuf)
        scatter(-3, 0, start)
        scatter(-2, 1, start)
        gather(0, 0, start)
        gather(jnp.minimum(1, nu - 1), 1, start)

    first_of_expert = jnp.logical_or(b == 0, e != be_ref[jnp.maximum(b - 1, 0)])

    @pl.when(jnp.logical_and(b < nu, first_of_expert))
    def _():
        slot = slot_ref[0]
        nxt = b + nblk_ref[e]

        @pl.when(nxt < nu)
        def _():
            for cp in fetch(be_ref[nxt], 1 - slot):
                cp.start()

        for cp in fetch(e, slot):
            cp.wait()
        wg_ref[...] = sg[slot].astype(BF16)
        wu_ref[...] = su[slot].astype(BF16)
        wd_ref[...] = sd[slot].astype(BF16)
        slot_ref[0] = 1 - slot

    for slot in range(N_SLOTS):
        prev, nxt = (slot - 1) % N_SLOTS, (slot + 1) % N_SLOTS

        @pl.when(jnp.logical_and(b < nu, b % N_SLOTS == slot))
        def _(slot=slot, prev=prev):
            gather(b, slot, wait)
            scatter(b - 3, slot, wait)
            xb = _load_rows(xbuf, BLK, (slot,)).astype(BF16)
            scatter(b - 1, prev, start)
            gather(jnp.minimum(b + 2, nu - 1), prev, start)
            h = jax.nn.silu(_dot(xb, wg_ref[...])) * _dot(xb, wu_ref[...])
            o = _dot(h.astype(BF16), wd_ref[...])
            _store_rows(obuf, o, (slot,))

        @pl.when(jnp.logical_and(b == nu, b % N_SLOTS == slot))
        def _(slot=slot, prev=prev, nxt=nxt):
            scatter(b - 1, prev, start)
            scatter(b - 3, slot, wait)
            scatter(b - 2, nxt, wait)
            scatter(b - 1, prev, wait)
            gather(nu - 1, slot, wait)
            gather(nu - 1, nxt, wait)


def _experts(blk_e, n_used, nblk, inv, x, wg, wu, wd, layer):
    d, de = wg.shape[2], wg.shape[3]
    nb = blk_e.shape[0]
    hbm = pl.BlockSpec(memory_space=pl.ANY)
    grid_spec = pltpu.PrefetchScalarGridSpec(
        num_scalar_prefetch=4,
        grid=(nb + 1,),
        in_specs=[hbm, hbm, hbm, hbm],
        out_specs=hbm,
        scratch_shapes=[
            pltpu.VMEM((N_SLOTS, BLK * SUBLANES, LANES), F32),
            pltpu.VMEM((N_SLOTS, BLK * SUBLANES, LANES), F32),
            pltpu.VMEM((2, d, de), F32), pltpu.VMEM((2, d, de), F32), pltpu.VMEM((2, de, d), F32),
            pltpu.VMEM((d, de), BF16), pltpu.VMEM((d, de), BF16), pltpu.VMEM((de, d), BF16),
            pltpu.SemaphoreType.DMA((2, 3)), pltpu.SemaphoreType.DMA((N_SLOTS,)),
            pltpu.SemaphoreType.DMA((N_SLOTS,)), pltpu.SMEM((1,), I32),
        ],
    )
    return pl.pallas_call(
        functools.partial(_experts_body, layer=layer),
        grid_spec=grid_spec,
        out_shape=jax.ShapeDtypeStruct((Y_ROWS * SUBLANES, LANES), F32),
        compiler_params=pltpu.CompilerParams(
            dimension_semantics=("arbitrary",), vmem_limit_bytes=VMEM_LIMIT),
        name="experts",
    )(blk_e, n_used, nblk, inv, x, wg, wu, wd)


def _combine_body(x_ref, ya_ref, yb_ref, w_ref, g_ref, b_ref, o_ref):
    tc = o_ref.shape[0]
    w = w_ref[...]
    y = w[:, 0:1] * _load_rows(ya_ref, tc) + w[:, 1:2] * _load_rows(yb_ref, tc)
    o_ref[...] = _layer_norm(ALPHA * _load_rows(x_ref, tc) + y, g_ref[...], b_ref[...])


def _combine(x, y, w_cols, g, b, tok0, t):
    d = g.shape[1]
    tc = min(TC, t)
    blk0 = tok0 // tc
    plane = (1 << PLANE_BITS) // tc
    row_tiled = lambda first: pl.BlockSpec((tc * SUBLANES, LANES), lambda i: (first + i, 0))
    return pl.pallas_call(
        _combine_body,
        grid=(t // tc,),
        in_specs=[
            row_tiled(blk0), row_tiled(blk0), row_tiled(plane + blk0),
            pl.BlockSpec((tc, 2), lambda i: (blk0 + i, 0)),
            pl.BlockSpec((1, d), lambda i: (0, 0)),
            pl.BlockSpec((1, d), lambda i: (0, 0)),
        ],
        out_specs=pl.BlockSpec((tc, d), lambda i: (i, 0)),
        out_shape=jax.ShapeDtypeStruct((t, d), F32),
        compiler_params=pltpu.CompilerParams(
            dimension_semantics=("arbitrary",), vmem_limit_bytes=VMEM_LIMIT),
        name="combine",
    )(x, y, y, w_cols, g, b)


def _router_weight(w_group, w_expert):
    d = w_group.shape[0]
    wrt = jnp.zeros((ROUTE_ROWS, d), F32)
    wrt = wrt.at[0:N_EXPERTS].set(w_expert.T).at[GROUP_ROW0:GROUP_ROW0 + N_GROUPS].set(w_group.T)
    return wrt.astype(BF16)


def _moe_layer(x, tp, ts, routed_p, wrt, wg, wu, wd, layer, g, b):
    t_total = tp + ts
    assert PAD_SPAN <= t_total <= 1 << PLANE_BITS and tp % ts == 0
    ints_p, wts_p, cnt_p = routed_p
    ints_s, wts_s, cnt = _route(x, wrt, cnt_p, tp, ts)
    ints = jnp.concatenate([ints_p, ints_s], axis=1)
    wts = jnp.concatenate([wts_p, wts_s], axis=1)

    counts = cnt[:, 0].astype(I32)
    pcounts = (counts + BLK - 1) // BLK * BLK
    pend = jnp.cumsum(pcounts)
    pstart = pend - pcounts
    nb = (2 * t_total + N_EXPERTS * (BLK - 1) + BLK - 1) // BLK
    p_rows = nb * BLK
    n_used = (pend[-1] // BLK).astype(I32).reshape(1)
    blk_first = jnp.minimum(jnp.arange(nb, dtype=I32), n_used[0] - 1) * BLK
    blk_e = jnp.sum((pend[None, :] <= blk_first[:, None]).astype(I32), axis=1)
    nblk = pcounts // BLK

    pstart_f = jnp.broadcast_to((pstart + INV_ROW0).astype(F32)[:, None], (N_EXPERTS, LANES))
    dest_flat = _dest(ints, pstart_f).reshape(2 * t_total)

    inv = _invert(dest_flat, pstart + counts + INV_ROW0, p_rows)
    y = _experts(blk_e, n_used, nblk, inv, x, wg, wu, wd, layer)
    w_cols = wts.T
    return (_combine(x, y, w_cols, g, b, 0, tp), _combine(x, y, w_cols, g, b, tp, ts))


def kernel(x_prompt, x_sample, state_conv_a, state_conv_b, state_h, a_w_in, a_conv_w, a_w_out,
           b_w_in, b_conv_w, b_conv_b, b_gate_a_w, b_gate_a_b, b_gate_x_w, b_gate_x_b, b_lambda,
           b_w_out, ln1_g, ln1_b, ln2_g, ln2_b, moe_w_group, moe_w_expert, moe_w_gate, moe_w_up,
           moe_w_down):
    bsz, seq, d = x_prompt.shape
    n_s = x_sample.shape[0]
    row = lambda v: v.reshape(1, d)

    tp = bsz * seq
    win, wout = a_w_in[0], a_w_out[0]
    wrt = _router_weight(moe_w_group[0], moe_w_expert[0])
    x1, conv_a_p, *routed = _conv_a_prompt(x_prompt, win, a_conv_w[0], wout, row(ln1_g[0]),
                                           row(ln1_b[0]), wrt, n_s)
    sa = state_conv_a[0]
    x1, u_s = _conv_a_sample(x_sample.reshape(n_s, d), sa[:, 0], sa[:, 1], win, a_conv_w[0], wout,
                             row(ln1_g[0]), row(ln1_b[0]), x1)
    conv_a_s = jnp.stack([sa[:, 1], u_s], axis=1)

    xp, xs = _moe_layer(x1, tp, n_s, routed, wrt,
                        moe_w_gate, moe_w_up, moe_w_down, 0, row(ln2_g[0]), row(ln2_b[0]))

    win, wout = b_w_in[0], b_w_out[0]
    wcat = jnp.concatenate([b_gate_a_w[0], b_gate_x_w[0]], axis=-1).astype(BF16)
    args = (win, b_conv_w[0], row(b_conv_b[0]), wcat, row(b_gate_a_b[0]), row(b_gate_x_b[0]),
            row(b_lambda[0]), wout, row(ln1_g[1]), row(ln1_b[1]))
    wrt = _router_weight(moe_w_group[1], moe_w_expert[1])
    x1, conv_b_p, h_p, *routed = _rglru_prompt(xp.reshape(bsz, seq, d), *args, wrt, n_s)
    sb = state_conv_b[0]
    x1, xr_s, h_s = _rglru_sample(xs, sb[:, 0], sb[:, 1], sb[:, 2], state_h[0], *args, x1)
    conv_b_s = jnp.stack([sb[:, 1], sb[:, 2], xr_s], axis=1)

    xp, xs = _moe_layer(x1, tp, n_s, routed, wrt,
                        moe_w_gate, moe_w_up, moe_w_down, 1, row(ln2_g[1]), row(ln2_b[1]))

    return (xp.reshape(bsz, seq, d), xs.reshape(n_s, 1, d),
            conv_a_p[None], conv_a_s[None], conv_b_p[None], conv_b_s[None],
            h_p.reshape(1, bsz, d), h_s[None])
```

```python
import functools

import jax
import jax.numpy as jnp
from jax import lax
from jax.experimental import pallas as pl
from jax.experimental.pallas import tpu as pltpu

F32 = jnp.float32
BF16 = jnp.bfloat16
I32 = jnp.int32

DEPTH = 2
N_RG_BLOCKS = 8
RG_C = 8.0
N_GROUPS = 4
EXP_PER_GROUP = 8
N_EXPERTS = N_GROUPS * EXP_PER_GROUP
ALPHA = (2.0 * DEPTH) ** 0.25
LN_EPS = 1e-5

LANES = 128
SUBLANES = 8
VMEM_LIMIT = 56 * 1024 * 1024

TS_A = 512
TS_B = 512
TT_ROUTE = 1024
TC = 1024
BLK = 256
ROUTE_ROWS = 128
GROUP_ROW0 = N_EXPERTS
W_CHUNK = 512
RG_UNIT = 2
ROW_UNROLL = 8

PLANE_BITS = 15
TOKEN_MASK = (1 << PLANE_BITS) - 1
PAD_BASE = 2 << PLANE_BITS
N_SLOTS = 3
PRIME_BLOCKS = N_SLOTS
INV_ROW0 = PRIME_BLOCKS * BLK
PAD_SPAN = 4 * BLK
SPARE_ROW0 = PAD_BASE + PAD_SPAN
Y_ROWS = SPARE_ROW0 + PRIME_BLOCKS * BLK


def _dot(a, b):
    return jnp.dot(a, b, preferred_element_type=F32)


def _load_rows(ref, m, idx=()):
    return jnp.concatenate(
        [ref[idx + (pl.ds(s, m, stride=SUBLANES), slice(None))] for s in range(SUBLANES)], axis=1)


def _store_rows(ref, v, idx=()):
    m = v.shape[0]
    for s in range(SUBLANES):
        ref[idx + (pl.ds(s, m, stride=SUBLANES), slice(None))] = v[:, s * LANES:(s + 1) * LANES]


def _tile_of_row(ref, r):
    return ref.at[pl.ds(pl.multiple_of(r * SUBLANES, SUBLANES), SUBLANES)]


def _load_weight_bf16(w_hbm, w_bf, stage, sem):
    nch = w_hbm.shape[1] // W_CHUNK

    def chunk_copy(c):
        return pltpu.make_async_copy(w_hbm.at[:, pl.ds(c * W_CHUNK, W_CHUNK)],
                                     stage.at[c % 2], sem.at[c % 2])

    chunk_copy(0).start()
    for c in range(nch):
        if c + 1 < nch:
            chunk_copy(c + 1).start()
        chunk_copy(c).wait()
        w_bf[:, c * W_CHUNK:(c + 1) * W_CHUNK] = stage[c % 2].astype(BF16)


def _weight_scratch(k, *ns):
    return ([pltpu.VMEM((k, n), BF16) for n in ns]
            + [pltpu.VMEM((2, k, W_CHUNK), F32), pltpu.SemaphoreType.DMA((2,))])


def _layer_norm(r, g, b):
    mu = jnp.mean(r, axis=-1, keepdims=True)
    d = r - mu
    var = jnp.mean(d * d, axis=-1, keepdims=True)
    return d * lax.rsqrt(var + LN_EPS) * g + b


def _shift_rows(v, k, prev8):
    rolled = pltpu.roll(v, k, axis=0)
    rows8 = lax.broadcasted_iota(I32, (SUBLANES, v.shape[1]), 0)
    first = jnp.where(rows8 < k, pltpu.roll(prev8, k, axis=0), rolled[0:SUBLANES])
    return jnp.concatenate([first, rolled[SUBLANES:]], axis=0)


def _softplus(v):
    return jnp.maximum(v, 0.0) + jnp.log1p(jnp.exp(-jnp.abs(v)))


def _sigmoid(z):
    return 0.5 * jnp.tanh(0.5 * z) + 0.5


def _rglru_coeffs(xc, wcat_ref, gab, gxb, lam, blk0=0):
    blk = wcat_ref.shape[1]
    xcb = xc.astype(BF16)
    rs, is_ = [], []
    for n in range(xc.shape[1] // blk):
        o = _dot(xcb[:, n * blk:(n + 1) * blk], wcat_ref[blk0 + n])
        rs.append(o[:, :blk])
        is_.append(o[:, blk:])
    r = _sigmoid(jnp.concatenate(rs, axis=1) + gab)
    i = _sigmoid(jnp.concatenate(is_, axis=1) + gxb)
    neg_log_a = RG_C * r * _softplus(-lam)
    a = jnp.exp(-neg_log_a)
    v = jnp.tanh(neg_log_a) * (a * a + 1.0)
    mult = jnp.where(v > 0.0, v * lax.rsqrt(v), 0.0)
    return a, mult * (i * xc)


def _scan_rows(a, b, h0):
    m, d = a.shape
    groups = m // SUBLANES
    a = a.reshape(groups, SUBLANES, d)
    b = b.reshape(groups, SUBLANES, d)
    sub = lax.broadcasted_iota(I32, a.shape, 1)
    for k in (1, 2, 4):
        keep = sub >= k
        a_sh = jnp.where(keep, pltpu.roll(a, k, axis=1), 1.0)
        b_sh = jnp.where(keep, pltpu.roll(b, k, axis=1), 0.0)
        b = a * b_sh + b
        a = a * a_sh
    outs = []
    h = h0
    for g in range(groups):
        hg = a[g] * h + b[g]
        outs.append(hg)
        h = hg[SUBLANES - 1:SUBLANES]
    return jnp.concatenate(outs, axis=0), h


def _conv_a_prompt_body(x_ref, win_hbm, cw_ref, wout_hbm, g_ref, b_ref, wrt_ref, tri_ref,
                        o_ref, buf_ref, ints_ref, wts_ref, cnt_ref,
                        carry, counts, win_ref, wout_ref, stage, wsem):
    s = pl.program_id(1)

    @pl.when(jnp.logical_and(pl.program_id(0) == 0, s == 0))
    def _():
        _load_weight_bf16(win_hbm, win_ref, stage, wsem)
        _load_weight_bf16(wout_hbm, wout_ref, stage, wsem)
        counts[...] = jnp.zeros_like(counts)

    @pl.when(s == 0)
    def _():
        carry[...] = jnp.zeros_like(carry)

    x = x_ref[0]
    d = x.shape[1]
    bcx = _dot(x.astype(BF16), win_ref[...])
    gb, gc, xh = bcx[:, :d], bcx[:, d:2 * d], bcx[:, 2 * d:]
    u = gc * xh
    prev = carry[...]
    cw = cw_ref[...]
    conv = (cw[0:1] * _shift_rows(u, 2, prev) + cw[1:2] * _shift_rows(u, 1, prev)
            + cw[2:3] * u)
    y = _dot((gb * conv).astype(BF16), wout_ref[...])
    x1 = _layer_norm(ALPHA * x + y, g_ref[...], b_ref[...])
    _store_rows(o_ref, x1)
    _route_tile(x1.astype(BF16), wrt_ref, tri_ref, counts, ints_ref, wts_ref, cnt_ref)
    ts = u.shape[0]
    carry[...] = u[ts - SUBLANES:ts]

    @pl.when(s == pl.num_programs(1) - 1)
    def _():
        buf_ref[0] = u[ts - 2:ts]


def _prefix_matrix(tt):
    return (jnp.arange(tt)[:, None] < jnp.arange(tt)[None, :]).astype(BF16)


def _conv_a_prompt(x, win, cw, wout, g, b, wrt, n_extra):
    bsz, seq, d = x.shape
    ts = min(TS_A, seq)
    grid = (bsz, seq // ts)
    nj = seq // ts
    tp = bsz * seq
    const2 = lambda i, j: (0, 0)
    return pl.pallas_call(
        _conv_a_prompt_body,
        grid=grid,
        in_specs=[
            pl.BlockSpec((1, ts, d), lambda i, j: (i, j, 0)),
            pl.BlockSpec(memory_space=pl.ANY),
            pl.BlockSpec((3, d), const2),
            pl.BlockSpec(memory_space=pl.ANY),
            pl.BlockSpec((1, d), const2),
            pl.BlockSpec((1, d), const2),
            pl.BlockSpec((ROUTE_ROWS, d), const2),
            pl.BlockSpec((ts, ts), const2),
        ],
        out_specs=[
            pl.BlockSpec((ts * SUBLANES, LANES), lambda i, j: (i * nj + j, 0)),
            pl.BlockSpec((1, 2, d), lambda i, j: (i, 0, 0)),
            pl.BlockSpec((4, ts), lambda i, j: (0, i * nj + j)),
            pl.BlockSpec((2, ts), lambda i, j: (0, i * nj + j)),
            pl.BlockSpec((N_EXPERTS, LANES), const2),
        ],
        out_shape=[
            jax.ShapeDtypeStruct(((tp + n_extra) * SUBLANES, LANES), F32),
            jax.ShapeDtypeStruct((bsz, 2, d), F32),
            jax.ShapeDtypeStruct((4, tp), I32),
            jax.ShapeDtypeStruct((2, tp), F32),
            jax.ShapeDtypeStruct((N_EXPERTS, LANES), F32),
        ],
        scratch_shapes=([pltpu.VMEM((SUBLANES, d), F32), pltpu.VMEM((N_EXPERTS, LANES), F32)]
                        + _weight_scratch(d, 3 * d, d)),
        compiler_params=pltpu.CompilerParams(
            dimension_semantics=("arbitrary", "arbitrary"), vmem_limit_bytes=VMEM_LIMIT),
        name="conv_a_prompt",
    )(x, win, cw, wout, g, b, wrt, _prefix_matrix(ts))


def _conv_a_sample_body(x_ref, s0_ref, s1_ref, win_hbm, cw_ref, wout_hbm, g_ref, b_ref, joint_ref,
                        o_ref, u_ref, win_ref, wout_ref, stage, wsem):
    del joint_ref
    _load_weight_bf16(win_hbm, win_ref, stage, wsem)
    _load_weight_bf16(wout_hbm, wout_ref, stage, wsem)
    x = x_ref[...]
    d = x.shape[1]
    bcx = _dot(x.astype(BF16), win_ref[...])
    gb, gc, xh = bcx[:, :d], bcx[:, d:2 * d], bcx[:, 2 * d:]
    u = gc * xh
    cw = cw_ref[...]
    conv = cw[0:1] * s0_ref[...] + cw[1:2] * s1_ref[...] + cw[2:3] * u
    y = _dot((gb * conv).astype(BF16), wout_ref[...])
    _store_rows(o_ref, _layer_norm(ALPHA * x + y, g_ref[...], b_ref[...]))
    u_ref[...] = u


def _whole(a):
    return pl.BlockSpec(a.shape, lambda i: (0,) * a.ndim)


def _conv_a_sample(x, s0, s1, win, cw, wout, g, b, joint):
    n, d = x.shape
    hbm = pl.BlockSpec(memory_space=pl.ANY)
    first_block = joint.shape[0] // (n * SUBLANES) - 1
    return pl.pallas_call(
        _conv_a_sample_body,
        grid=(1,),
        in_specs=[_whole(x), _whole(s0), _whole(s1), hbm, _whole(cw), hbm, _whole(g), _whole(b),
                  hbm],
        out_specs=[pl.BlockSpec((n * SUBLANES, LANES), lambda i: (first_block, 0)),
                   pl.BlockSpec((n, d), lambda i: (0, 0))],
        out_shape=[jax.ShapeDtypeStruct(joint.shape, F32), jax.ShapeDtypeStruct((n, d), F32)],
        scratch_shapes=_weight_scratch(d, 3 * d, d),
        input_output_aliases={8: 0},
        compiler_params=pltpu.CompilerParams(
            dimension_semantics=("arbitrary",), vmem_limit_bytes=VMEM_LIMIT),
        name="conv_a_sample",
    )(x, s0, s1, win, cw, wout, g, b, joint)


def _rglru_prompt_body(x1_ref, ya_ref, yb_ref, w_ref, g2_ref, b2_ref, win_hbm, cw_ref, cb_ref,
                       wcat_ref, gab_ref, gxb_ref, lam_ref, wout_hbm, g_ref, b_ref, wrt_ref, tri_ref,
                       o_ref, buf_ref, hl_ref, ints_ref, wts_ref, cnt_ref,
                       xcarry, hcarry, counts, x_scr, gx_scr, win_ref, wout_ref, stage, wsem,
                       *, nj):
    i = pl.program_id(0)
    t = i - 1
    s = t % nj

    def layer_input():
        ts = x_scr.shape[1]
        w = w_ref[...]
        y = w[:, 0:1] * _load_rows(ya_ref, ts) + w[:, 1:2] * _load_rows(yb_ref, ts)
        return _layer_norm(ALPHA * _load_rows(x1_ref, ts) + y, g2_ref[...], b2_ref[...])

    @pl.when(i == 0)
    def _():
        _load_weight_bf16(win_hbm, win_ref, stage, wsem)
        _load_weight_bf16(wout_hbm, wout_ref, stage, wsem)
        x0 = layer_input()
        x_scr[0] = x0
        gx_scr[0] = _dot(x0.astype(BF16), win_ref[...])
        counts[...] = jnp.zeros_like(counts)

    @pl.when(jnp.logical_and(i > 0, s == 0))
    def _():
        xcarry[...] = jnp.zeros_like(xcarry)
        hcarry[...] = jnp.zeros_like(hcarry)

    for par in range(2):
        @pl.when(jnp.logical_and(i > 0, t % 2 == par))
        def _(par=par):
            x_next = layer_input()
            x_scr[1 - par] = x_next
            xn = x_next.astype(BF16)
            x = x_scr[par]
            ts, d = x.shape
            blk = d // N_RG_BLOCKS
            zs = []
            for u in range(N_RG_BLOCKS // RG_UNIT):
                c0, c1 = u * RG_UNIT * blk, (u + 1) * RG_UNIT * blk
                for half in (0, d):
                    gx_scr[1 - par, :, half + c0:half + c1] = _dot(
                        xn, win_ref[:, half + c0:half + c1])
                gate, xr = gx_scr[par, :, c0:c1], gx_scr[par, :, d + c0:d + c1]
                prev = xcarry[:, c0:c1]
                cw = cw_ref[:, c0:c1]
                xc = (cw[0:1] * _shift_rows(xr, 3, prev) + cw[1:2] * _shift_rows(xr, 2, prev)
                      + cw[2:3] * _shift_rows(xr, 1, prev) + cw[3:4] * xr) + cb_ref[:, c0:c1]
                a, bt = _rglru_coeffs(xc, wcat_ref, gab_ref[:, c0:c1], gxb_ref[:, c0:c1],
                                      lam_ref[:, c0:c1], u * RG_UNIT)
                hs, hlast = _scan_rows(a, bt, hcarry[0:1, c0:c1])
                zs.append((jax.nn.gelu(gate, approximate=True) * hs).astype(BF16))
                xcarry[:, c0:c1] = xr[ts - SUBLANES:ts]
                hcarry[:, c0:c1] = jnp.broadcast_to(hlast, (SUBLANES, c1 - c0))

            y = _dot(jnp.concatenate(zs, axis=1), wout_ref[...])
            x1 = _layer_norm(ALPHA * x + y, g_ref[...], b_ref[...])
            _store_rows(o_ref, x1)
            _route_tile(x1.astype(BF16), wrt_ref, tri_ref, counts, ints_ref, wts_ref, cnt_ref)

    @pl.when(jnp.logical_and(i > 0, s == nj - 1))
    def _():
        buf_ref[0] = xcarry[SUBLANES - 3:SUBLANES]
        hl_ref[0] = hcarry[0:1]


def _rglru_prompt(x_prev, y, w_cols, g2, b2, bsz, seq,
                  win, cw, cb, wcat, gab, gxb, lam, wout, g, b, wrt, n_extra):
    d = g.shape[1]
    ts = min(TS_B, seq)
    nj = seq // ts
    n = bsz * nj
    tp = bsz * seq
    blk = d // N_RG_BLOCKS
    plane = (1 << PLANE_BITS) // ts
    const2 = lambda i: (0, 0)
    nxt = lambda i: jnp.minimum(i, n - 1)
    cur = lambda i: jnp.maximum(i - 1, 0)
    return pl.pallas_call(
        functools.partial(_rglru_prompt_body, nj=nj),
        grid=(n + 1,),
        in_specs=[
            pl.BlockSpec((ts * SUBLANES, LANES), lambda i: (nxt(i), 0)),
            pl.BlockSpec((ts * SUBLANES, LANES), lambda i: (nxt(i), 0)),
            pl.BlockSpec((ts * SUBLANES, LANES), lambda i: (plane + nxt(i), 0)),
            pl.BlockSpec((ts, 2), lambda i: (nxt(i), 0)),
            pl.BlockSpec((1, d), const2),
            pl.BlockSpec((1, d), const2),
            pl.BlockSpec(memory_space=pl.ANY),
            pl.BlockSpec((4, d), const2),
            pl.BlockSpec((1, d), const2),
            pl.BlockSpec((N_RG_BLOCKS, blk, 2 * blk), lambda i: (0, 0, 0)),
            pl.BlockSpec((1, d), const2),
            pl.BlockSpec((1, d), const2),
            pl.BlockSpec((1, d), const2),
            pl.BlockSpec(memory_space=pl.ANY),
            pl.BlockSpec((1, d), const2),
            pl.BlockSpec((1, d), const2),
            pl.BlockSpec((ROUTE_ROWS, d), const2),
            pl.BlockSpec((ts, ts), const2),
        ],
        out_specs=[
            pl.BlockSpec((ts * SUBLANES, LANES), lambda i: (cur(i), 0)),
            pl.BlockSpec((1, 3, d), lambda i: (cur(i) // nj, 0, 0)),
            pl.BlockSpec((1, 1, d), lambda i: (cur(i) // nj, 0, 0)),
            pl.BlockSpec((4, ts), lambda i: (0, cur(i))),
            pl.BlockSpec((2, ts), lambda i: (0, cur(i))),
            pl.BlockSpec((N_EXPERTS, LANES), const2),
        ],
        out_shape=[
            jax.ShapeDtypeStruct(((tp + n_extra) * SUBLANES, LANES), F32),
            jax.ShapeDtypeStruct((bsz, 3, d), F32),
            jax.ShapeDtypeStruct((bsz, 1, d), F32),
            jax.ShapeDtypeStruct((4, tp), I32),
            jax.ShapeDtypeStruct((2, tp), F32),
            jax.ShapeDtypeStruct((N_EXPERTS, LANES), F32),
        ],
        scratch_shapes=([pltpu.VMEM((SUBLANES, d), F32), pltpu.VMEM((SUBLANES, d), F32),
                         pltpu.VMEM((N_EXPERTS, LANES), F32), pltpu.VMEM((2, ts, d), F32),
                         pltpu.VMEM((2, ts, 2 * d), F32)]
                        + _weight_scratch(d, 2 * d, d)),
        compiler_params=pltpu.CompilerParams(
            dimension_semantics=("arbitrary",), vmem_limit_bytes=VMEM_LIMIT),
        name="rglru_prompt",
    )(x_prev, y, y, w_cols, g2, b2, win, cw, cb, wcat, gab, gxb, lam, wout, g, b, wrt,
      _prefix_matrix(ts))


def _rglru_sample_body(x_ref, s0_ref, s1_ref, s2_ref, h0_ref, win_hbm, cw_ref, cb_ref, wcat_ref,
                       gab_ref, gxb_ref, lam_ref, wout_hbm, g_ref, b_ref, joint_ref,
                       o_ref, xr_ref, h_ref, win_ref, wout_ref, stage, wsem):
    del joint_ref
    _load_weight_bf16(win_hbm, win_ref, stage, wsem)
    _load_weight_bf16(wout_hbm, wout_ref, stage, wsem)
    x = x_ref[...]
    d = x.shape[1]
    gx = _dot(x.astype(BF16), win_ref[...])
    gate, xr = gx[:, :d], gx[:, d:]
    cw = cw_ref[...]
    xc = (cw[0:1] * s0_ref[...] + cw[1:2] * s1_ref[...] + cw[2:3] * s2_ref[...]
          + cw[3:4] * xr) + cb_ref[...]
    a, bt = _rglru_coeffs(xc, wcat_ref, gab_ref[...], gxb_ref[...], lam_ref[...])
    h = a * h0_ref[...] + bt
    y = _dot((jax.nn.gelu(gate, approximate=True) * h).astype(BF16), wout_ref[...])
    _store_rows(o_ref, _layer_norm(ALPHA * x + y, g_ref[...], b_ref[...]))
    xr_ref[...] = xr
    h_ref[...] = h


def _rglru_sample(x, s0, s1, s2, h0, win, cw, cb, wcat, gab, gxb, lam, wout, g, b, joint):
    n, d = x.shape
    hbm = pl.BlockSpec(memory_space=pl.ANY)
    first_block = joint.shape[0] // (n * SUBLANES) - 1
    vec = pl.BlockSpec((n, d), lambda i: (0, 0))
    return pl.pallas_call(
        _rglru_sample_body,
        grid=(1,),
        in_specs=[_whole(x), _whole(s0), _whole(s1), _whole(s2), _whole(h0), hbm, _whole(cw),
                  _whole(cb), _whole(wcat), _whole(gab), _whole(gxb), _whole(lam), hbm,
                  _whole(g), _whole(b), hbm],
        out_specs=[pl.BlockSpec((n * SUBLANES, LANES), lambda i: (first_block, 0)), vec, vec],
        out_shape=[jax.ShapeDtypeStruct(joint.shape, F32),
                   jax.ShapeDtypeStruct((n, d), F32), jax.ShapeDtypeStruct((n, d), F32)],
        scratch_shapes=_weight_scratch(d, 2 * d, d),
        input_output_aliases={15: 0},
        compiler_params=pltpu.CompilerParams(
            dimension_semantics=("arbitrary",), vmem_limit_bytes=VMEM_LIMIT),
        name="rglru_sample",
    )(x, s0, s1, s2, h0, win, cw, cb, wcat, gab, gxb, lam, wout, g, b, joint)


def _first_argmax(v, rows):
    m = jnp.max(v, axis=0, keepdims=True)
    idx = jnp.min(jnp.where(v == m, rows, v.shape[0]), axis=0, keepdims=True)
    return m, idx


def _route_body(x_ref, wrt_ref, tri_ref, cin_ref, ints_ref, wts_ref, cnt_ref, carry):
    @pl.when(pl.program_id(0) == 0)
    def _():
        carry[...] = cin_ref[...]

    tt = x_ref.shape[0] // SUBLANES
    _route_tile(_load_rows(x_ref, tt).astype(BF16), wrt_ref, tri_ref, carry, ints_ref, wts_ref,
                cnt_ref)


def _route_tile(xb, wrt_ref, tri_ref, carry, ints_ref, wts_ref, cnt_ref):
    tt = xb.shape[0]
    lt = lax.dot_general(wrt_ref[...], xb, (((1,), (1,)), ((), ())), preferred_element_type=F32)
    rows8 = lax.broadcasted_iota(I32, (SUBLANES, tt), 0)
    neg_inf = jnp.float32(-jnp.inf)

    gl = jnp.where(rows8 < N_GROUPS, lt[GROUP_ROW0:GROUP_ROW0 + SUBLANES], neg_inf)
    gmax, gidx = _first_argmax(gl, rows8)
    gw = 1.0 / jnp.sum(jnp.exp(gl - gmax), axis=0, keepdims=True)

    el = lt[0:EXP_PER_GROUP]
    for g in range(1, N_GROUPS):
        el = jnp.where(gidx == g, lt[g * EXP_PER_GROUP:(g + 1) * EXP_PER_GROUP], el)
    emax, i1 = _first_argmax(el, rows8)
    el2 = jnp.where(rows8 == i1, neg_inf, el)
    m2, i2 = _first_argmax(el2, rows8)
    psum = jnp.sum(jnp.exp(el - emax), axis=0, keepdims=True)
    ep1 = 1.0 / psum
    ep2 = jnp.exp(m2 - emax) / psum
    tot = ep1 + ep2
    wa = gw * (ep1 / tot)
    wb = gw * (ep2 / tot)
    ea = gidx * EXP_PER_GROUP + i1
    eb = gidx * EXP_PER_GROUP + i2

    rows_e = lax.broadcasted_iota(I32, (N_EXPERTS, tt), 0)
    oha = rows_e == ea
    ohb = rows_e == eb
    oh = jnp.where(oha | ohb, 1.0, 0.0)
    base = carry[...][:, 0:1]
    excl = _dot(oh.astype(BF16), tri_ref[...]) + base
    ra = jnp.sum(jnp.where(oha, excl, 0.0), axis=0, keepdims=True)
    rb = jnp.sum(jnp.where(ohb, excl, 0.0), axis=0, keepdims=True)
    new = carry[...] + jnp.sum(oh, axis=1, keepdims=True)
    carry[...] = new
    cnt_ref[...] = new

    ints_ref[0:1, :] = ea
    ints_ref[1:2, :] = eb
    ints_ref[2:3, :] = ra.astype(I32)
    ints_ref[3:4, :] = rb.astype(I32)
    wts_ref[0:1, :] = wa
    wts_ref[1:2, :] = wb


def _route(x, wrt, cin, tok0, t):
    d = wrt.shape[1]
    tt = min(TT_ROUTE, t)
    blk0 = tok0 // tt
    tri = _prefix_matrix(tt)
    return pl.pallas_call(
        _route_body,
        grid=(t // tt,),
        in_specs=[
            pl.BlockSpec((tt * SUBLANES, LANES), lambda i: (blk0 + i, 0)),
            pl.BlockSpec((ROUTE_ROWS, d), lambda i: (0, 0)),
            pl.BlockSpec((tt, tt), lambda i: (0, 0)),
            pl.BlockSpec((N_EXPERTS, LANES), lambda i: (0, 0)),
        ],
        out_specs=[
            pl.BlockSpec((4, tt), lambda i: (0, i)),
            pl.BlockSpec((2, tt), lambda i: (0, i)),
            pl.BlockSpec((N_EXPERTS, LANES), lambda i: (0, 0)),
        ],
        out_shape=[
            jax.ShapeDtypeStruct((4, t), I32),
            jax.ShapeDtypeStruct((2, t), F32),
            jax.ShapeDtypeStruct((N_EXPERTS, LANES), F32),
        ],
        scratch_shapes=[pltpu.VMEM((N_EXPERTS, LANES), F32)],
        compiler_params=pltpu.CompilerParams(
            dimension_semantics=("arbitrary",), vmem_limit_bytes=VMEM_LIMIT),
        name="route",
    )(x, wrt, tri, cin)


def _dest_body(ints_ref, pst_ref, dest_ref):
    ints = ints_ref[...]
    tt = ints.shape[1]
    rows_e = lax.broadcasted_iota(I32, (N_EXPERTS, tt), 0)
    pst = pst_ref[...][:, 0:1]
    for k in range(2):
        start = jnp.sum(jnp.where(rows_e == ints[k:k + 1], pst, 0.0), axis=0, keepdims=True)
        dest_ref[k:k + 1, :] = start.astype(I32) + ints[2 + k:3 + k]


def _dest(ints, pstart_f):
    t = ints.shape[1]
    return pl.pallas_call(
        _dest_body,
        out_shape=jax.ShapeDtypeStruct((2, t), I32),
        compiler_params=pltpu.CompilerParams(vmem_limit_bytes=VMEM_LIMIT),
        name="dest",
    )(ints, pstart_f)


def _invert_body(dest_ref, padpos_ref, inv_ref, *, t_total):
    def prime_body(i, c):
        inv_ref[i] = SPARE_ROW0 + i
        return c

    lax.fori_loop(0, INV_ROW0, prime_body, 0, unroll=8)

    def pad_body(e, c):
        q0 = padpos_ref[e]
        for r in range(BLK):
            inv_ref[q0 + r] = PAD_BASE + ((q0 + r) & (PAD_SPAN - 1))
        return c

    lax.fori_loop(0, N_EXPERTS, pad_body, 0)

    def tok_body(j, c):
        toks = [j * ROW_UNROLL + u for u in range(ROW_UNROLL)]
        rows = [[dest_ref[k * t_total + t] for k in range(2)] for t in toks]
        for t, qs in zip(toks, rows):
            for k in range(2):
                inv_ref[qs[k]] = t + (k << PLANE_BITS)
        return c

    lax.fori_loop(0, t_total // ROW_UNROLL, tok_body, 0)


def _invert(dest_flat, pad_pos, p_rows):
    t_total = dest_flat.shape[0] // 2
    grid_spec = pltpu.PrefetchScalarGridSpec(
        num_scalar_prefetch=2,
        grid=(1,),
        in_specs=[],
        out_specs=pl.BlockSpec(memory_space=pltpu.SMEM),
    )
    return pl.pallas_call(
        functools.partial(_invert_body, t_total=t_total),
        grid_spec=grid_spec,
        out_shape=jax.ShapeDtypeStruct((INV_ROW0 + p_rows + BLK,), I32),
        compiler_params=pltpu.CompilerParams(dimension_semantics=("arbitrary",)),
        name="invert",
    )(dest_flat, pad_pos)


def _experts_body(be_ref, nu_ref, nblk_ref, inv_ref, x_hbm, wg_hbm, wu_hbm, wd_hbm, y_hbm,
                  xbuf, obuf, sg, su, sd, wg_ref, wu_ref, wd_ref, wsem, gsem, ssem, slot_ref,
                  *, layer):
    b = pl.program_id(0)
    nu = nu_ref[0]
    e = be_ref[jnp.minimum(b, nu - 1)]

    def gather(blk, slot, fn):
        vs = [inv_ref[(blk + PRIME_BLOCKS) * BLK + r] for r in range(BLK)]
        for r, v in enumerate(vs):
            fn(pltpu.make_async_copy(_tile_of_row(x_hbm, v & TOKEN_MASK),
                                     _tile_of_row(xbuf.at[slot], r), gsem.at[slot]), 0)

    def scatter(blk, slot, fn):
        vs = [inv_ref[(blk + PRIME_BLOCKS) * BLK + r] for r in range(BLK)]
        for r, v in enumerate(vs):
            fn(pltpu.make_async_copy(_tile_of_row(obuf.at[slot], r),
                                     _tile_of_row(y_hbm, v), ssem.at[slot]), 1)

    def start(cp, priority):
        cp.start(priority=priority)

    def wait(cp, priority):
        cp.wait()

    def fetch(ex, slot):
        return (pltpu.make_async_copy(wg_hbm.at[layer, ex], sg.at[slot], wsem.at[slot, 0]),
                pltpu.make_async_copy(wu_hbm.at[layer, ex], su.at[slot], wsem.at[slot, 1]),
                pltpu.make_async_copy(wd_hbm.at[layer, ex], sd.at[slot], wsem.at[slot, 2]))

    @pl.when(b == 0)
    def _():
        slot_ref[0] = 0
        for cp in fetch(e, 0):
            cp.start()
        obuf[...] = jnp.zeros_like(obuf)
        scatter(-3, 0, start)
        scatter(-2, 1, start)
        gather(0, 0, start)
        gather(jnp.minimum(1, nu - 1), 1, start)

    first_of_expert = jnp.logical_or(b == 0, e != be_ref[jnp.maximum(b - 1, 0)])

    @pl.when(jnp.logical_and(b < nu, first_of_expert))
    def _():
        slot = slot_ref[0]
        nxt = b + nblk_ref[e]

        @pl.when(nxt < nu)
        def _():
            for cp in fetch(be_ref[nxt], 1 - slot):
                cp.start()

        for cp in fetch(e, slot):
            cp.wait()
        wg_ref[...] = sg[slot].astype(BF16)
        wu_ref[...] = su[slot].astype(BF16)
        wd_ref[...] = sd[slot].astype(BF16)
        slot_ref[0] = 1 - slot

    for slot in range(N_SLOTS):
        prev, nxt = (slot - 1) % N_SLOTS, (slot + 1) % N_SLOTS

        @pl.when(jnp.logical_and(b < nu, b % N_SLOTS == slot))
        def _(slot=slot, prev=prev):
            gather(b, slot, wait)
            scatter(b - 3, slot, wait)
            xb = _load_rows(xbuf, BLK, (slot,)).astype(BF16)
            scatter(b - 1, prev, start)
            gather(jnp.minimum(b + 2, nu - 1), prev, start)
            h = jax.nn.silu(_dot(xb, wg_ref[...])) * _dot(xb, wu_ref[...])
            o = _dot(h.astype(BF16), wd_ref[...])
            _store_rows(obuf, o, (slot,))

        @pl.when(jnp.logical_and(b == nu, b % N_SLOTS == slot))
        def _(slot=slot, prev=prev, nxt=nxt):
            scatter(b - 1, prev, start)
            scatter(b - 3, slot, wait)
            scatter(b - 2, nxt, wait)
            scatter(b - 1, prev, wait)
            gather(nu - 1, slot, wait)
            gather(nu - 1, nxt, wait)


def _experts(blk_e, n_used, nblk, inv, x, wg, wu, wd, layer):
    d, de = wg.shape[2], wg.shape[3]
    nb = blk_e.shape[0]
    hbm = pl.BlockSpec(memory_space=pl.ANY)
    grid_spec = pltpu.PrefetchScalarGridSpec(
        num_scalar_prefetch=4,
        grid=(nb + 1,),
        in_specs=[hbm, hbm, hbm, hbm],
        out_specs=hbm,
        scratch_shapes=[
            pltpu.VMEM((N_SLOTS, BLK * SUBLANES, LANES), F32),
            pltpu.VMEM((N_SLOTS, BLK * SUBLANES, LANES), F32),
            pltpu.VMEM((2, d, de), F32), pltpu.VMEM((2, d, de), F32), pltpu.VMEM((2, de, d), F32),
            pltpu.VMEM((d, de), BF16), pltpu.VMEM((d, de), BF16), pltpu.VMEM((de, d), BF16),
            pltpu.SemaphoreType.DMA((2, 3)), pltpu.SemaphoreType.DMA((N_SLOTS,)),
            pltpu.SemaphoreType.DMA((N_SLOTS,)), pltpu.SMEM((1,), I32),
        ],
    )
    return pl.pallas_call(
        functools.partial(_experts_body, layer=layer),
        grid_spec=grid_spec,
        out_shape=jax.ShapeDtypeStruct((Y_ROWS * SUBLANES, LANES), F32),
        compiler_params=pltpu.CompilerParams(
            dimension_semantics=("arbitrary",), vmem_limit_bytes=VMEM_LIMIT),
        name="experts",
    )(blk_e, n_used, nblk, inv, x, wg, wu, wd)


def _combine_body(x_ref, ya_ref, yb_ref, w_ref, g_ref, b_ref, o_ref):
    tc = o_ref.shape[0]
    w = w_ref[...]
    y = w[:, 0:1] * _load_rows(ya_ref, tc) + w[:, 1:2] * _load_rows(yb_ref, tc)
    o_ref[...] = _layer_norm(ALPHA * _load_rows(x_ref, tc) + y, g_ref[...], b_ref[...])


def _combine(x, y, w_cols, g, b, tok0, t):
    d = g.shape[1]
    tc = min(TC, t)
    blk0 = tok0 // tc
    plane = (1 << PLANE_BITS) // tc
    row_tiled = lambda first: pl.BlockSpec((tc * SUBLANES, LANES), lambda i: (first + i, 0))
    return pl.pallas_call(
        _combine_body,
        grid=(t // tc,),
        in_specs=[
            row_tiled(blk0), row_tiled(blk0), row_tiled(plane + blk0),
            pl.BlockSpec((tc, 2), lambda i: (blk0 + i, 0)),
            pl.BlockSpec((1, d), lambda i: (0, 0)),
            pl.BlockSpec((1, d), lambda i: (0, 0)),
        ],
        out_specs=pl.BlockSpec((tc, d), lambda i: (i, 0)),
        out_shape=jax.ShapeDtypeStruct((t, d), F32),
        compiler_params=pltpu.CompilerParams(
            dimension_semantics=("arbitrary",), vmem_limit_bytes=VMEM_LIMIT),
        name="combine",
    )(x, y, y, w_cols, g, b)


def _router_weight(w_group, w_expert):
    d = w_group.shape[0]
    wrt = jnp.zeros((ROUTE_ROWS, d), F32)
    wrt = wrt.at[0:N_EXPERTS].set(w_expert.T).at[GROUP_ROW0:GROUP_ROW0 + N_GROUPS].set(w_group.T)
    return wrt.astype(BF16)


def _moe_experts(x, tp, ts, routed_p, wrt, wg, wu, wd, layer):
    t_total = tp + ts
    assert PAD_SPAN <= t_total <= 1 << PLANE_BITS and tp % ts == 0
    ints_p, wts_p, cnt_p = routed_p
    ints_s, wts_s, cnt = _route(x, wrt, cnt_p, tp, ts)
    ints = jnp.concatenate([ints_p, ints_s], axis=1)
    wts = jnp.concatenate([wts_p, wts_s], axis=1)

    counts = cnt[:, 0].astype(I32)
    pcounts = (counts + BLK - 1) // BLK * BLK
    pend = jnp.cumsum(pcounts)
    pstart = pend - pcounts
    nb = (2 * t_total + N_EXPERTS * (BLK - 1) + BLK - 1) // BLK
    p_rows = nb * BLK
    n_used = (pend[-1] // BLK).astype(I32).reshape(1)
    blk_first = jnp.minimum(jnp.arange(nb, dtype=I32), n_used[0] - 1) * BLK
    blk_e = jnp.sum((pend[None, :] <= blk_first[:, None]).astype(I32), axis=1)
    nblk = pcounts // BLK

    pstart_f = jnp.broadcast_to((pstart + INV_ROW0).astype(F32)[:, None], (N_EXPERTS, LANES))
    dest_flat = _dest(ints, pstart_f).reshape(2 * t_total)

    inv = _invert(dest_flat, pstart + counts + INV_ROW0, p_rows)
    return _experts(blk_e, n_used, nblk, inv, x, wg, wu, wd, layer), wts.T


def kernel(x_prompt, x_sample, state_conv_a, state_conv_b, state_h, a_w_in, a_conv_w, a_w_out,
           b_w_in, b_conv_w, b_conv_b, b_gate_a_w, b_gate_a_b, b_gate_x_w, b_gate_x_b, b_lambda,
           b_w_out, ln1_g, ln1_b, ln2_g, ln2_b, moe_w_group, moe_w_expert, moe_w_gate, moe_w_up,
           moe_w_down):
    bsz, seq, d = x_prompt.shape
    n_s = x_sample.shape[0]
    row = lambda v: v.reshape(1, d)

    tp = bsz * seq
    win, wout = a_w_in[0], a_w_out[0]
    wrt = _router_weight(moe_w_group[0], moe_w_expert[0])
    x1, conv_a_p, *routed = _conv_a_prompt(x_prompt, win, a_conv_w[0], wout, row(ln1_g[0]),
                                           row(ln1_b[0]), wrt, n_s)
    sa = state_conv_a[0]
    x1, u_s = _conv_a_sample(x_sample.reshape(n_s, d), sa[:, 0], sa[:, 1], win, a_conv_w[0], wout,
                             row(ln1_g[0]), row(ln1_b[0]), x1)
    conv_a_s = jnp.stack([sa[:, 1], u_s], axis=1)

    y, w_cols = _moe_experts(x1, tp, n_s, routed, wrt, moe_w_gate, moe_w_up, moe_w_down, 0)
    g2, b2 = row(ln2_g[0]), row(ln2_b[0])
    xs = _combine(x1, y, w_cols, g2, b2, tp, n_s)

    win, wout = b_w_in[0], b_w_out[0]
    wcat = jnp.concatenate([b_gate_a_w[0], b_gate_x_w[0]], axis=-1).astype(BF16)
    args = (win, b_conv_w[0], row(b_conv_b[0]), wcat, row(b_gate_a_b[0]), row(b_gate_x_b[0]),
            row(b_lambda[0]), wout, row(ln1_g[1]), row(ln1_b[1]))
    wrt = _router_weight(moe_w_group[1], moe_w_expert[1])
    x1, conv_b_p, h_p, *routed = _rglru_prompt(x1, y, w_cols, g2, b2, bsz, seq, *args, wrt, n_s)
    sb = state_conv_b[0]
    x1, xr_s, h_s = _rglru_sample(xs, sb[:, 0], sb[:, 1], sb[:, 2], state_h[0], *args, x1)
    conv_b_s = jnp.stack([sb[:, 1], sb[:, 2], xr_s], axis=1)

    y, w_cols = _moe_experts(x1, tp, n_s, routed, wrt, moe_w_gate, moe_w_up, moe_w_down, 1)
    g2, b2 = row(ln2_g[1]), row(ln2_b[1])
    xp = _combine(x1, y, w_cols, g2, b2, 0, tp)
    xs = _combine(x1, y, w_cols, g2, b2, tp, n_s)

    return (xp.reshape(bsz, seq, d), xs.reshape(n_s, 1, d),
            conv_a_p[None], conv_a_s[None], conv_b_p[None], conv_b_s[None],
            h_p.reshape(1, bsz, d), h_s[None])
```

```python
import functools

import jax
import jax.numpy as jnp
from jax import lax
from jax.experimental import pallas as pl
from jax.experimental.pallas import tpu as pltpu

F32 = jnp.float32
BF16 = jnp.bfloat16
I32 = jnp.int32

DEPTH = 2
N_RG_BLOCKS = 8
RG_C = 8.0
N_GROUPS = 4
EXP_PER_GROUP = 8
N_EXPERTS = N_GROUPS * EXP_PER_GROUP
ALPHA = (2.0 * DEPTH) ** 0.25
LN_EPS = 1e-5

LANES = 128
SUBLANES = 8
VMEM_LIMIT = 56 * 1024 * 1024

TS_A = 512
TS_B = 512
TT_ROUTE = 1024
TC = 1024
BLK = 128
ROUTE_ROWS = 128
GROUP_ROW0 = N_EXPERTS
W_CHUNK = 512
RG_UNIT = 2
ROW_UNROLL = 8

PLANE_BITS = 15
TOKEN_MASK = (1 << PLANE_BITS) - 1
PAD_BASE = 2 << PLANE_BITS
N_SLOTS = 3
PRIME_BLOCKS = N_SLOTS
INV_ROW0 = PRIME_BLOCKS * BLK
PAD_SPAN = 4 * BLK
SPARE_ROW0 = PAD_BASE + PAD_SPAN
Y_ROWS = SPARE_ROW0 + PRIME_BLOCKS * BLK


def _dot(a, b):
    return jnp.dot(a, b, preferred_element_type=F32)


def _load_rows(ref, m, idx=()):
    return jnp.concatenate(
        [ref[idx + (pl.ds(s, m, stride=SUBLANES), slice(None))] for s in range(SUBLANES)], axis=1)


def _store_rows(ref, v, idx=()):
    m = v.shape[0]
    for s in range(SUBLANES):
        ref[idx + (pl.ds(s, m, stride=SUBLANES), slice(None))] = v[:, s * LANES:(s + 1) * LANES]


def _tile_of_row(ref, r):
    return ref.at[pl.ds(pl.multiple_of(r * SUBLANES, SUBLANES), SUBLANES)]


def _load_weight_bf16(w_hbm, w_bf, stage, sem):
    nch = w_hbm.shape[1] // W_CHUNK

    def chunk_copy(c):
        return pltpu.make_async_copy(w_hbm.at[:, pl.ds(c * W_CHUNK, W_CHUNK)],
                                     stage.at[c % 2], sem.at[c % 2])

    chunk_copy(0).start()
    for c in range(nch):
        if c + 1 < nch:
            chunk_copy(c + 1).start()
        chunk_copy(c).wait()
        w_bf[:, c * W_CHUNK:(c + 1) * W_CHUNK] = stage[c % 2].astype(BF16)


def _weight_scratch(k, *ns):
    return ([pltpu.VMEM((k, n), BF16) for n in ns]
            + [pltpu.VMEM((2, k, W_CHUNK), F32), pltpu.SemaphoreType.DMA((2,))])


def _layer_norm(r, g, b):
    mu = jnp.mean(r, axis=-1, keepdims=True)
    d = r - mu
    var = jnp.mean(d * d, axis=-1, keepdims=True)
    return d * lax.rsqrt(var + LN_EPS) * g + b


def _shift_rows(v, k, prev8):
    rolled = pltpu.roll(v, k, axis=0)
    rows8 = lax.broadcasted_iota(I32, (SUBLANES, v.shape[1]), 0)
    first = jnp.where(rows8 < k, pltpu.roll(prev8, k, axis=0), rolled[0:SUBLANES])
    return jnp.concatenate([first, rolled[SUBLANES:]], axis=0)


def _softplus(v):
    return jnp.maximum(v, 0.0) + jnp.log1p(jnp.exp(-jnp.abs(v)))


def _sigmoid(z):
    return 0.5 * jnp.tanh(0.5 * z) + 0.5


def _rglru_coeffs(xc, wcat_ref, gab, gxb, lam, blk0=0):
    blk = wcat_ref.shape[1]
    xcb = xc.astype(BF16)
    rs, is_ = [], []
    for n in range(xc.shape[1] // blk):
        o = _dot(xcb[:, n * blk:(n + 1) * blk], wcat_ref[blk0 + n])
        rs.append(o[:, :blk])
        is_.append(o[:, blk:])
    r = _sigmoid(jnp.concatenate(rs, axis=1) + gab)
    i = _sigmoid(jnp.concatenate(is_, axis=1) + gxb)
    neg_log_a = RG_C * r * _softplus(-lam)
    a = jnp.exp(-neg_log_a)
    v = jnp.tanh(neg_log_a) * (a * a + 1.0)
    mult = jnp.where(v > 0.0, v * lax.rsqrt(v), 0.0)
    return a, mult * (i * xc)


def _scan_rows(a, b, h0):
    m, d = a.shape
    groups = m // SUBLANES
    a = a.reshape(groups, SUBLANES, d)
    b = b.reshape(groups, SUBLANES, d)
    sub = lax.broadcasted_iota(I32, a.shape, 1)
    for k in (1, 2, 4):
        keep = sub >= k
        a_sh = jnp.where(keep, pltpu.roll(a, k, axis=1), 1.0)
        b_sh = jnp.where(keep, pltpu.roll(b, k, axis=1), 0.0)
        b = a * b_sh + b
        a = a * a_sh
    outs = []
    h = h0
    for g in range(groups):
        hg = a[g] * h + b[g]
        outs.append(hg)
        h = hg[SUBLANES - 1:SUBLANES]
    return jnp.concatenate(outs, axis=0), h


def _conv_a_prompt_body(x_ref, win_hbm, cw_ref, wout_hbm, g_ref, b_ref, wrt_ref, tri_ref,
                        o_ref, buf_ref, ints_ref, wts_ref, cnt_ref,
                        carry, counts, win_ref, wout_ref, stage, wsem):
    s = pl.program_id(1)

    @pl.when(jnp.logical_and(pl.program_id(0) == 0, s == 0))
    def _():
        _load_weight_bf16(win_hbm, win_ref, stage, wsem)
        _load_weight_bf16(wout_hbm, wout_ref, stage, wsem)
        counts[...] = jnp.zeros_like(counts)

    @pl.when(s == 0)
    def _():
        carry[...] = jnp.zeros_like(carry)

    x = x_ref[0]
    d = x.shape[1]
    bcx = _dot(x.astype(BF16), win_ref[...])
    gb, gc, xh = bcx[:, :d], bcx[:, d:2 * d], bcx[:, 2 * d:]
    u = gc * xh
    prev = carry[...]
    cw = cw_ref[...]
    conv = (cw[0:1] * _shift_rows(u, 2, prev) + cw[1:2] * _shift_rows(u, 1, prev)
            + cw[2:3] * u)
    y = _dot((gb * conv).astype(BF16), wout_ref[...])
    x1 = _layer_norm(ALPHA * x + y, g_ref[...], b_ref[...])
    _store_rows(o_ref, x1)
    _route_tile(x1.astype(BF16), wrt_ref, tri_ref, counts, ints_ref, wts_ref, cnt_ref)
    ts = u.shape[0]
    carry[...] = u[ts - SUBLANES:ts]

    @pl.when(s == pl.num_programs(1) - 1)
    def _():
        buf_ref[0] = u[ts - 2:ts]


def _prefix_matrix(tt):
    return (jnp.arange(tt)[:, None] < jnp.arange(tt)[None, :]).astype(BF16)


def _conv_a_prompt(x, win, cw, wout, g, b, wrt, n_extra):
    bsz, seq, d = x.shape
    ts = min(TS_A, seq)
    grid = (bsz, seq // ts)
    nj = seq // ts
    tp = bsz * seq
    const2 = lambda i, j: (0, 0)
    return pl.pallas_call(
        _conv_a_prompt_body,
        grid=grid,
        in_specs=[
            pl.BlockSpec((1, ts, d), lambda i, j: (i, j, 0)),
            pl.BlockSpec(memory_space=pl.ANY),
            pl.BlockSpec((3, d), const2),
            pl.BlockSpec(memory_space=pl.ANY),
            pl.BlockSpec((1, d), const2),
            pl.BlockSpec((1, d), const2),
            pl.BlockSpec((ROUTE_ROWS, d), const2),
            pl.BlockSpec((ts, ts), const2),
        ],
        out_specs=[
            pl.BlockSpec((ts * SUBLANES, LANES), lambda i, j: (i * nj + j, 0)),
            pl.BlockSpec((1, 2, d), lambda i, j: (i, 0, 0)),
            pl.BlockSpec((4, ts), lambda i, j: (0, i * nj + j)),
            pl.BlockSpec((2, ts), lambda i, j: (0, i * nj + j)),
            pl.BlockSpec((N_EXPERTS, LANES), const2),
        ],
        out_shape=[
            jax.ShapeDtypeStruct(((tp + n_extra) * SUBLANES, LANES), F32),
            jax.ShapeDtypeStruct((bsz, 2, d), F32),
            jax.ShapeDtypeStruct((4, tp), I32),
            jax.ShapeDtypeStruct((2, tp), F32),
            jax.ShapeDtypeStruct((N_EXPERTS, LANES), F32),
        ],
        scratch_shapes=([pltpu.VMEM((SUBLANES, d), F32), pltpu.VMEM((N_EXPERTS, LANES), F32)]
                        + _weight_scratch(d, 3 * d, d)),
        compiler_params=pltpu.CompilerParams(
            dimension_semantics=("arbitrary", "arbitrary"), vmem_limit_bytes=VMEM_LIMIT),
        name="conv_a_prompt",
    )(x, win, cw, wout, g, b, wrt, _prefix_matrix(ts))


def _conv_a_sample_body(x_ref, s0_ref, s1_ref, win_hbm, cw_ref, wout_hbm, g_ref, b_ref, joint_ref,
                        o_ref, u_ref, win_ref, wout_ref, stage, wsem):
    del joint_ref
    _load_weight_bf16(win_hbm, win_ref, stage, wsem)
    _load_weight_bf16(wout_hbm, wout_ref, stage, wsem)
    x = x_ref[...]
    d = x.shape[1]
    bcx = _dot(x.astype(BF16), win_ref[...])
    gb, gc, xh = bcx[:, :d], bcx[:, d:2 * d], bcx[:, 2 * d:]
    u = gc * xh
    cw = cw_ref[...]
    conv = cw[0:1] * s0_ref[...] + cw[1:2] * s1_ref[...] + cw[2:3] * u
    y = _dot((gb * conv).astype(BF16), wout_ref[...])
    _store_rows(o_ref, _layer_norm(ALPHA * x + y, g_ref[...], b_ref[...]))
    u_ref[...] = u


def _whole(a):
    return pl.BlockSpec(a.shape, lambda i: (0,) * a.ndim)


def _conv_a_sample(x, s0, s1, win, cw, wout, g, b, joint):
    n, d = x.shape
    hbm = pl.BlockSpec(memory_space=pl.ANY)
    first_block = joint.shape[0] // (n * SUBLANES) - 1
    return pl.pallas_call(
        _conv_a_sample_body,
        grid=(1,),
        in_specs=[_whole(x), _whole(s0), _whole(s1), hbm, _whole(cw), hbm, _whole(g), _whole(b),
                  hbm],
        out_specs=[pl.BlockSpec((n * SUBLANES, LANES), lambda i: (first_block, 0)),
                   pl.BlockSpec((n, d), lambda i: (0, 0))],
        out_shape=[jax.ShapeDtypeStruct(joint.shape, F32), jax.ShapeDtypeStruct((n, d), F32)],
        scratch_shapes=_weight_scratch(d, 3 * d, d),
        input_output_aliases={8: 0},
        compiler_params=pltpu.CompilerParams(
            dimension_semantics=("arbitrary",), vmem_limit_bytes=VMEM_LIMIT),
        name="conv_a_sample",
    )(x, s0, s1, win, cw, wout, g, b, joint)


def _rglru_prompt_body(x1_ref, ya_ref, yb_ref, w_ref, g2_ref, b2_ref, win_hbm, cw_ref, cb_ref,
                       wcat_ref, gab_ref, gxb_ref, lam_ref, wout_hbm, g_ref, b_ref, wrt_ref, tri_ref,
                       o_ref, buf_ref, hl_ref, ints_ref, wts_ref, cnt_ref,
                       xcarry, hcarry, counts, x_scr, gx_scr, win_ref, wout_ref, stage, wsem,
                       *, nj):
    i = pl.program_id(0)
    t = i - 1
    s = t % nj

    def layer_input():
        ts = x_scr.shape[1]
        w = w_ref[...]
        y = w[:, 0:1] * _load_rows(ya_ref, ts) + w[:, 1:2] * _load_rows(yb_ref, ts)
        return _layer_norm(ALPHA * _load_rows(x1_ref, ts) + y, g2_ref[...], b2_ref[...])

    @pl.when(i == 0)
    def _():
        _load_weight_bf16(win_hbm, win_ref, stage, wsem)
        _load_weight_bf16(wout_hbm, wout_ref, stage, wsem)
        x0 = layer_input()
        x_scr[0] = x0
        gx_scr[0] = _dot(x0.astype(BF16), win_ref[...])
        counts[...] = jnp.zeros_like(counts)

    @pl.when(jnp.logical_and(i > 0, s == 0))
    def _():
        xcarry[...] = jnp.zeros_like(xcarry)
        hcarry[...] = jnp.zeros_like(hcarry)

    for par in range(2):
        @pl.when(jnp.logical_and(i > 0, t % 2 == par))
        def _(par=par):
            x_next = layer_input()
            x_scr[1 - par] = x_next
            xn = x_next.astype(BF16)
            x = x_scr[par]
            ts, d = x.shape
            blk = d // N_RG_BLOCKS
            zs = []
            for u in range(N_RG_BLOCKS // RG_UNIT):
                c0, c1 = u * RG_UNIT * blk, (u + 1) * RG_UNIT * blk
                for half in (0, d):
                    gx_scr[1 - par, :, half + c0:half + c1] = _dot(
                        xn, win_ref[:, half + c0:half + c1])
                gate, xr = gx_scr[par, :, c0:c1], gx_scr[par, :, d + c0:d + c1]
                prev = xcarry[:, c0:c1]
                cw = cw_ref[:, c0:c1]
                xc = (cw[0:1] * _shift_rows(xr, 3, prev) + cw[1:2] * _shift_rows(xr, 2, prev)
                      + cw[2:3] * _shift_rows(xr, 1, prev) + cw[3:4] * xr) + cb_ref[:, c0:c1]
                a, bt = _rglru_coeffs(xc, wcat_ref, gab_ref[:, c0:c1], gxb_ref[:, c0:c1],
                                      lam_ref[:, c0:c1], u * RG_UNIT)
                hs, hlast = _scan_rows(a, bt, hcarry[0:1, c0:c1])
                zs.append((jax.nn.gelu(gate, approximate=True) * hs).astype(BF16))
                xcarry[:, c0:c1] = xr[ts - SUBLANES:ts]
                hcarry[:, c0:c1] = jnp.broadcast_to(hlast, (SUBLANES, c1 - c0))

            y = _dot(jnp.concatenate(zs, axis=1), wout_ref[...])
            x1 = _layer_norm(ALPHA * x + y, g_ref[...], b_ref[...])
            _store_rows(o_ref, x1)
            _route_tile(x1.astype(BF16), wrt_ref, tri_ref, counts, ints_ref, wts_ref, cnt_ref)

    @pl.when(jnp.logical_and(i > 0, s == nj - 1))
    def _():
        buf_ref[0] = xcarry[SUBLANES - 3:SUBLANES]
        hl_ref[0] = hcarry[0:1]


def _rglru_prompt(x_prev, y, w_cols, g2, b2, bsz, seq,
                  win, cw, cb, wcat, gab, gxb, lam, wout, g, b, wrt, n_extra):
    d = g.shape[1]
    ts = min(TS_B, seq)
    nj = seq // ts
    n = bsz * nj
    tp = bsz * seq
    blk = d // N_RG_BLOCKS
    plane = (1 << PLANE_BITS) // ts
    const2 = lambda i: (0, 0)
    nxt = lambda i: jnp.minimum(i, n - 1)
    cur = lambda i: jnp.maximum(i - 1, 0)
    return pl.pallas_call(
        functools.partial(_rglru_prompt_body, nj=nj),
        grid=(n + 1,),
        in_specs=[
            pl.BlockSpec((ts * SUBLANES, LANES), lambda i: (nxt(i), 0)),
            pl.BlockSpec((ts * SUBLANES, LANES), lambda i: (nxt(i), 0)),
            pl.BlockSpec((ts * SUBLANES, LANES), lambda i: (plane + nxt(i), 0)),
            pl.BlockSpec((ts, 2), lambda i: (nxt(i), 0)),
            pl.BlockSpec((1, d), const2),
            pl.BlockSpec((1, d), const2),
            pl.BlockSpec(memory_space=pl.ANY),
            pl.BlockSpec((4, d), const2),
            pl.BlockSpec((1, d), const2),
            pl.BlockSpec((N_RG_BLOCKS, blk, 2 * blk), lambda i: (0, 0, 0)),
            pl.BlockSpec((1, d), const2),
            pl.BlockSpec((1, d), const2),
            pl.BlockSpec((1, d), const2),
            pl.BlockSpec(memory_space=pl.ANY),
            pl.BlockSpec((1, d), const2),
            pl.BlockSpec((1, d), const2),
            pl.BlockSpec((ROUTE_ROWS, d), const2),
            pl.BlockSpec((ts, ts), const2),
        ],
        out_specs=[
            pl.BlockSpec((ts * SUBLANES, LANES), lambda i: (cur(i), 0)),
            pl.BlockSpec((1, 3, d), lambda i: (cur(i) // nj, 0, 0)),
            pl.BlockSpec((1, 1, d), lambda i: (cur(i) // nj, 0, 0)),
            pl.BlockSpec((4, ts), lambda i: (0, cur(i))),
            pl.BlockSpec((2, ts), lambda i: (0, cur(i))),
            pl.BlockSpec((N_EXPERTS, LANES), const2),
        ],
        out_shape=[
            jax.ShapeDtypeStruct(((tp + n_extra) * SUBLANES, LANES), F32),
            jax.ShapeDtypeStruct((bsz, 3, d), F32),
            jax.ShapeDtypeStruct((bsz, 1, d), F32),
            jax.ShapeDtypeStruct((4, tp), I32),
            jax.ShapeDtypeStruct((2, tp), F32),
            jax.ShapeDtypeStruct((N_EXPERTS, LANES), F32),
        ],
        scratch_shapes=([pltpu.VMEM((SUBLANES, d), F32), pltpu.VMEM((SUBLANES, d), F32),
                         pltpu.VMEM((N_EXPERTS, LANES), F32), pltpu.VMEM((2, ts, d), F32),
                         pltpu.VMEM((2, ts, 2 * d), F32)]
                        + _weight_scratch(d, 2 * d, d)),
        compiler_params=pltpu.CompilerParams(
            dimension_semantics=("arbitrary",), vmem_limit_bytes=VMEM_LIMIT),
        name="rglru_prompt",
    )(x_prev, y, y, w_cols, g2, b2, win, cw, cb, wcat, gab, gxb, lam, wout, g, b, wrt,
      _prefix_matrix(ts))


def _rglru_sample_body(x_ref, s0_ref, s1_ref, s2_ref, h0_ref, win_hbm, cw_ref, cb_ref, wcat_ref,
                       gab_ref, gxb_ref, lam_ref, wout_hbm, g_ref, b_ref, joint_ref,
                       o_ref, xr_ref, h_ref, win_ref, wout_ref, stage, wsem):
    del joint_ref
    _load_weight_bf16(win_hbm, win_ref, stage, wsem)
    _load_weight_bf16(wout_hbm, wout_ref, stage, wsem)
    x = x_ref[...]
    d = x.shape[1]
    gx = _dot(x.astype(BF16), win_ref[...])
    gate, xr = gx[:, :d], gx[:, d:]
    cw = cw_ref[...]
    xc = (cw[0:1] * s0_ref[...] + cw[1:2] * s1_ref[...] + cw[2:3] * s2_ref[...]
          + cw[3:4] * xr) + cb_ref[...]
    a, bt = _rglru_coeffs(xc, wcat_ref, gab_ref[...], gxb_ref[...], lam_ref[...])
    h = a * h0_ref[...] + bt
    y = _dot((jax.nn.gelu(gate, approximate=True) * h).astype(BF16), wout_ref[...])
    _store_rows(o_ref, _layer_norm(ALPHA * x + y, g_ref[...], b_ref[...]))
    xr_ref[...] = xr
    h_ref[...] = h


def _rglru_sample(x, s0, s1, s2, h0, win, cw, cb, wcat, gab, gxb, lam, wout, g, b, joint):
    n, d = x.shape
    hbm = pl.BlockSpec(memory_space=pl.ANY)
    first_block = joint.shape[0] // (n * SUBLANES) - 1
    vec = pl.BlockSpec((n, d), lambda i: (0, 0))
    return pl.pallas_call(
        _rglru_sample_body,
        grid=(1,),
        in_specs=[_whole(x), _whole(s0), _whole(s1), _whole(s2), _whole(h0), hbm, _whole(cw),
                  _whole(cb), _whole(wcat), _whole(gab), _whole(gxb), _whole(lam), hbm,
                  _whole(g), _whole(b), hbm],
        out_specs=[pl.BlockSpec((n * SUBLANES, LANES), lambda i: (first_block, 0)), vec, vec],
        out_shape=[jax.ShapeDtypeStruct(joint.shape, F32),
                   jax.ShapeDtypeStruct((n, d), F32), jax.ShapeDtypeStruct((n, d), F32)],
        scratch_shapes=_weight_scratch(d, 2 * d, d),
        input_output_aliases={15: 0},
        compiler_params=pltpu.CompilerParams(
            dimension_semantics=("arbitrary",), vmem_limit_bytes=VMEM_LIMIT),
        name="rglru_sample",
    )(x, s0, s1, s2, h0, win, cw, cb, wcat, gab, gxb, lam, wout, g, b, joint)


def _first_argmax(v, rows):
    m = jnp.max(v, axis=0, keepdims=True)
    idx = jnp.min(jnp.where(v == m, rows, v.shape[0]), axis=0, keepdims=True)
    return m, idx


def _route_body(x_ref, wrt_ref, tri_ref, cin_ref, ints_ref, wts_ref, cnt_ref, carry):
    @pl.when(pl.program_id(0) == 0)
    def _():
        carry[...] = cin_ref[...]

    tt = x_ref.shape[0] // SUBLANES
    _route_tile(_load_rows(x_ref, tt).astype(BF16), wrt_ref, tri_ref, carry, ints_ref, wts_ref,
                cnt_ref)


def _route_tile(xb, wrt_ref, tri_ref, carry, ints_ref, wts_ref, cnt_ref):
    tt = xb.shape[0]
    lt = lax.dot_general(wrt_ref[...], xb, (((1,), (1,)), ((), ())), preferred_element_type=F32)
    rows8 = lax.broadcasted_iota(I32, (SUBLANES, tt), 0)
    neg_inf = jnp.float32(-jnp.inf)

    gl = jnp.where(rows8 < N_GROUPS, lt[GROUP_ROW0:GROUP_ROW0 + SUBLANES], neg_inf)
    gmax, gidx = _first_argmax(gl, rows8)
    gw = 1.0 / jnp.sum(jnp.exp(gl - gmax), axis=0, keepdims=True)

    el = lt[0:EXP_PER_GROUP]
    for g in range(1, N_GROUPS):
        el = jnp.where(gidx == g, lt[g * EXP_PER_GROUP:(g + 1) * EXP_PER_GROUP], el)
    emax, i1 = _first_argmax(el, rows8)
    el2 = jnp.where(rows8 == i1, neg_inf, el)
    m2, i2 = _first_argmax(el2, rows8)
    psum = jnp.sum(jnp.exp(el - emax), axis=0, keepdims=True)
    ep1 = 1.0 / psum
    ep2 = jnp.exp(m2 - emax) / psum
    tot = ep1 + ep2
    wa = gw * (ep1 / tot)
    wb = gw * (ep2 / tot)
    ea = gidx * EXP_PER_GROUP + i1
    eb = gidx * EXP_PER_GROUP + i2

    rows_e = lax.broadcasted_iota(I32, (N_EXPERTS, tt), 0)
    oha = rows_e == ea
    ohb = rows_e == eb
    oh = jnp.where(oha | ohb, 1.0, 0.0)
    base = carry[...][:, 0:1]
    excl = _dot(oh.astype(BF16), tri_ref[...]) + base
    ra = jnp.sum(jnp.where(oha, excl, 0.0), axis=0, keepdims=True)
    rb = jnp.sum(jnp.where(ohb, excl, 0.0), axis=0, keepdims=True)
    new = carry[...] + jnp.sum(oh, axis=1, keepdims=True)
    carry[...] = new
    cnt_ref[...] = new

    ints_ref[0:1, :] = ea
    ints_ref[1:2, :] = eb
    ints_ref[2:3, :] = ra.astype(I32)
    ints_ref[3:4, :] = rb.astype(I32)
    wts_ref[0:1, :] = wa
    wts_ref[1:2, :] = wb


def _route(x, wrt, cin, tok0, t):
    d = wrt.shape[1]
    tt = min(TT_ROUTE, t)
    blk0 = tok0 // tt
    tri = _prefix_matrix(tt)
    return pl.pallas_call(
        _route_body,
        grid=(t // tt,),
        in_specs=[
            pl.BlockSpec((tt * SUBLANES, LANES), lambda i: (blk0 + i, 0)),
            pl.BlockSpec((ROUTE_ROWS, d), lambda i: (0, 0)),
            pl.BlockSpec((tt, tt), lambda i: (0, 0)),
            pl.BlockSpec((N_EXPERTS, LANES), lambda i: (0, 0)),
        ],
        out_specs=[
            pl.BlockSpec((4, tt), lambda i: (0, i)),
            pl.BlockSpec((2, tt), lambda i: (0, i)),
            pl.BlockSpec((N_EXPERTS, LANES), lambda i: (0, 0)),
        ],
        out_shape=[
            jax.ShapeDtypeStruct((4, t), I32),
            jax.ShapeDtypeStruct((2, t), F32),
            jax.ShapeDtypeStruct((N_EXPERTS, LANES), F32),
        ],
        scratch_shapes=[pltpu.VMEM((N_EXPERTS, LANES), F32)],
        compiler_params=pltpu.CompilerParams(
            dimension_semantics=("arbitrary",), vmem_limit_bytes=VMEM_LIMIT),
        name="route",
    )(x, wrt, tri, cin)


def _dest_body(ints_ref, pst_ref, dest_ref):
    ints = ints_ref[...]
    tt = ints.shape[1]
    rows_e = lax.broadcasted_iota(I32, (N_EXPERTS, tt), 0)
    pst = pst_ref[...][:, 0:1]
    for k in range(2):
        start = jnp.sum(jnp.where(rows_e == ints[k:k + 1], pst, 0.0), axis=0, keepdims=True)
        dest_ref[k:k + 1, :] = start.astype(I32) + ints[2 + k:3 + k]


def _dest(ints, pstart_f):
    t = ints.shape[1]
    return pl.pallas_call(
        _dest_body,
        out_shape=jax.ShapeDtypeStruct((2, t), I32),
        compiler_params=pltpu.CompilerParams(vmem_limit_bytes=VMEM_LIMIT),
        name="dest",
    )(ints, pstart_f)


def _invert_body(dest_ref, padpos_ref, inv_ref, *, t_total):
    def prime_body(i, c):
        inv_ref[i] = SPARE_ROW0 + i
        return c

    lax.fori_loop(0, INV_ROW0, prime_body, 0, unroll=8)

    def pad_body(e, c):
        q0 = padpos_ref[e]
        for r in range(BLK):
            inv_ref[q0 + r] = PAD_BASE + ((q0 + r) & (PAD_SPAN - 1))
        return c

    lax.fori_loop(0, N_EXPERTS, pad_body, 0)

    def tok_body(j, c):
        toks = [j * ROW_UNROLL + u for u in range(ROW_UNROLL)]
        rows = [[dest_ref[k * t_total + t] for k in range(2)] for t in toks]
        for t, qs in zip(toks, rows):
            for k in range(2):
                inv_ref[qs[k]] = t + (k << PLANE_BITS)
        return c

    lax.fori_loop(0, t_total // ROW_UNROLL, tok_body, 0)


def _invert(dest_flat, pad_pos, p_rows):
    t_total = dest_flat.shape[0] // 2
    grid_spec = pltpu.PrefetchScalarGridSpec(
        num_scalar_prefetch=2,
        grid=(1,),
        in_specs=[],
        out_specs=pl.BlockSpec(memory_space=pltpu.SMEM),
    )
    return pl.pallas_call(
        functools.partial(_invert_body, t_total=t_total),
        grid_spec=grid_spec,
        out_shape=jax.ShapeDtypeStruct((INV_ROW0 + p_rows + BLK,), I32),
        compiler_params=pltpu.CompilerParams(dimension_semantics=("arbitrary",)),
        name="invert",
    )(dest_flat, pad_pos)


def _experts_body(be_ref, nu_ref, nblk_ref, inv_ref, x_hbm, wg_hbm, wu_hbm, wd_hbm, y_hbm,
                  xbuf, obuf, sg, su, sd, wg_ref, wu_ref, wd_ref, wsem, gsem, ssem, slot_ref,
                  *, layer):
    b = pl.program_id(0)
    nu = nu_ref[0]
    e = be_ref[jnp.minimum(b, nu - 1)]

    def gather(blk, slot, fn):
        vs = [inv_ref[(blk + PRIME_BLOCKS) * BLK + r] for r in range(BLK)]
        for r, v in enumerate(vs):
            fn(pltpu.make_async_copy(_tile_of_row(x_hbm, v & TOKEN_MASK),
                                     _tile_of_row(xbuf.at[slot], r), gsem.at[slot]), 0)

    def scatter(blk, slot, fn):
        vs = [inv_ref[(blk + PRIME_BLOCKS) * BLK + r] for r in range(BLK)]
        for r, v in enumerate(vs):
            fn(pltpu.make_async_copy(_tile_of_row(obuf.at[slot], r),
                                     _tile_of_row(y_hbm, v), ssem.at[slot]), 1)

    def start(cp, priority):
        cp.start(priority=priority)

    def wait(cp, priority):
        cp.wait()

    def fetch(ex, slot):
        return (pltpu.make_async_copy(wg_hbm.at[layer, ex], sg.at[slot], wsem.at[slot, 0]),
                pltpu.make_async_copy(wu_hbm.at[layer, ex], su.at[slot], wsem.at[slot, 1]),
                pltpu.make_async_copy(wd_hbm.at[layer, ex], sd.at[slot], wsem.at[slot, 2]))

    @pl.when(b == 0)
    def _():
        slot_ref[0] = 0
        for cp in fetch(e, 0):
            cp.start()
        obuf[...] = jnp.zeros_like(obuf)
        scatter(-3, 0, start)
        scatter(-2, 1, start)
        gather(0, 0, start)
        gather(jnp.minimum(1, nu - 1), 1, start)

    first_of_expert = jnp.logical_or(b == 0, e != be_ref[jnp.maximum(b - 1, 0)])

    @pl.when(jnp.logical_and(b < nu, first_of_expert))
    def _():
        slot = slot_ref[0]
        nxt = b + nblk_ref[e]

        @pl.when(nxt < nu)
        def _():
            for cp in fetch(be_ref[nxt], 1 - slot):
                cp.start()

        for cp in fetch(e, slot):
            cp.wait()
        wg_ref[...] = sg[slot].astype(BF16)
        wu_ref[...] = su[slot].astype(BF16)
        wd_ref[...] = sd[slot].astype(BF16)
        slot_ref[0] = 1 - slot

    for slot in range(N_SLOTS):
        prev, nxt = (slot - 1) % N_SLOTS, (slot + 1) % N_SLOTS

        @pl.when(jnp.logical_and(b < nu, b % N_SLOTS == slot))
        def _(slot=slot, prev=prev):
            gather(b, slot, wait)
            scatter(b - 3, slot, wait)
            xb = _load_rows(xbuf, BLK, (slot,)).astype(BF16)
            scatter(b - 1, prev, start)
            gather(jnp.minimum(b + 2, nu - 1), prev, start)
            h = jax.nn.silu(_dot(xb, wg_ref[...])) * _dot(xb, wu_ref[...])
            o = _dot(h.astype(BF16), wd_ref[...])
            _store_rows(obuf, o, (slot,))

        @pl.when(jnp.logical_and(b == nu, b % N_SLOTS == slot))
        def _(slot=slot, prev=prev, nxt=nxt):
            scatter(b - 1, prev, start)
            scatter(b - 3, slot, wait)
            scatter(b - 2, nxt, wait)
            scatter(b - 1, prev, wait)
            gather(nu - 1, slot, wait)
            gather(nu - 1, nxt, wait)


def _experts(blk_e, n_used, nblk, inv, x, wg, wu, wd, layer):
    d, de = wg.shape[2], wg.shape[3]
    nb = blk_e.shape[0]
    hbm = pl.BlockSpec(memory_space=pl.ANY)
    grid_spec = pltpu.PrefetchScalarGridSpec(
        num_scalar_prefetch=4,
        grid=(nb + 1,),
        in_specs=[hbm, hbm, hbm, hbm],
        out_specs=hbm,
        scratch_shapes=[
            pltpu.VMEM((N_SLOTS, BLK * SUBLANES, LANES), F32),
            pltpu.VMEM((N_SLOTS, BLK * SUBLANES, LANES), F32),
            pltpu.VMEM((2, d, de), F32), pltpu.VMEM((2, d, de), F32), pltpu.VMEM((2, de, d), F32),
            pltpu.VMEM((d, de), BF16), pltpu.VMEM((d, de), BF16), pltpu.VMEM((de, d), BF16),
            pltpu.SemaphoreType.DMA((2, 3)), pltpu.SemaphoreType.DMA((N_SLOTS,)),
            pltpu.SemaphoreType.DMA((N_SLOTS,)), pltpu.SMEM((1,), I32),
        ],
    )
    return pl.pallas_call(
        functools.partial(_experts_body, layer=layer),
        grid_spec=grid_spec,
        out_shape=jax.ShapeDtypeStruct((Y_ROWS * SUBLANES, LANES), F32),
        compiler_params=pltpu.CompilerParams(
            dimension_semantics=("arbitrary",), vmem_limit_bytes=VMEM_LIMIT),
        name="experts",
    )(blk_e, n_used, nblk, inv, x, wg, wu, wd)


def _combine_body(x_ref, ya_ref, yb_ref, w_ref, g_ref, b_ref, o_ref):
    tc = o_ref.shape[0]
    w = w_ref[...]
    y = w[:, 0:1] * _load_rows(ya_ref, tc) + w[:, 1:2] * _load_rows(yb_ref, tc)
    o_ref[...] = _layer_norm(ALPHA * _load_rows(x_ref, tc) + y, g_ref[...], b_ref[...])


def _combine(x, y, w_cols, g, b, tok0, t):
    d = g.shape[1]
    tc = min(TC, t)
    blk0 = tok0 // tc
    plane = (1 << PLANE_BITS) // tc
    row_tiled = lambda first: pl.BlockSpec((tc * SUBLANES, LANES), lambda i: (first + i, 0))
    return pl.pallas_call(
        _combine_body,
        grid=(t // tc,),
        in_specs=[
            row_tiled(blk0), row_tiled(blk0), row_tiled(plane + blk0),
            pl.BlockSpec((tc, 2), lambda i: (blk0 + i, 0)),
            pl.BlockSpec((1, d), lambda i: (0, 0)),
            pl.BlockSpec((1, d), lambda i: (0, 0)),
        ],
        out_specs=pl.BlockSpec((tc, d), lambda i: (i, 0)),
        out_shape=jax.ShapeDtypeStruct((t, d), F32),
        compiler_params=pltpu.CompilerParams(
            dimension_semantics=("arbitrary",), vmem_limit_bytes=VMEM_LIMIT),
        name="combine",
    )(x, y, y, w_cols, g, b)


def _router_weight(w_group, w_expert):
    d = w_group.shape[0]
    wrt = jnp.zeros((ROUTE_ROWS, d), F32)
    wrt = wrt.at[0:N_EXPERTS].set(w_expert.T).at[GROUP_ROW0:GROUP_ROW0 + N_GROUPS].set(w_group.T)
    return wrt.astype(BF16)


def _moe_experts(x, tp, ts, routed_p, wrt, wg, wu, wd, layer):
    t_total = tp + ts
    assert PAD_SPAN <= t_total <= 1 << PLANE_BITS and tp % ts == 0
    ints_p, wts_p, cnt_p = routed_p
    ints_s, wts_s, cnt = _route(x, wrt, cnt_p, tp, ts)
    ints = jnp.concatenate([ints_p, ints_s], axis=1)
    wts = jnp.concatenate([wts_p, wts_s], axis=1)

    counts = cnt[:, 0].astype(I32)
    pcounts = (counts + BLK - 1) // BLK * BLK
    pend = jnp.cumsum(pcounts)
    pstart = pend - pcounts
    nb = (2 * t_total + N_EXPERTS * (BLK - 1) + BLK - 1) // BLK
    p_rows = nb * BLK
    n_used = (pend[-1] // BLK).astype(I32).reshape(1)
    blk_first = jnp.minimum(jnp.arange(nb, dtype=I32), n_used[0] - 1) * BLK
    blk_e = jnp.sum((pend[None, :] <= blk_first[:, None]).astype(I32), axis=1)
    nblk = pcounts // BLK

    pstart_f = jnp.broadcast_to((pstart + INV_ROW0).astype(F32)[:, None], (N_EXPERTS, LANES))
    dest_flat = _dest(ints, pstart_f).reshape(2 * t_total)

    inv = _invert(dest_flat, pstart + counts + INV_ROW0, p_rows)
    return _experts(blk_e, n_used, nblk, inv, x, wg, wu, wd, layer), wts.T


def kernel(x_prompt, x_sample, state_conv_a, state_conv_b, state_h, a_w_in, a_conv_w, a_w_out,
           b_w_in, b_conv_w, b_conv_b, b_gate_a_w, b_gate_a_b, b_gate_x_w, b_gate_x_b, b_lambda,
           b_w_out, ln1_g, ln1_b, ln2_g, ln2_b, moe_w_group, moe_w_expert, moe_w_gate, moe_w_up,
           moe_w_down):
    bsz, seq, d = x_prompt.shape
    n_s = x_sample.shape[0]
    row = lambda v: v.reshape(1, d)

    tp = bsz * seq
    win, wout = a_w_in[0], a_w_out[0]
    wrt = _router_weight(moe_w_group[0], moe_w_expert[0])
    x1, conv_a_p, *routed = _conv_a_prompt(x_prompt, win, a_conv_w[0], wout, row(ln1_g[0]),
                                           row(ln1_b[0]), wrt, n_s)
    sa = state_conv_a[0]
    x1, u_s = _conv_a_sample(x_sample.reshape(n_s, d), sa[:, 0], sa[:, 1], win, a_conv_w[0], wout,
                             row(ln1_g[0]), row(ln1_b[0]), x1)
    conv_a_s = jnp.stack([sa[:, 1], u_s], axis=1)

    y, w_cols = _moe_experts(x1, tp, n_s, routed, wrt, moe_w_gate, moe_w_up, moe_w_down, 0)
    g2, b2 = row(ln2_g[0]), row(ln2_b[0])
    xs = _combine(x1, y, w_cols, g2, b2, tp, n_s)

    win, wout = b_w_in[0], b_w_out[0]
    wcat = jnp.concatenate([b_gate_a_w[0], b_gate_x_w[0]], axis=-1).astype(BF16)
    args = (win, b_conv_w[0], row(b_conv_b[0]), wcat, row(b_gate_a_b[0]), row(b_gate_x_b[0]),
            row(b_lambda[0]), wout, row(ln1_g[1]), row(ln1_b[1]))
    wrt = _router_weight(moe_w_group[1], moe_w_expert[1])
    x1, conv_b_p, h_p, *routed = _rglru_prompt(x1, y, w_cols, g2, b2, bsz, seq, *args, wrt, n_s)
    sb = state_conv_b[0]
    x1, xr_s, h_s = _rglru_sample(xs, sb[:, 0], sb[:, 1], sb[:, 2], state_h[0], *args, x1)
    conv_b_s = jnp.stack([sb[:, 1], sb[:, 2], xr_s], axis=1)

    y, w_cols = _moe_experts(x1, tp, n_s, routed, wrt, moe_w_gate, moe_w_up, moe_w_down, 1)
    g2, b2 = row(ln2_g[1]), row(ln2_b[1])
    xp = _combine(x1, y, w_cols, g2, b2, 0, tp)
    xs = _combine(x1, y, w_cols, g2, b2, tp, n_s)

    return (xp.reshape(bsz, seq, d), xs.reshape(n_s, 1, d),
            conv_a_p[None], conv_a_s[None], conv_b_p[None], conv_b_s[None],
            h_p.reshape(1, bsz, d), h_s[None])
```

```python
import functools

import jax
import jax.numpy as jnp
from jax import lax
from jax.experimental import pallas as pl
from jax.experimental.pallas import tpu as pltpu

F32 = jnp.float32
BF16 = jnp.bfloat16
I32 = jnp.int32

DEPTH = 2
N_RG_BLOCKS = 8
RG_C = 8.0
N_GROUPS = 4
EXP_PER_GROUP = 8
N_EXPERTS = N_GROUPS * EXP_PER_GROUP
ALPHA = (2.0 * DEPTH) ** 0.25
LN_EPS = 1e-5

LANES = 128
SUBLANES = 8
VMEM_LIMIT = 56 * 1024 * 1024

TS_A = 512
TS_B = 512
TT_ROUTE = 1024
TC = 1024
BLK = 256
ROUTE_ROWS = 128
GROUP_ROW0 = N_EXPERTS
W_CHUNK = 512
RG_UNIT = 2
ROW_UNROLL = 8

PLANE_BITS = 15
TOKEN_MASK = (1 << PLANE_BITS) - 1
PAD_BASE = 2 << PLANE_BITS
N_SLOTS = 3
PRIME_BLOCKS = N_SLOTS
INV_ROW0 = PRIME_BLOCKS * BLK
PAD_SPAN = 4 * BLK
SPARE_ROW0 = PAD_BASE + PAD_SPAN
Y_ROWS = SPARE_ROW0 + PRIME_BLOCKS * BLK


def _dot(a, b):
    return jnp.dot(a, b, preferred_element_type=F32)


def _load_rows(ref, m, idx=()):
    return jnp.concatenate(
        [ref[idx + (pl.ds(s, m, stride=SUBLANES), slice(None))] for s in range(SUBLANES)], axis=1)


def _store_rows(ref, v, idx=()):
    m = v.shape[0]
    for s in range(SUBLANES):
        ref[idx + (pl.ds(s, m, stride=SUBLANES), slice(None))] = v[:, s * LANES:(s + 1) * LANES]


def _tile_of_row(ref, r):
    return ref.at[pl.ds(pl.multiple_of(r * SUBLANES, SUBLANES), SUBLANES)]


def _load_weight_bf16(w_hbm, w_bf, stage, sem):
    nch = w_hbm.shape[1] // W_CHUNK

    def chunk_copy(c):
        return pltpu.make_async_copy(w_hbm.at[:, pl.ds(c * W_CHUNK, W_CHUNK)],
                                     stage.at[c % 2], sem.at[c % 2])

    chunk_copy(0).start()
    for c in range(nch):
        if c + 1 < nch:
            chunk_copy(c + 1).start()
        chunk_copy(c).wait()
        w_bf[:, c * W_CHUNK:(c + 1) * W_CHUNK] = stage[c % 2].astype(BF16)


def _weight_scratch(k, *ns):
    return ([pltpu.VMEM((k, n), BF16) for n in ns]
            + [pltpu.VMEM((2, k, W_CHUNK), F32), pltpu.SemaphoreType.DMA((2,))])


def _layer_norm(r, g, b):
    mu = jnp.mean(r, axis=-1, keepdims=True)
    d = r - mu
    var = jnp.mean(d * d, axis=-1, keepdims=True)
    return d * lax.rsqrt(var + LN_EPS) * g + b


def _shift_rows(v, k, prev8):
    rolled = pltpu.roll(v, k, axis=0)
    rows8 = lax.broadcasted_iota(I32, (SUBLANES, v.shape[1]), 0)
    first = jnp.where(rows8 < k, pltpu.roll(prev8, k, axis=0), rolled[0:SUBLANES])
    return jnp.concatenate([first, rolled[SUBLANES:]], axis=0)


def _softplus(v):
    return jnp.maximum(v, 0.0) + jnp.log1p(jnp.exp(-jnp.abs(v)))


def _sigmoid(z):
    return 0.5 * jnp.tanh(0.5 * z) + 0.5


def _rglru_coeffs(xc, wcat_ref, gab, gxb, lam, blk0=0):
    blk = wcat_ref.shape[1]
    xcb = xc.astype(BF16)
    rs, is_ = [], []
    for n in range(xc.shape[1] // blk):
        o = _dot(xcb[:, n * blk:(n + 1) * blk], wcat_ref[blk0 + n])
        rs.append(o[:, :blk])
        is_.append(o[:, blk:])
    r = _sigmoid(jnp.concatenate(rs, axis=1) + gab)
    i = _sigmoid(jnp.concatenate(is_, axis=1) + gxb)
    neg_log_a = RG_C * r * _softplus(-lam)
    a = jnp.exp(-neg_log_a)
    v = jnp.tanh(neg_log_a) * (a * a + 1.0)
    mult = jnp.where(v > 0.0, v * lax.rsqrt(v), 0.0)
    return a, mult * (i * xc)


def _scan_rows(a, b, h0):
    m, d = a.shape
    groups = m // SUBLANES
    a = a.reshape(groups, SUBLANES, d)
    b = b.reshape(groups, SUBLANES, d)
    sub = lax.broadcasted_iota(I32, a.shape, 1)
    for k in (1, 2, 4):
        keep = sub >= k
        a_sh = jnp.where(keep, pltpu.roll(a, k, axis=1), 1.0)
        b_sh = jnp.where(keep, pltpu.roll(b, k, axis=1), 0.0)
        b = a * b_sh + b
        a = a * a_sh
    outs = []
    h = h0
    for g in range(groups):
        hg = a[g] * h + b[g]
        outs.append(hg)
        h = hg[SUBLANES - 1:SUBLANES]
    return jnp.concatenate(outs, axis=0), h


def _conv_a_prompt_body(x_ref, win_hbm, cw_ref, wout_hbm, g_ref, b_ref, wrt_ref, tri_ref,
                        o_ref, buf_ref, ints_ref, wts_ref, cnt_ref,
                        carry, counts, win_ref, wout_ref, stage, wsem):
    s = pl.program_id(1)

    @pl.when(jnp.logical_and(pl.program_id(0) == 0, s == 0))
    def _():
        _load_weight_bf16(win_hbm, win_ref, stage, wsem)
        _load_weight_bf16(wout_hbm, wout_ref, stage, wsem)
        counts[...] = jnp.zeros_like(counts)

    @pl.when(s == 0)
    def _():
        carry[...] = jnp.zeros_like(carry)

    x = x_ref[0]
    d = x.shape[1]
    bcx = _dot(x.astype(BF16), win_ref[...])
    gb, gc, xh = bcx[:, :d], bcx[:, d:2 * d], bcx[:, 2 * d:]
    u = gc * xh
    prev = carry[...]
    cw = cw_ref[...]
    conv = (cw[0:1] * _shift_rows(u, 2, prev) + cw[1:2] * _shift_rows(u, 1, prev)
            + cw[2:3] * u)
    y = _dot((gb * conv).astype(BF16), wout_ref[...])
    x1 = _layer_norm(ALPHA * x + y, g_ref[...], b_ref[...])
    _store_rows(o_ref, x1)
    _route_tile(x1.astype(BF16), wrt_ref, tri_ref, counts, ints_ref, wts_ref, cnt_ref)
    ts = u.shape[0]
    carry[...] = u[ts - SUBLANES:ts]

    @pl.when(s == pl.num_programs(1) - 1)
    def _():
        buf_ref[0] = u[ts - 2:ts]


def _prefix_matrix(tt):
    return (jnp.arange(tt)[:, None] < jnp.arange(tt)[None, :]).astype(BF16)


def _conv_a_prompt(x, win, cw, wout, g, b, wrt, n_extra):
    bsz, seq, d = x.shape
    ts = min(TS_A, seq)
    grid = (bsz, seq // ts)
    nj = seq // ts
    tp = bsz * seq
    const2 = lambda i, j: (0, 0)
    return pl.pallas_call(
        _conv_a_prompt_body,
        grid=grid,
        in_specs=[
            pl.BlockSpec((1, ts, d), lambda i, j: (i, j, 0)),
            pl.BlockSpec(memory_space=pl.ANY),
            pl.BlockSpec((3, d), const2),
            pl.BlockSpec(memory_space=pl.ANY),
            pl.BlockSpec((1, d), const2),
            pl.BlockSpec((1, d), const2),
            pl.BlockSpec((ROUTE_ROWS, d), const2),
            pl.BlockSpec((ts, ts), const2),
        ],
        out_specs=[
            pl.BlockSpec((ts * SUBLANES, LANES), lambda i, j: (i * nj + j, 0)),
            pl.BlockSpec((1, 2, d), lambda i, j: (i, 0, 0)),
            pl.BlockSpec((4, ts), lambda i, j: (0, i * nj + j)),
            pl.BlockSpec((2, ts), lambda i, j: (0, i * nj + j)),
            pl.BlockSpec((N_EXPERTS, LANES), const2),
        ],
        out_shape=[
            jax.ShapeDtypeStruct(((tp + n_extra) * SUBLANES, LANES), F32),
            jax.ShapeDtypeStruct((bsz, 2, d), F32),
            jax.ShapeDtypeStruct((4, tp), I32),
            jax.ShapeDtypeStruct((2, tp), F32),
            jax.ShapeDtypeStruct((N_EXPERTS, LANES), F32),
        ],
        scratch_shapes=([pltpu.VMEM((SUBLANES, d), F32), pltpu.VMEM((N_EXPERTS, LANES), F32)]
                        + _weight_scratch(d, 3 * d, d)),
        compiler_params=pltpu.CompilerParams(
            dimension_semantics=("arbitrary", "arbitrary"), vmem_limit_bytes=VMEM_LIMIT),
        name="conv_a_prompt",
    )(x, win, cw, wout, g, b, wrt, _prefix_matrix(ts))


def _conv_a_sample_body(x_ref, s0_ref, s1_ref, win_hbm, cw_ref, wout_hbm, g_ref, b_ref, joint_ref,
                        o_ref, u_ref, win_ref, wout_ref, stage, wsem):
    del joint_ref
    _load_weight_bf16(win_hbm, win_ref, stage, wsem)
    _load_weight_bf16(wout_hbm, wout_ref, stage, wsem)
    x = x_ref[...]
    d = x.shape[1]
    bcx = _dot(x.astype(BF16), win_ref[...])
    gb, gc, xh = bcx[:, :d], bcx[:, d:2 * d], bcx[:, 2 * d:]
    u = gc * xh
    cw = cw_ref[...]
    conv = cw[0:1] * s0_ref[...] + cw[1:2] * s1_ref[...] + cw[2:3] * u
    y = _dot((gb * conv).astype(BF16), wout_ref[...])
    _store_rows(o_ref, _layer_norm(ALPHA * x + y, g_ref[...], b_ref[...]))
    u_ref[...] = u


def _whole(a):
    return pl.BlockSpec(a.shape, lambda i: (0,) * a.ndim)


def _conv_a_sample(x, s0, s1, win, cw, wout, g, b, joint):
    n, d = x.shape
    hbm = pl.BlockSpec(memory_space=pl.ANY)
    first_block = joint.shape[0] // (n * SUBLANES) - 1
    return pl.pallas_call(
        _conv_a_sample_body,
        grid=(1,),
        in_specs=[_whole(x), _whole(s0), _whole(s1), hbm, _whole(cw), hbm, _whole(g), _whole(b),
                  hbm],
        out_specs=[pl.BlockSpec((n * SUBLANES, LANES), lambda i: (first_block, 0)),
                   pl.BlockSpec((n, d), lambda i: (0, 0))],
        out_shape=[jax.ShapeDtypeStruct(joint.shape, F32), jax.ShapeDtypeStruct((n, d), F32)],
        scratch_shapes=_weight_scratch(d, 3 * d, d),
        input_output_aliases={8: 0},
        compiler_params=pltpu.CompilerParams(
            dimension_semantics=("arbitrary",), vmem_limit_bytes=VMEM_LIMIT),
        name="conv_a_sample",
    )(x, s0, s1, win, cw, wout, g, b, joint)


def _rglru_prompt_body(x1_ref, ya_ref, yb_ref, w_ref, g2_ref, b2_ref, win_hbm, cw_ref, cb_ref,
                       wcat_ref, gab_ref, gxb_ref, lam_ref, wout_hbm, g_ref, b_ref, wrt_ref, tri_ref,
                       o_ref, buf_ref, hl_ref, ints_ref, wts_ref, cnt_ref,
                       xcarry, hcarry, counts, x_scr, gx_scr, win_ref, wout_ref, stage, wsem,
                       *, nj):
    i = pl.program_id(0)
    t = i - 1
    s = t % nj

    def layer_input():
        ts = x_scr.shape[1]
        w = w_ref[...]
        y = w[:, 0:1] * _load_rows(ya_ref, ts) + w[:, 1:2] * _load_rows(yb_ref, ts)
        return _layer_norm(ALPHA * _load_rows(x1_ref, ts) + y, g2_ref[...], b2_ref[...])

    @pl.when(i == 0)
    def _():
        _load_weight_bf16(win_hbm, win_ref, stage, wsem)
        _load_weight_bf16(wout_hbm, wout_ref, stage, wsem)
        x0 = layer_input()
        x_scr[0] = x0
        gx_scr[0] = _dot(x0.astype(BF16), win_ref[...])
        counts[...] = jnp.zeros_like(counts)

    @pl.when(jnp.logical_and(i > 0, s == 0))
    def _():
        xcarry[...] = jnp.zeros_like(xcarry)
        hcarry[...] = jnp.zeros_like(hcarry)

    for par in (t % 2,):
        @pl.when(i > 0)
        def _(par=par):
            x_next = layer_input()
            x_scr[1 - par] = x_next
            xn = x_next.astype(BF16)
            x = x_scr[par]
            ts, d = x.shape
            blk = d // N_RG_BLOCKS
            zs = []
            for u in range(N_RG_BLOCKS // RG_UNIT):
                c0, c1 = u * RG_UNIT * blk, (u + 1) * RG_UNIT * blk
                for half in (0, d):
                    gx_scr[1 - par, :, half + c0:half + c1] = _dot(
                        xn, win_ref[:, half + c0:half + c1])
                gate, xr = gx_scr[par, :, c0:c1], gx_scr[par, :, d + c0:d + c1]
                prev = xcarry[:, c0:c1]
                cw = cw_ref[:, c0:c1]
                xc = (cw[0:1] * _shift_rows(xr, 3, prev) + cw[1:2] * _shift_rows(xr, 2, prev)
                      + cw[2:3] * _shift_rows(xr, 1, prev) + cw[3:4] * xr) + cb_ref[:, c0:c1]
                a, bt = _rglru_coeffs(xc, wcat_ref, gab_ref[:, c0:c1], gxb_ref[:, c0:c1],
                                      lam_ref[:, c0:c1], u * RG_UNIT)
                hs, hlast = _scan_rows(a, bt, hcarry[0:1, c0:c1])
                zs.append((jax.nn.gelu(gate, approximate=True) * hs).astype(BF16))
                xcarry[:, c0:c1] = xr[ts - SUBLANES:ts]
                hcarry[:, c0:c1] = jnp.broadcast_to(hlast, (SUBLANES, c1 - c0))

            y = _dot(jnp.concatenate(zs, axis=1), wout_ref[...])
            x1 = _layer_norm(ALPHA * x + y, g_ref[...], b_ref[...])
            _store_rows(o_ref, x1)
            _route_tile(x1.astype(BF16), wrt_ref, tri_ref, counts, ints_ref, wts_ref, cnt_ref)

    @pl.when(jnp.logical_and(i > 0, s == nj - 1))
    def _():
        buf_ref[0] = xcarry[SUBLANES - 3:SUBLANES]
        hl_ref[0] = hcarry[0:1]


def _rglru_prompt(x_prev, y, w_cols, g2, b2, bsz, seq,
                  win, cw, cb, wcat, gab, gxb, lam, wout, g, b, wrt, n_extra):
    d = g.shape[1]
    ts = min(TS_B, seq)
    nj = seq // ts
    n = bsz * nj
    tp = bsz * seq
    blk = d // N_RG_BLOCKS
    plane = (1 << PLANE_BITS) // ts
    const2 = lambda i: (0, 0)
    nxt = lambda i: jnp.minimum(i, n - 1)
    cur = lambda i: jnp.maximum(i - 1, 0)
    return pl.pallas_call(
        functools.partial(_rglru_prompt_body, nj=nj),
        grid=(n + 1,),
        in_specs=[
            pl.BlockSpec((ts * SUBLANES, LANES), lambda i: (nxt(i), 0)),
            pl.BlockSpec((ts * SUBLANES, LANES), lambda i: (nxt(i), 0)),
            pl.BlockSpec((ts * SUBLANES, LANES), lambda i: (plane + nxt(i), 0)),
            pl.BlockSpec((ts, 2), lambda i: (nxt(i), 0)),
            pl.BlockSpec((1, d), const2),
            pl.BlockSpec((1, d), const2),
            pl.BlockSpec(memory_space=pl.ANY),
            pl.BlockSpec((4, d), const2),
            pl.BlockSpec((1, d), const2),
            pl.BlockSpec((N_RG_BLOCKS, blk, 2 * blk), lambda i: (0, 0, 0)),
            pl.BlockSpec((1, d), const2),
            pl.BlockSpec((1, d), const2),
            pl.BlockSpec((1, d), const2),
            pl.BlockSpec(memory_space=pl.ANY),
            pl.BlockSpec((1, d), const2),
            pl.BlockSpec((1, d), const2),
            pl.BlockSpec((ROUTE_ROWS, d), const2),
            pl.BlockSpec((ts, ts), const2),
        ],
        out_specs=[
            pl.BlockSpec((ts * SUBLANES, LANES), lambda i: (cur(i), 0)),
            pl.BlockSpec((1, 3, d), lambda i: (cur(i) // nj, 0, 0)),
            pl.BlockSpec((1, 1, d), lambda i: (cur(i) // nj, 0, 0)),
            pl.BlockSpec((4, ts), lambda i: (0, cur(i))),
            pl.BlockSpec((2, ts), lambda i: (0, cur(i))),
            pl.BlockSpec((N_EXPERTS, LANES), const2),
        ],
        out_shape=[
            jax.ShapeDtypeStruct(((tp + n_extra) * SUBLANES, LANES), F32),
            jax.ShapeDtypeStruct((bsz, 3, d), F32),
            jax.ShapeDtypeStruct((bsz, 1, d), F32),
            jax.ShapeDtypeStruct((4, tp), I32),
            jax.ShapeDtypeStruct((2, tp), F32),
            jax.ShapeDtypeStruct((N_EXPERTS, LANES), F32),
        ],
        scratch_shapes=([pltpu.VMEM((SUBLANES, d), F32), pltpu.VMEM((SUBLANES, d), F32),
                         pltpu.VMEM((N_EXPERTS, LANES), F32), pltpu.VMEM((2, ts, d), F32),
                         pltpu.VMEM((2, ts, 2 * d), F32)]
                        + _weight_scratch(d, 2 * d, d)),
        compiler_params=pltpu.CompilerParams(
            dimension_semantics=("arbitrary",), vmem_limit_bytes=VMEM_LIMIT),
        name="rglru_prompt",
    )(x_prev, y, y, w_cols, g2, b2, win, cw, cb, wcat, gab, gxb, lam, wout, g, b, wrt,
      _prefix_matrix(ts))


def _rglru_sample_body(x_ref, s0_ref, s1_ref, s2_ref, h0_ref, win_hbm, cw_ref, cb_ref, wcat_ref,
                       gab_ref, gxb_ref, lam_ref, wout_hbm, g_ref, b_ref, joint_ref,
                       o_ref, xr_ref, h_ref, win_ref, wout_ref, stage, wsem):
    del joint_ref
    _load_weight_bf16(win_hbm, win_ref, stage, wsem)
    _load_weight_bf16(wout_hbm, wout_ref, stage, wsem)
    x = x_ref[...]
    d = x.shape[1]
    gx = _dot(x.astype(BF16), win_ref[...])
    gate, xr = gx[:, :d], gx[:, d:]
    cw = cw_ref[...]
    xc = (cw[0:1] * s0_ref[...] + cw[1:2] * s1_ref[...] + cw[2:3] * s2_ref[...]
          + cw[3:4] * xr) + cb_ref[...]
    a, bt = _rglru_coeffs(xc, wcat_ref, gab_ref[...], gxb_ref[...], lam_ref[...])
    h = a * h0_ref[...] + bt
    y = _dot((jax.nn.gelu(gate, approximate=True) * h).astype(BF16), wout_ref[...])
    _store_rows(o_ref, _layer_norm(ALPHA * x + y, g_ref[...], b_ref[...]))
    xr_ref[...] = xr
    h_ref[...] = h


def _rglru_sample(x, s0, s1, s2, h0, win, cw, cb, wcat, gab, gxb, lam, wout, g, b, joint):
    n, d = x.shape
    hbm = pl.BlockSpec(memory_space=pl.ANY)
    first_block = joint.shape[0] // (n * SUBLANES) - 1
    vec = pl.BlockSpec((n, d), lambda i: (0, 0))
    return pl.pallas_call(
        _rglru_sample_body,
        grid=(1,),
        in_specs=[_whole(x), _whole(s0), _whole(s1), _whole(s2), _whole(h0), hbm, _whole(cw),
                  _whole(cb), _whole(wcat), _whole(gab), _whole(gxb), _whole(lam), hbm,
                  _whole(g), _whole(b), hbm],
        out_specs=[pl.BlockSpec((n * SUBLANES, LANES), lambda i: (first_block, 0)), vec, vec],
        out_shape=[jax.ShapeDtypeStruct(joint.shape, F32),
                   jax.ShapeDtypeStruct((n, d), F32), jax.ShapeDtypeStruct((n, d), F32)],
        scratch_shapes=_weight_scratch(d, 2 * d, d),
        input_output_aliases={15: 0},
        compiler_params=pltpu.CompilerParams(
            dimension_semantics=("arbitrary",), vmem_limit_bytes=VMEM_LIMIT),
        name="rglru_sample",
    )(x, s0, s1, s2, h0, win, cw, cb, wcat, gab, gxb, lam, wout, g, b, joint)


def _first_argmax(v, rows):
    m = jnp.max(v, axis=0, keepdims=True)
    idx = jnp.min(jnp.where(v == m, rows, v.shape[0]), axis=0, keepdims=True)
    return m, idx


def _route_body(x_ref, wrt_ref, tri_ref, cin_ref, ints_ref, wts_ref, cnt_ref, carry):
    @pl.when(pl.program_id(0) == 0)
    def _():
        carry[...] = cin_ref[...]

    tt = x_ref.shape[0] // SUBLANES
    _route_tile(_load_rows(x_ref, tt).astype(BF16), wrt_ref, tri_ref, carry, ints_ref, wts_ref,
                cnt_ref)


def _route_tile(xb, wrt_ref, tri_ref, carry, ints_ref, wts_ref, cnt_ref):
    tt = xb.shape[0]
    lt = lax.dot_general(wrt_ref[...], xb, (((1,), (1,)), ((), ())), preferred_element_type=F32)
    rows8 = lax.broadcasted_iota(I32, (SUBLANES, tt), 0)
    neg_inf = jnp.float32(-jnp.inf)

    gl = jnp.where(rows8 < N_GROUPS, lt[GROUP_ROW0:GROUP_ROW0 + SUBLANES], neg_inf)
    gmax, gidx = _first_argmax(gl, rows8)
    gw = 1.0 / jnp.sum(jnp.exp(gl - gmax), axis=0, keepdims=True)

    el = lt[0:EXP_PER_GROUP]
    for g in range(1, N_GROUPS):
        el = jnp.where(gidx == g, lt[g * EXP_PER_GROUP:(g + 1) * EXP_PER_GROUP], el)
    emax, i1 = _first_argmax(el, rows8)
    el2 = jnp.where(rows8 == i1, neg_inf, el)
    m2, i2 = _first_argmax(el2, rows8)
    psum = jnp.sum(jnp.exp(el - emax), axis=0, keepdims=True)
    ep1 = 1.0 / psum
    ep2 = jnp.exp(m2 - emax) / psum
    tot = ep1 + ep2
    wa = gw * (ep1 / tot)
    wb = gw * (ep2 / tot)
    ea = gidx * EXP_PER_GROUP + i1
    eb = gidx * EXP_PER_GROUP + i2

    rows_e = lax.broadcasted_iota(I32, (N_EXPERTS, tt), 0)
    oha = rows_e == ea
    ohb = rows_e == eb
    oh = jnp.where(oha | ohb, 1.0, 0.0)
    base = carry[...][:, 0:1]
    excl = _dot(oh.astype(BF16), tri_ref[...]) + base
    ra = jnp.sum(jnp.where(oha, excl, 0.0), axis=0, keepdims=True)
    rb = jnp.sum(jnp.where(ohb, excl, 0.0), axis=0, keepdims=True)
    new = carry[...] + jnp.sum(oh, axis=1, keepdims=True)
    carry[...] = new
    cnt_ref[...] = new

    ints_ref[0:1, :] = ea
    ints_ref[1:2, :] = eb
    ints_ref[2:3, :] = ra.astype(I32)
    ints_ref[3:4, :] = rb.astype(I32)
    wts_ref[0:1, :] = wa
    wts_ref[1:2, :] = wb


def _route(x, wrt, cin, tok0, t):
    d = wrt.shape[1]
    tt = min(TT_ROUTE, t)
    blk0 = tok0 // tt
    tri = _prefix_matrix(tt)
    return pl.pallas_call(
        _route_body,
        grid=(t // tt,),
        in_specs=[
            pl.BlockSpec((tt * SUBLANES, LANES), lambda i: (blk0 + i, 0)),
            pl.BlockSpec((ROUTE_ROWS, d), lambda i: (0, 0)),
            pl.BlockSpec((tt, tt), lambda i: (0, 0)),
            pl.BlockSpec((N_EXPERTS, LANES), lambda i: (0, 0)),
        ],
        out_specs=[
            pl.BlockSpec((4, tt), lambda i: (0, i)),
            pl.BlockSpec((2, tt), lambda i: (0, i)),
            pl.BlockSpec((N_EXPERTS, LANES), lambda i: (0, 0)),
        ],
        out_shape=[
            jax.ShapeDtypeStruct((4, t), I32),
            jax.ShapeDtypeStruct((2, t), F32),
            jax.ShapeDtypeStruct((N_EXPERTS, LANES), F32),
        ],
        scratch_shapes=[pltpu.VMEM((N_EXPERTS, LANES), F32)],
        compiler_params=pltpu.CompilerParams(
            dimension_semantics=("arbitrary",), vmem_limit_bytes=VMEM_LIMIT),
        name="route",
    )(x, wrt, tri, cin)


def _dest_body(ints_ref, pst_ref, dest_ref):
    ints = ints_ref[...]
    tt = ints.shape[1]
    rows_e = lax.broadcasted_iota(I32, (N_EXPERTS, tt), 0)
    pst = pst_ref[...][:, 0:1]
    for k in range(2):
        start = jnp.sum(jnp.where(rows_e == ints[k:k + 1], pst, 0.0), axis=0, keepdims=True)
        dest_ref[k:k + 1, :] = start.astype(I32) + ints[2 + k:3 + k]


def _dest(ints, pstart_f):
    t = ints.shape[1]
    return pl.pallas_call(
        _dest_body,
        out_shape=jax.ShapeDtypeStruct((2, t), I32),
        compiler_params=pltpu.CompilerParams(vmem_limit_bytes=VMEM_LIMIT),
        name="dest",
    )(ints, pstart_f)


def _invert_body(dest_ref, padpos_ref, inv_ref, *, t_total):
    def prime_body(i, c):
        inv_ref[i] = SPARE_ROW0 + i
        return c

    lax.fori_loop(0, INV_ROW0, prime_body, 0, unroll=8)

    def pad_body(e, c):
        q0 = padpos_ref[e]
        for r in range(BLK):
            inv_ref[q0 + r] = PAD_BASE + ((q0 + r) & (PAD_SPAN - 1))
        return c

    lax.fori_loop(0, N_EXPERTS, pad_body, 0)

    def tok_body(j, c):
        toks = [j * ROW_UNROLL + u for u in range(ROW_UNROLL)]
        rows = [[dest_ref[k * t_total + t] for k in range(2)] for t in toks]
        for t, qs in zip(toks, rows):
            for k in range(2):
                inv_ref[qs[k]] = t + (k << PLANE_BITS)
        return c

    lax.fori_loop(0, t_total // ROW_UNROLL, tok_body, 0)


def _invert(dest_flat, pad_pos, p_rows):
    t_total = dest_flat.shape[0] // 2
    grid_spec = pltpu.PrefetchScalarGridSpec(
        num_scalar_prefetch=2,
        grid=(1,),
        in_specs=[],
        out_specs=pl.BlockSpec(memory_space=pltpu.SMEM),
    )
    return pl.pallas_call(
        functools.partial(_invert_body, t_total=t_total),
        grid_spec=grid_spec,
        out_shape=jax.ShapeDtypeStruct((INV_ROW0 + p_rows + BLK,), I32),
        compiler_params=pltpu.CompilerParams(dimension_semantics=("arbitrary",)),
        name="invert",
    )(dest_flat, pad_pos)


def _experts_body(be_ref, nu_ref, nblk_ref, inv_ref, x_hbm, wg_hbm, wu_hbm, wd_hbm, y_hbm,
                  xbuf, obuf, sg, su, sd, wg_ref, wu_ref, wd_ref, wsem, gsem, ssem, slot_ref,
                  *, layer):
    b = pl.program_id(0)
    nu = nu_ref[0]
    e = be_ref[jnp.minimum(b, nu - 1)]

    def gather(blk, slot, fn):
        vs = [inv_ref[(blk + PRIME_BLOCKS) * BLK + r] for r in range(BLK)]
        for r, v in enumerate(vs):
            fn(pltpu.make_async_copy(_tile_of_row(x_hbm, v & TOKEN_MASK),
                                     _tile_of_row(xbuf.at[slot], r), gsem.at[slot]), 0)

    def scatter(blk, slot, fn):
        vs = [inv_ref[(blk + PRIME_BLOCKS) * BLK + r] for r in range(BLK)]
        for r, v in enumerate(vs):
            fn(pltpu.make_async_copy(_tile_of_row(obuf.at[slot], r),
                                     _tile_of_row(y_hbm, v), ssem.at[slot]), 1)

    def start(cp, priority):
        cp.start(priority=priority)

    def wait(cp, priority):
        cp.wait()

    def fetch(ex, slot):
        return (pltpu.make_async_copy(wg_hbm.at[layer, ex], sg.at[slot], wsem.at[slot, 0]),
                pltpu.make_async_copy(wu_hbm.at[layer, ex], su.at[slot], wsem.at[slot, 1]),
                pltpu.make_async_copy(wd_hbm.at[layer, ex], sd.at[slot], wsem.at[slot, 2]))

    @pl.when(b == 0)
    def _():
        slot_ref[0] = 0
        for cp in fetch(e, 0):
            cp.start()
        obuf[...] = jnp.zeros_like(obuf)
        scatter(-3, 0, start)
        scatter(-2, 1, start)
        gather(0, 0, start)
        gather(jnp.minimum(1, nu - 1), 1, start)

    first_of_expert = jnp.logical_or(b == 0, e != be_ref[jnp.maximum(b - 1, 0)])

    @pl.when(jnp.logical_and(b < nu, first_of_expert))
    def _():
        slot = slot_ref[0]
        nxt = b + nblk_ref[e]

        @pl.when(nxt < nu)
        def _():
            for cp in fetch(be_ref[nxt], 1 - slot):
                cp.start()

        for cp in fetch(e, slot):
            cp.wait()
        wg_ref[...] = sg[slot].astype(BF16)
        wu_ref[...] = su[slot].astype(BF16)
        wd_ref[...] = sd[slot].astype(BF16)
        slot_ref[0] = 1 - slot

    for slot in range(N_SLOTS):
        prev, nxt = (slot - 1) % N_SLOTS, (slot + 1) % N_SLOTS

        @pl.when(jnp.logical_and(b < nu, b % N_SLOTS == slot))
        def _(slot=slot, prev=prev):
            gather(b, slot, wait)
            scatter(b - 3, slot, wait)
            xb = _load_rows(xbuf, BLK, (slot,)).astype(BF16)
            scatter(b - 1, prev, start)
            gather(jnp.minimum(b + 2, nu - 1), prev, start)
            h = jax.nn.silu(_dot(xb, wg_ref[...])) * _dot(xb, wu_ref[...])
            o = _dot(h.astype(BF16), wd_ref[...])
            _store_rows(obuf, o, (slot,))

        @pl.when(jnp.logical_and(b == nu, b % N_SLOTS == slot))
        def _(slot=slot, prev=prev, nxt=nxt):
            scatter(b - 1, prev, start)
            scatter(b - 3, slot, wait)
            scatter(b - 2, nxt, wait)
            scatter(b - 1, prev, wait)
            gather(nu - 1, slot, wait)
            gather(nu - 1, nxt, wait)


def _experts(blk_e, n_used, nblk, inv, x, wg, wu, wd, layer):
    d, de = wg.shape[2], wg.shape[3]
    nb = blk_e.shape[0]
    hbm = pl.BlockSpec(memory_space=pl.ANY)
    grid_spec = pltpu.PrefetchScalarGridSpec(
        num_scalar_prefetch=4,
        grid=(nb + 1,),
        in_specs=[hbm, hbm, hbm, hbm],
        out_specs=hbm,
        scratch_shapes=[
            pltpu.VMEM((N_SLOTS, BLK * SUBLANES, LANES), F32),
            pltpu.VMEM((N_SLOTS, BLK * SUBLANES, LANES), F32),
            pltpu.VMEM((2, d, de), F32), pltpu.VMEM((2, d, de), F32), pltpu.VMEM((2, de, d), F32),
            pltpu.VMEM((d, de), BF16), pltpu.VMEM((d, de), BF16), pltpu.VMEM((de, d), BF16),
            pltpu.SemaphoreType.DMA((2, 3)), pltpu.SemaphoreType.DMA((N_SLOTS,)),
            pltpu.SemaphoreType.DMA((N_SLOTS,)), pltpu.SMEM((1,), I32),
        ],
    )
    return pl.pallas_call(
        functools.partial(_experts_body, layer=layer),
        grid_spec=grid_spec,
        out_shape=jax.ShapeDtypeStruct((Y_ROWS * SUBLANES, LANES), F32),
        compiler_params=pltpu.CompilerParams(
            dimension_semantics=("arbitrary",), vmem_limit_bytes=VMEM_LIMIT),
        name="experts",
    )(blk_e, n_used, nblk, inv, x, wg, wu, wd)


def _combine_body(x_ref, ya_ref, yb_ref, w_ref, g_ref, b_ref, o_ref):
    tc = o_ref.shape[0]
    w = w_ref[...]
    y = w[:, 0:1] * _load_rows(ya_ref, tc) + w[:, 1:2] * _load_rows(yb_ref, tc)
    o_ref[...] = _layer_norm(ALPHA * _load_rows(x_ref, tc) + y, g_ref[...], b_ref[...])


def _combine(x, y, w_cols, g, b, tok0, t):
    d = g.shape[1]
    tc = min(TC, t)
    blk0 = tok0 // tc
    plane = (1 << PLANE_BITS) // tc
    row_tiled = lambda first: pl.BlockSpec((tc * SUBLANES, LANES), lambda i: (first + i, 0))
    return pl.pallas_call(
        _combine_body,
        grid=(t // tc,),
        in_specs=[
            row_tiled(blk0), row_tiled(blk0), row_tiled(plane + blk0),
            pl.BlockSpec((tc, 2), lambda i: (blk0 + i, 0)),
            pl.BlockSpec((1, d), lambda i: (0, 0)),
            pl.BlockSpec((1, d), lambda i: (0, 0)),
        ],
        out_specs=pl.BlockSpec((tc, d), lambda i: (i, 0)),
        out_shape=jax.ShapeDtypeStruct((t, d), F32),
        compiler_params=pltpu.CompilerParams(
            dimension_semantics=("arbitrary",), vmem_limit_bytes=VMEM_LIMIT),
        name="combine",
    )(x, y, y, w_cols, g, b)


def _router_weight(w_group, w_expert):
    d = w_group.shape[0]
    wrt = jnp.zeros((ROUTE_ROWS, d), F32)
    wrt = wrt.at[0:N_EXPERTS].set(w_expert.T).at[GROUP_ROW0:GROUP_ROW0 + N_GROUPS].set(w_group.T)
    return wrt.astype(BF16)


def _moe_experts(x, tp, ts, routed_p, wrt, wg, wu, wd, layer):
    t_total = tp + ts
    assert PAD_SPAN <= t_total <= 1 << PLANE_BITS and tp % ts == 0
    ints_p, wts_p, cnt_p = routed_p
    ints_s, wts_s, cnt = _route(x, wrt, cnt_p, tp, ts)
    ints = jnp.concatenate([ints_p, ints_s], axis=1)
    wts = jnp.concatenate([wts_p, wts_s], axis=1)

    counts = cnt[:, 0].astype(I32)
    pcounts = (counts + BLK - 1) // BLK * BLK
    pend = jnp.cumsum(pcounts)
    pstart = pend - pcounts
    nb = (2 * t_total + N_EXPERTS * (BLK - 1) + BLK - 1) // BLK
    p_rows = nb * BLK
    n_used = (pend[-1] // BLK).astype(I32).reshape(1)
    blk_first = jnp.minimum(jnp.arange(nb, dtype=I32), n_used[0] - 1) * BLK
    blk_e = jnp.sum((pend[None, :] <= blk_first[:, None]).astype(I32), axis=1)
    nblk = pcounts // BLK

    pstart_f = jnp.broadcast_to((pstart + INV_ROW0).astype(F32)[:, None], (N_EXPERTS, LANES))
    dest_flat = _dest(ints, pstart_f).reshape(2 * t_total)

    inv = _invert(dest_flat, pstart + counts + INV_ROW0, p_rows)
    return _experts(blk_e, n_used, nblk, inv, x, wg, wu, wd, layer), wts.T


def kernel(x_prompt, x_sample, state_conv_a, state_conv_b, state_h, a_w_in, a_conv_w, a_w_out,
           b_w_in, b_conv_w, b_conv_b, b_gate_a_w, b_gate_a_b, b_gate_x_w, b_gate_x_b, b_lambda,
           b_w_out, ln1_g, ln1_b, ln2_g, ln2_b, moe_w_group, moe_w_expert, moe_w_gate, moe_w_up,
           moe_w_down):
    bsz, seq, d = x_prompt.shape
    n_s = x_sample.shape[0]
    row = lambda v: v.reshape(1, d)

    tp = bsz * seq
    win, wout = a_w_in[0], a_w_out[0]
    wrt = _router_weight(moe_w_group[0], moe_w_expert[0])
    x1, conv_a_p, *routed = _conv_a_prompt(x_prompt, win, a_conv_w[0], wout, row(ln1_g[0]),
                                           row(ln1_b[0]), wrt, n_s)
    sa = state_conv_a[0]
    x1, u_s = _conv_a_sample(x_sample.reshape(n_s, d), sa[:, 0], sa[:, 1], win, a_conv_w[0], wout,
                             row(ln1_g[0]), row(ln1_b[0]), x1)
    conv_a_s = jnp.stack([sa[:, 1], u_s], axis=1)

    y, w_cols = _moe_experts(x1, tp, n_s, routed, wrt, moe_w_gate, moe_w_up, moe_w_down, 0)
    g2, b2 = row(ln2_g[0]), row(ln2_b[0])
    xs = _combine(x1, y, w_cols, g2, b2, tp, n_s)

    win, wout = b_w_in[0], b_w_out[0]
    wcat = jnp.concatenate([b_gate_a_w[0], b_gate_x_w[0]], axis=-1).astype(BF16)
    args = (win, b_conv_w[0], row(b_conv_b[0]), wcat, row(b_gate_a_b[0]), row(b_gate_x_b[0]),
            row(b_lambda[0]), wout, row(ln1_g[1]), row(ln1_b[1]))
    wrt = _router_weight(moe_w_group[1], moe_w_expert[1])
    x1, conv_b_p, h_p, *routed = _rglru_prompt(x1, y, w_cols, g2, b2, bsz, seq, *args, wrt, n_s)
    sb = state_conv_b[0]
    x1, xr_s, h_s = _rglru_sample(xs, sb[:, 0], sb[:, 1], sb[:, 2], state_h[0], *args, x1)
    conv_b_s = jnp.stack([sb[:, 1], sb[:, 2], xr_s], axis=1)

    y, w_cols = _moe_experts(x1, tp, n_s, routed, wrt, moe_w_gate, moe_w_up, moe_w_down, 1)
    g2, b2 = row(ln2_g[1]), row(ln2_b[1])
    xp = _combine(x1, y, w_cols, g2, b2, 0, tp)
    xs = _combine(x1, y, w_cols, g2, b2, tp, n_s)

    return (xp.reshape(bsz, seq, d), xs.reshape(n_s, 1, d),
            conv_a_p[None], conv_a_s[None], conv_b_p[None], conv_b_s[None],
            h_p.reshape(1, bsz, d), h_s[None])
```

```python
import functools

import jax
import jax.numpy as jnp
from jax import lax
from jax.experimental import pallas as pl
from jax.experimental.pallas import tpu as pltpu

F32 = jnp.float32
BF16 = jnp.bfloat16
I32 = jnp.int32

DEPTH = 2
N_RG_BLOCKS = 8
RG_C = 8.0
N_GROUPS = 4
EXP_PER_GROUP = 8
N_EXPERTS = N_GROUPS * EXP_PER_GROUP
ALPHA = (2.0 * DEPTH) ** 0.25
LN_EPS = 1e-5

LANES = 128
SUBLANES = 8
VMEM_LIMIT = 56 * 1024 * 1024

TS_A = 512
TS_B = 512
TT_ROUTE = 1024
TC = 1024
BLK = 256
ROUTE_ROWS = 128
GROUP_ROW0 = N_EXPERTS
W_CHUNK = 512
RG_UNIT = 2
ROW_UNROLL = 8

PLANE_BITS = 15
TOKEN_MASK = (1 << PLANE_BITS) - 1
PAD_BASE = 2 << PLANE_BITS
N_SLOTS = 3
PRIME_BLOCKS = N_SLOTS
INV_ROW0 = PRIME_BLOCKS * BLK
PAD_SPAN = 4 * BLK
SPARE_ROW0 = PAD_BASE + PAD_SPAN
Y_ROWS = SPARE_ROW0 + PRIME_BLOCKS * BLK


def _dot(a, b):
    return jnp.dot(a, b, preferred_element_type=F32)


def _load_rows(ref, m, idx=()):
    return jnp.concatenate(
        [ref[idx + (pl.ds(s, m, stride=SUBLANES), slice(None))] for s in range(SUBLANES)], axis=1)


def _store_rows(ref, v, idx=()):
    m = v.shape[0]
    for s in range(SUBLANES):
        ref[idx + (pl.ds(s, m, stride=SUBLANES), slice(None))] = v[:, s * LANES:(s + 1) * LANES]


def _tile_of_row(ref, r):
    return ref.at[pl.ds(pl.multiple_of(r * SUBLANES, SUBLANES), SUBLANES)]


def _load_weight_bf16(w_hbm, w_bf, stage, sem):
    nch = w_hbm.shape[1] // W_CHUNK

    def chunk_copy(c):
        return pltpu.make_async_copy(w_hbm.at[:, pl.ds(c * W_CHUNK, W_CHUNK)],
                                     stage.at[c % 2], sem.at[c % 2])

    chunk_copy(0).start()
    for c in range(nch):
        if c + 1 < nch:
            chunk_copy(c + 1).start()
        chunk_copy(c).wait()
        w_bf[:, c * W_CHUNK:(c + 1) * W_CHUNK] = stage[c % 2].astype(BF16)


def _weight_scratch(k, *ns):
    return ([pltpu.VMEM((k, n), BF16) for n in ns]
            + [pltpu.VMEM((2, k, W_CHUNK), F32), pltpu.SemaphoreType.DMA((2,))])


def _layer_norm(r, g, b):
    mu = jnp.mean(r, axis=-1, keepdims=True)
    d = r - mu
    var = jnp.mean(d * d, axis=-1, keepdims=True)
    return d * lax.rsqrt(var + LN_EPS) * g + b


def _shift_rows(v, k, prev8):
    rolled = pltpu.roll(v, k, axis=0)
    rows8 = lax.broadcasted_iota(I32, (SUBLANES, v.shape[1]), 0)
    first = jnp.where(rows8 < k, pltpu.roll(prev8, k, axis=0), rolled[0:SUBLANES])
    return jnp.concatenate([first, rolled[SUBLANES:]], axis=0)


def _softplus(v):
    return jnp.maximum(v, 0.0) + jnp.log1p(jnp.exp(-jnp.abs(v)))


def _sigmoid(z):
    return 0.5 * jnp.tanh(0.5 * z) + 0.5


def _rglru_coeffs(xc, wcat_ref, gab, gxb, lam, blk0=0):
    blk = wcat_ref.shape[1]
    xcb = xc.astype(BF16)
    rs, is_ = [], []
    for n in range(xc.shape[1] // blk):
        o = _dot(xcb[:, n * blk:(n + 1) * blk], wcat_ref[blk0 + n])
        rs.append(o[:, :blk])
        is_.append(o[:, blk:])
    r = _sigmoid(jnp.concatenate(rs, axis=1) + gab)
    i = _sigmoid(jnp.concatenate(is_, axis=1) + gxb)
    neg_log_a = RG_C * r * _softplus(-lam)
    a = jnp.exp(-neg_log_a)
    v = jnp.tanh(neg_log_a) * (a * a + 1.0)
    mult = jnp.where(v > 0.0, v * lax.rsqrt(v), 0.0)
    return a, mult * (i * xc)


def _scan_rows(a, b, h0):
    m, d = a.shape
    groups = m // SUBLANES
    a = a.reshape(groups, SUBLANES, d)
    b = b.reshape(groups, SUBLANES, d)
    sub = lax.broadcasted_iota(I32, a.shape, 1)
    for k in (1, 2, 4):
        keep = sub >= k
        a_sh = jnp.where(keep, pltpu.roll(a, k, axis=1), 1.0)
        b_sh = jnp.where(keep, pltpu.roll(b, k, axis=1), 0.0)
        b = a * b_sh + b
        a = a * a_sh
    outs = []
    h = h0
    for g in range(groups):
        hg = a[g] * h + b[g]
        outs.append(hg)
        h = hg[SUBLANES - 1:SUBLANES]
    return jnp.concatenate(outs, axis=0), h


def _conv_a_prompt_body(x_ref, win_hbm, cw_ref, wout_hbm, g_ref, b_ref, wrt_ref, tri_ref,
                        o_ref, buf_ref, ints_ref, wts_ref, cnt_ref,
                        carry, counts, win_ref, wout_ref, stage, wsem):
    s = pl.program_id(1)

    @pl.when(jnp.logical_and(pl.program_id(0) == 0, s == 0))
    def _():
        _load_weight_bf16(win_hbm, win_ref, stage, wsem)
        _load_weight_bf16(wout_hbm, wout_ref, stage, wsem)
        counts[...] = jnp.zeros_like(counts)

    @pl.when(s == 0)
    def _():
        carry[...] = jnp.zeros_like(carry)

    x = x_ref[0]
    d = x.shape[1]
    bcx = _dot(x.astype(BF16), win_ref[...])
    gb, gc, xh = bcx[:, :d], bcx[:, d:2 * d], bcx[:, 2 * d:]
    u = gc * xh
    prev = carry[...]
    cw = cw_ref[...]
    conv = (cw[0:1] * _shift_rows(u, 2, prev) + cw[1:2] * _shift_rows(u, 1, prev)
            + cw[2:3] * u)
    y = _dot((gb * conv).astype(BF16), wout_ref[...])
    x1 = _layer_norm(ALPHA * x + y, g_ref[...], b_ref[...])
    _store_rows(o_ref, x1)
    _route_tile(x1.astype(BF16), wrt_ref, tri_ref, counts, ints_ref, wts_ref, cnt_ref)
    ts = u.shape[0]
    carry[...] = u[ts - SUBLANES:ts]

    @pl.when(s == pl.num_programs(1) - 1)
    def _():
        buf_ref[0] = u[ts - 2:ts]


def _prefix_matrix(tt):
    return (jnp.arange(tt)[:, None] < jnp.arange(tt)[None, :]).astype(BF16)


def _conv_a_prompt(x, win, cw, wout, g, b, wrt, n_extra):
    bsz, seq, d = x.shape
    ts = min(TS_A, seq)
    grid = (bsz, seq // ts)
    nj = seq // ts
    tp = bsz * seq
    const2 = lambda i, j: (0, 0)
    return pl.pallas_call(
        _conv_a_prompt_body,
        grid=grid,
        in_specs=[
            pl.BlockSpec((1, ts, d), lambda i, j: (i, j, 0)),
            pl.BlockSpec(memory_space=pl.ANY),
            pl.BlockSpec((3, d), const2),
            pl.BlockSpec(memory_space=pl.ANY),
            pl.BlockSpec((1, d), const2),
            pl.BlockSpec((1, d), const2),
            pl.BlockSpec((ROUTE_ROWS, d), const2),
            pl.BlockSpec((ts, ts), const2),
        ],
        out_specs=[
            pl.BlockSpec((ts * SUBLANES, LANES), lambda i, j: (i * nj + j, 0)),
            pl.BlockSpec((1, 2, d), lambda i, j: (i, 0, 0)),
            pl.BlockSpec((4, ts), lambda i, j: (0, i * nj + j)),
            pl.BlockSpec((2, ts), lambda i, j: (0, i * nj + j)),
            pl.BlockSpec((N_EXPERTS, LANES), const2),
        ],
        out_shape=[
            jax.ShapeDtypeStruct(((tp + n_extra) * SUBLANES, LANES), F32),
            jax.ShapeDtypeStruct((bsz, 2, d), F32),
            jax.ShapeDtypeStruct((4, tp), I32),
            jax.ShapeDtypeStruct((2, tp), F32),
            jax.ShapeDtypeStruct((N_EXPERTS, LANES), F32),
        ],
        scratch_shapes=([pltpu.VMEM((SUBLANES, d), F32), pltpu.VMEM((N_EXPERTS, LANES), F32)]
                        + _weight_scratch(d, 3 * d, d)),
        compiler_params=pltpu.CompilerParams(
            dimension_semantics=("arbitrary", "arbitrary"), vmem_limit_bytes=VMEM_LIMIT),
        name="conv_a_prompt",
    )(x, win, cw, wout, g, b, wrt, _prefix_matrix(ts))


def _conv_a_sample_body(x_ref, s0_ref, s1_ref, win_hbm, cw_ref, wout_hbm, g_ref, b_ref, joint_ref,
                        o_ref, u_ref, win_ref, wout_ref, stage, wsem):
    del joint_ref
    _load_weight_bf16(win_hbm, win_ref, stage, wsem)
    _load_weight_bf16(wout_hbm, wout_ref, stage, wsem)
    x = x_ref[...]
    d = x.shape[1]
    bcx = _dot(x.astype(BF16), win_ref[...])
    gb, gc, xh = bcx[:, :d], bcx[:, d:2 * d], bcx[:, 2 * d:]
    u = gc * xh
    cw = cw_ref[...]
    conv = cw[0:1] * s0_ref[...] + cw[1:2] * s1_ref[...] + cw[2:3] * u
    y = _dot((gb * conv).astype(BF16), wout_ref[...])
    _store_rows(o_ref, _layer_norm(ALPHA * x + y, g_ref[...], b_ref[...]))
    u_ref[...] = u


def _whole(a):
    return pl.BlockSpec(a.shape, lambda i: (0,) * a.ndim)


def _conv_a_sample(x, s0, s1, win, cw, wout, g, b, joint):
    n, d = x.shape
    hbm = pl.BlockSpec(memory_space=pl.ANY)
    first_block = joint.shape[0] // (n * SUBLANES) - 1
    return pl.pallas_call(
        _conv_a_sample_body,
        grid=(1,),
        in_specs=[_whole(x), _whole(s0), _whole(s1), hbm, _whole(cw), hbm, _whole(g), _whole(b),
                  hbm],
        out_specs=[pl.BlockSpec((n * SUBLANES, LANES), lambda i: (first_block, 0)),
                   pl.BlockSpec((n, d), lambda i: (0, 0))],
        out_shape=[jax.ShapeDtypeStruct(joint.shape, F32), jax.ShapeDtypeStruct((n, d), F32)],
        scratch_shapes=_weight_scratch(d, 3 * d, d),
        input_output_aliases={8: 0},
        compiler_params=pltpu.CompilerParams(
            dimension_semantics=("arbitrary",), vmem_limit_bytes=VMEM_LIMIT),
        name="conv_a_sample",
    )(x, s0, s1, win, cw, wout, g, b, joint)


def _rglru_prompt_body(x1_ref, ya_ref, yb_ref, w_ref, g2_ref, b2_ref, win_hbm, cw_ref, cb_ref,
                       wcat_ref, gab_ref, gxb_ref, lam_ref, wout_hbm, g_ref, b_ref, wrt_ref, tri_ref,
                       o_ref, buf_ref, hl_ref, ints_ref, wts_ref, cnt_ref,
                       xcarry, hcarry, counts, x_scr, gx_scr, win_ref, wout_ref, stage, wsem,
                       *, nj):
    i = pl.program_id(0)
    t = i - 1
    s = t % nj

    def layer_input():
        ts = x_scr.shape[1]
        w = w_ref[...]
        y = w[:, 0:1] * _load_rows(ya_ref, ts) + w[:, 1:2] * _load_rows(yb_ref, ts)
        return _layer_norm(ALPHA * _load_rows(x1_ref, ts) + y, g2_ref[...], b2_ref[...])

    @pl.when(i == 0)
    def _():
        _load_weight_bf16(win_hbm, win_ref, stage, wsem)
        _load_weight_bf16(wout_hbm, wout_ref, stage, wsem)
        x0 = layer_input()
        x_scr[0] = x0
        gx_scr[0] = _dot(x0.astype(BF16), win_ref[...])
        counts[...] = jnp.zeros_like(counts)

    @pl.when(jnp.logical_and(i > 0, s == 0))
    def _():
        xcarry[...] = jnp.zeros_like(xcarry)
        hcarry[...] = jnp.zeros_like(hcarry)

    for par in range(2):
        @pl.when(jnp.logical_and(i > 0, t % 2 == par))
        def _(par=par):
            x_next = layer_input()
            x_scr[1 - par] = x_next
            xn = x_next.astype(BF16)
            x = x_scr[par]
            ts, d = x.shape
            blk = d // N_RG_BLOCKS
            zs = []
            for u in range(N_RG_BLOCKS // RG_UNIT):
                c0, c1 = u * RG_UNIT * blk, (u + 1) * RG_UNIT * blk
                for half in (0, d):
                    gx_scr[1 - par, :, half + c0:half + c1] = _dot(
                        xn, win_ref[:, half + c0:half + c1])
                gate, xr = gx_scr[par, :, c0:c1], gx_scr[par, :, d + c0:d + c1]
                prev = xcarry[:, c0:c1]
                cw = cw_ref[:, c0:c1]
                xc = (cw[0:1] * _shift_rows(xr, 3, prev) + cw[1:2] * _shift_rows(xr, 2, prev)
                      + cw[2:3] * _shift_rows(xr, 1, prev) + cw[3:4] * xr) + cb_ref[:, c0:c1]
                a, bt = _rglru_coeffs(xc, wcat_ref, gab_ref[:, c0:c1], gxb_ref[:, c0:c1],
                                      lam_ref[:, c0:c1], u * RG_UNIT)
                hs, hlast = _scan_rows(a, bt, hcarry[0:1, c0:c1])
                zs.append((jax.nn.gelu(gate, approximate=True) * hs).astype(BF16))
                xcarry[:, c0:c1] = xr[ts - SUBLANES:ts]
                hcarry[:, c0:c1] = jnp.broadcast_to(hlast, (SUBLANES, c1 - c0))

            y = _dot(jnp.concatenate(zs, axis=1), wout_ref[...])
            x1 = _layer_norm(ALPHA * x + y, g_ref[...], b_ref[...])
            _store_rows(o_ref, x1)
            _route_tile(x1.astype(BF16), wrt_ref, tri_ref, counts, ints_ref, wts_ref, cnt_ref)

    @pl.when(jnp.logical_and(i > 0, s == nj - 1))
    def _():
        buf_ref[0] = xcarry[SUBLANES - 3:SUBLANES]
        hl_ref[0] = hcarry[0:1]


def _rglru_prompt(x_prev, y, w_cols, g2, b2, bsz, seq,
                  win, cw, cb, wcat, gab, gxb, lam, wout, g, b, wrt, n_extra):
    d = g.shape[1]
    ts = min(TS_B, seq)
    nj = seq // ts
    n = bsz * nj
    tp = bsz * seq
    blk = d // N_RG_BLOCKS
    plane = (1 << PLANE_BITS) // ts
    const2 = lambda i: (0, 0)
    nxt = lambda i: jnp.minimum(i, n - 1)
    cur = lambda i: jnp.maximum(i - 1, 0)
    return pl.pallas_call(
        functools.partial(_rglru_prompt_body, nj=nj),
        grid=(n + 1,),
        in_specs=[
            pl.BlockSpec((ts * SUBLANES, LANES), lambda i: (nxt(i), 0)),
            pl.BlockSpec((ts * SUBLANES, LANES), lambda i: (nxt(i), 0)),
            pl.BlockSpec((ts * SUBLANES, LANES), lambda i: (plane + nxt(i), 0)),
            pl.BlockSpec((ts, 2), lambda i: (nxt(i), 0)),
            pl.BlockSpec((1, d), const2),
            pl.BlockSpec((1, d), const2),
            pl.BlockSpec(memory_space=pl.ANY),
            pl.BlockSpec((4, d), const2),
            pl.BlockSpec((1, d), const2),
            pl.BlockSpec((N_RG_BLOCKS, blk, 2 * blk), lambda i: (0, 0, 0)),
            pl.BlockSpec((1, d), const2),
            pl.BlockSpec((1, d), const2),
            pl.BlockSpec((1, d), const2),
            pl.BlockSpec(memory_space=pl.ANY),
            pl.BlockSpec((1, d), const2),
            pl.BlockSpec((1, d), const2),
            pl.BlockSpec((ROUTE_ROWS, d), const2),
            pl.BlockSpec((ts, ts), const2),
        ],
        out_specs=[
            pl.BlockSpec((ts * SUBLANES, LANES), lambda i: (cur(i), 0)),
            pl.BlockSpec((1, 3, d), lambda i: (cur(i) // nj, 0, 0)),
            pl.BlockSpec((1, 1, d), lambda i: (cur(i) // nj, 0, 0)),
            pl.BlockSpec((4, ts), lambda i: (0, cur(i))),
            pl.BlockSpec((2, ts), lambda i: (0, cur(i))),
            pl.BlockSpec((N_EXPERTS, LANES), const2),
        ],
        out_shape=[
            jax.ShapeDtypeStruct(((tp + n_extra) * SUBLANES, LANES), F32),
            jax.ShapeDtypeStruct((bsz, 3, d), F32),
            jax.ShapeDtypeStruct((bsz, 1, d), F32),
            jax.ShapeDtypeStruct((4, tp), I32),
            jax.ShapeDtypeStruct((2, tp), F32),
            jax.ShapeDtypeStruct((N_EXPERTS, LANES), F32),
        ],
        scratch_shapes=([pltpu.VMEM((SUBLANES, d), F32), pltpu.VMEM((SUBLANES, d), F32),
                         pltpu.VMEM((N_EXPERTS, LANES), F32), pltpu.VMEM((2, ts, d), F32),
                         pltpu.VMEM((2, ts, 2 * d), F32)]
                        + _weight_scratch(d, 2 * d, d)),
        compiler_params=pltpu.CompilerParams(
            dimension_semantics=("arbitrary",), vmem_limit_bytes=VMEM_LIMIT),
        name="rglru_prompt",
    )(x_prev, y, y, w_cols, g2, b2, win, cw, cb, wcat, gab, gxb, lam, wout, g, b, wrt,
      _prefix_matrix(ts))


def _rglru_sample_body(x_ref, s0_ref, s1_ref, s2_ref, h0_ref, win_hbm, cw_ref, cb_ref, wcat_ref,
                       gab_ref, gxb_ref, lam_ref, wout_hbm, g_ref, b_ref, joint_ref,
                       o_ref, xr_ref, h_ref, win_ref, wout_ref, stage, wsem):
    del joint_ref
    _load_weight_bf16(win_hbm, win_ref, stage, wsem)
    _load_weight_bf16(wout_hbm, wout_ref, stage, wsem)
    x = x_ref[...]
    d = x.shape[1]
    gx = _dot(x.astype(BF16), win_ref[...])
    gate, xr = gx[:, :d], gx[:, d:]
    cw = cw_ref[...]
    xc = (cw[0:1] * s0_ref[...] + cw[1:2] * s1_ref[...] + cw[2:3] * s2_ref[...]
          + cw[3:4] * xr) + cb_ref[...]
    a, bt = _rglru_coeffs(xc, wcat_ref, gab_ref[...], gxb_ref[...], lam_ref[...])
    h = a * h0_ref[...] + bt
    y = _dot((jax.nn.gelu(gate, approximate=True) * h).astype(BF16), wout_ref[...])
    _store_rows(o_ref, _layer_norm(ALPHA * x + y, g_ref[...], b_ref[...]))
    xr_ref[...] = xr
    h_ref[...] = h


def _rglru_sample(x, s0, s1, s2, h0, win, cw, cb, wcat, gab, gxb, lam, wout, g, b, joint):
    n, d = x.shape
    hbm = pl.BlockSpec(memory_space=pl.ANY)
    first_block = joint.shape[0] // (n * SUBLANES) - 1
    vec = pl.BlockSpec((n, d), lambda i: (0, 0))
    return pl.pallas_call(
        _rglru_sample_body,
        grid=(1,),
        in_specs=[_whole(x), _whole(s0), _whole(s1), _whole(s2), _whole(h0), hbm, _whole(cw),
                  _whole(cb), _whole(wcat), _whole(gab), _whole(gxb), _whole(lam), hbm,
                  _whole(g), _whole(b), hbm],
        out_specs=[pl.BlockSpec((n * SUBLANES, LANES), lambda i: (first_block, 0)), vec, vec],
        out_shape=[jax.ShapeDtypeStruct(joint.shape, F32),
                   jax.ShapeDtypeStruct((n, d), F32), jax.ShapeDtypeStruct((n, d), F32)],
        scratch_shapes=_weight_scratch(d, 2 * d, d),
        input_output_aliases={15: 0},
        compiler_params=pltpu.CompilerParams(
            dimension_semantics=("arbitrary",), vmem_limit_bytes=VMEM_LIMIT),
        name="rglru_sample",
    )(x, s0, s1, s2, h0, win, cw, cb, wcat, gab, gxb, lam, wout, g, b, joint)


def _first_argmax(v, rows):
    m = jnp.max(v, axis=0, keepdims=True)
    idx = jnp.min(jnp.where(v == m, rows, v.shape[0]), axis=0, keepdims=True)
    return m, idx


def _route_body(x_ref, wrt_ref, tri_ref, cin_ref, ints_ref, wts_ref, cnt_ref, carry):
    @pl.when(pl.program_id(0) == 0)
    def _():
        carry[...] = cin_ref[...]

    tt = x_ref.shape[0] // SUBLANES
    _route_tile(_load_rows(x_ref, tt).astype(BF16), wrt_ref, tri_ref, carry, ints_ref, wts_ref,
                cnt_ref)


def _route_tile(xb, wrt_ref, tri_ref, carry, ints_ref, wts_ref, cnt_ref):
    tt = xb.shape[0]
    lt = lax.dot_general(wrt_ref[...], xb, (((1,), (1,)), ((), ())), preferred_element_type=F32)
    rows8 = lax.broadcasted_iota(I32, (SUBLANES, tt), 0)
    neg_inf = jnp.float32(-jnp.inf)

    gl = jnp.where(rows8 < N_GROUPS, lt[GROUP_ROW0:GROUP_ROW0 + SUBLANES], neg_inf)
    gmax, gidx = _first_argmax(gl, rows8)
    gw = 1.0 / jnp.sum(jnp.exp(gl - gmax), axis=0, keepdims=True)

    el = lt[0:EXP_PER_GROUP]
    for g in range(1, N_GROUPS):
        el = jnp.where(gidx == g, lt[g * EXP_PER_GROUP:(g + 1) * EXP_PER_GROUP], el)
    emax, i1 = _first_argmax(el, rows8)
    el2 = jnp.where(rows8 == i1, neg_inf, el)
    m2, i2 = _first_argmax(el2, rows8)
    psum = jnp.sum(jnp.exp(el - emax), axis=0, keepdims=True)
    ep1 = 1.0 / psum
    ep2 = jnp.exp(m2 - emax) / psum
    tot = ep1 + ep2
    wa = gw * (ep1 / tot)
    wb = gw * (ep2 / tot)
    ea = gidx * EXP_PER_GROUP + i1
    eb = gidx * EXP_PER_GROUP + i2

    rows_e = lax.broadcasted_iota(I32, (N_EXPERTS, tt), 0)
    oha = rows_e == ea
    ohb = rows_e == eb
    oh = jnp.where(oha | ohb, 1.0, 0.0)
    base = carry[...][:, 0:1]
    excl = _dot(oh.astype(BF16), tri_ref[...]) + base
    ra = jnp.sum(jnp.where(oha, excl, 0.0), axis=0, keepdims=True)
    rb = jnp.sum(jnp.where(ohb, excl, 0.0), axis=0, keepdims=True)
    new = carry[...] + jnp.sum(oh, axis=1, keepdims=True)
    carry[...] = new
    cnt_ref[...] = new

    ints_ref[0:1, :] = ea
    ints_ref[1:2, :] = eb
    ints_ref[2:3, :] = ra.astype(I32)
    ints_ref[3:4, :] = rb.astype(I32)
    wts_ref[0:1, :] = wa
    wts_ref[1:2, :] = wb


def _route(x, wrt, cin, tok0, t):
    d = wrt.shape[1]
    tt = min(TT_ROUTE, t)
    blk0 = tok0 // tt
    tri = _prefix_matrix(tt)
    return pl.pallas_call(
        _route_body,
        grid=(t // tt,),
        in_specs=[
            pl.BlockSpec((tt * SUBLANES, LANES), lambda i: (blk0 + i, 0)),
            pl.BlockSpec((ROUTE_ROWS, d), lambda i: (0, 0)),
            pl.BlockSpec((tt, tt), lambda i: (0, 0)),
            pl.BlockSpec((N_EXPERTS, LANES), lambda i: (0, 0)),
        ],
        out_specs=[
            pl.BlockSpec((4, tt), lambda i: (0, i)),
            pl.BlockSpec((2, tt), lambda i: (0, i)),
            pl.BlockSpec((N_EXPERTS, LANES), lambda i: (0, 0)),
        ],
        out_shape=[
            jax.ShapeDtypeStruct((4, t), I32),
            jax.ShapeDtypeStruct((2, t), F32),
            jax.ShapeDtypeStruct((N_EXPERTS, LANES), F32),
        ],
        scratch_shapes=[pltpu.VMEM((N_EXPERTS, LANES), F32)],
        compiler_params=pltpu.CompilerParams(
            dimension_semantics=("arbitrary",), vmem_limit_bytes=VMEM_LIMIT),
        name="route",
    )(x, wrt, tri, cin)


def _dest_body(ints_ref, pst_ref, dest_ref):
    ints = ints_ref[...]
    tt = ints.shape[1]
    rows_e = lax.broadcasted_iota(I32, (N_EXPERTS, tt), 0)
    pst = pst_ref[...][:, 0:1]
    for k in range(2):
        start = jnp.sum(jnp.where(rows_e == ints[k:k + 1], pst, 0.0), axis=0, keepdims=True)
        dest_ref[k:k + 1, :] = start.astype(I32) + ints[2 + k:3 + k]


def _dest(ints, pstart_f):
    t = ints.shape[1]
    return pl.pallas_call(
        _dest_body,
        out_shape=jax.ShapeDtypeStruct((2, t), I32),
        compiler_params=pltpu.CompilerParams(vmem_limit_bytes=VMEM_LIMIT),
        name="dest",
    )(ints, pstart_f)


def _invert_body(dest_ref, padpos_ref, inv_ref, *, t_total):
    def prime_body(i, c):
        inv_ref[i] = SPARE_ROW0 + i
        return c

    lax.fori_loop(0, INV_ROW0, prime_body, 0, unroll=8)

    def pad_body(e, c):
        q0 = padpos_ref[e]
        for r in range(BLK):
            inv_ref[q0 + r] = PAD_BASE + ((q0 + r) & (PAD_SPAN - 1))
        return c

    lax.fori_loop(0, N_EXPERTS, pad_body, 0)

    def tok_body(j, c):
        toks = [j * ROW_UNROLL + u for u in range(ROW_UNROLL)]
        rows = [[dest_ref[k * t_total + t] for k in range(2)] for t in toks]
        for t, qs in zip(toks, rows):
            for k in range(2):
                inv_ref[qs[k]] = t + (k << PLANE_BITS)
        return c

    lax.fori_loop(0, t_total // ROW_UNROLL, tok_body, 0)


def _invert(dest_flat, pad_pos, p_rows):
    t_total = dest_flat.shape[0] // 2
    grid_spec = pltpu.PrefetchScalarGridSpec(
        num_scalar_prefetch=2,
        grid=(1,),
        in_specs=[],
        out_specs=pl.BlockSpec(memory_space=pltpu.SMEM),
    )
    return pl.pallas_call(
        functools.partial(_invert_body, t_total=t_total),
        grid_spec=grid_spec,
        out_shape=jax.ShapeDtypeStruct((INV_ROW0 + p_rows + BLK,), I32),
        compiler_params=pltpu.CompilerParams(dimension_semantics=("arbitrary",)),
        name="invert",
    )(dest_flat, pad_pos)


def _experts_body(be_ref, nu_ref, nblk_ref, inv_ref, x_hbm, wg_hbm, wu_hbm, wd_hbm, y_hbm,
                  xbuf, obuf, sg, su, sd, wg_ref, wu_ref, wd_ref, wsem, rsem, slot_ref,
                  *, layer):
    b = pl.program_id(0)
    nu = nu_ref[0]
    e = be_ref[jnp.minimum(b, nu - 1)]

    def gather(blk, slot, fn):
        vs = [inv_ref[(blk + PRIME_BLOCKS) * BLK + r] for r in range(BLK)]
        for r, v in enumerate(vs):
            fn(pltpu.make_async_copy(_tile_of_row(x_hbm, v & TOKEN_MASK),
                                     _tile_of_row(xbuf.at[slot], r), rsem.at[slot]), 0)

    def scatter(blk, slot, fn):
        vs = [inv_ref[(blk + PRIME_BLOCKS) * BLK + r] for r in range(BLK)]
        for r, v in enumerate(vs):
            fn(pltpu.make_async_copy(_tile_of_row(obuf.at[slot], r),
                                     _tile_of_row(y_hbm, v), rsem.at[slot]), 1)

    def start(cp, priority):
        cp.start(priority=priority)

    def wait(cp, priority):
        cp.wait()

    def fetch(ex, slot):
        return (pltpu.make_async_copy(wg_hbm.at[layer, ex], sg.at[slot], wsem.at[slot, 0]),
                pltpu.make_async_copy(wu_hbm.at[layer, ex], su.at[slot], wsem.at[slot, 1]),
                pltpu.make_async_copy(wd_hbm.at[layer, ex], sd.at[slot], wsem.at[slot, 2]))

    @pl.when(b == 0)
    def _():
        slot_ref[0] = 0
        for cp in fetch(e, 0):
            cp.start()
        obuf[...] = jnp.zeros_like(obuf)
        scatter(-3, 0, start)
        scatter(-2, 1, start)
        gather(0, 0, start)
        gather(jnp.minimum(1, nu - 1), 1, start)

    first_of_expert = jnp.logical_or(b == 0, e != be_ref[jnp.maximum(b - 1, 0)])

    @pl.when(jnp.logical_and(b < nu, first_of_expert))
    def _():
        slot = slot_ref[0]
        nxt = b + nblk_ref[e]

        @pl.when(nxt < nu)
        def _():
            for cp in fetch(be_ref[nxt], 1 - slot):
                cp.start()

        for cp in fetch(e, slot):
            cp.wait()
        wg_ref[...] = sg[slot].astype(BF16)
        wu_ref[...] = su[slot].astype(BF16)
        wd_ref[...] = sd[slot].astype(BF16)
        slot_ref[0] = 1 - slot

    for slot in range(N_SLOTS):
        prev, nxt = (slot - 1) % N_SLOTS, (slot + 1) % N_SLOTS

        @pl.when(jnp.logical_and(b < nu, b % N_SLOTS == slot))
        def _(slot=slot, prev=prev):
            gather(b, slot, wait)
            scatter(b - 3, slot, wait)
            xb = _load_rows(xbuf, BLK, (slot,)).astype(BF16)
            scatter(b - 1, prev, start)
            gather(jnp.minimum(b + 2, nu - 1), prev, start)
            h = jax.nn.silu(_dot(xb, wg_ref[...])) * _dot(xb, wu_ref[...])
            o = _dot(h.astype(BF16), wd_ref[...])
            _store_rows(obuf, o, (slot,))

        @pl.when(jnp.logical_and(b == nu, b % N_SLOTS == slot))
        def _(slot=slot, prev=prev, nxt=nxt):
            scatter(b - 1, prev, start)
            scatter(b - 3, slot, wait)
            scatter(b - 2, nxt, wait)
            scatter(b - 1, prev, wait)
            gather(nu - 1, slot, wait)
            gather(nu - 1, nxt, wait)


def _experts(blk_e, n_used, nblk, inv, x, wg, wu, wd, layer):
    d, de = wg.shape[2], wg.shape[3]
    nb = blk_e.shape[0]
    hbm = pl.BlockSpec(memory_space=pl.ANY)
    grid_spec = pltpu.PrefetchScalarGridSpec(
        num_scalar_prefetch=4,
        grid=(nb + 1,),
        in_specs=[hbm, hbm, hbm, hbm],
        out_specs=hbm,
        scratch_shapes=[
            pltpu.VMEM((N_SLOTS, BLK * SUBLANES, LANES), F32),
            pltpu.VMEM((N_SLOTS, BLK * SUBLANES, LANES), F32),
            pltpu.VMEM((2, d, de), F32), pltpu.VMEM((2, d, de), F32), pltpu.VMEM((2, de, d), F32),
            pltpu.VMEM((d, de), BF16), pltpu.VMEM((d, de), BF16), pltpu.VMEM((de, d), BF16),
            pltpu.SemaphoreType.DMA((2, 3)), pltpu.SemaphoreType.DMA((N_SLOTS,)),
            pltpu.SMEM((1,), I32),
        ],
    )
    return pl.pallas_call(
        functools.partial(_experts_body, layer=layer),
        grid_spec=grid_spec,
        out_shape=jax.ShapeDtypeStruct((Y_ROWS * SUBLANES, LANES), F32),
        compiler_params=pltpu.CompilerParams(
            dimension_semantics=("arbitrary",), vmem_limit_bytes=VMEM_LIMIT),
        name="experts",
    )(blk_e, n_used, nblk, inv, x, wg, wu, wd)


def _combine_body(x_ref, ya_ref, yb_ref, w_ref, g_ref, b_ref, o_ref):
    tc = o_ref.shape[0]
    w = w_ref[...]
    y = w[:, 0:1] * _load_rows(ya_ref, tc) + w[:, 1:2] * _load_rows(yb_ref, tc)
    o_ref[...] = _layer_norm(ALPHA * _load_rows(x_ref, tc) + y, g_ref[...], b_ref[...])


def _combine(x, y, w_cols, g, b, tok0, t):
    d = g.shape[1]
    tc = min(TC, t)
    blk0 = tok0 // tc
    plane = (1 << PLANE_BITS) // tc
    row_tiled = lambda first: pl.BlockSpec((tc * SUBLANES, LANES), lambda i: (first + i, 0))
    return pl.pallas_call(
        _combine_body,
        grid=(t // tc,),
        in_specs=[
            row_tiled(blk0), row_tiled(blk0), row_tiled(plane + blk0),
            pl.BlockSpec((tc, 2), lambda i: (blk0 + i, 0)),
            pl.BlockSpec((1, d), lambda i: (0, 0)),
            pl.BlockSpec((1, d), lambda i: (0, 0)),
        ],
        out_specs=pl.BlockSpec((tc, d), lambda i: (i, 0)),
        out_shape=jax.ShapeDtypeStruct((t, d), F32),
        compiler_params=pltpu.CompilerParams(
            dimension_semantics=("arbitrary",), vmem_limit_bytes=VMEM_LIMIT),
        name="combine",
    )(x, y, y, w_cols, g, b)


def _router_weight(w_group, w_expert):
    d = w_group.shape[0]
    wrt = jnp.zeros((ROUTE_ROWS, d), F32)
    wrt = wrt.at[0:N_EXPERTS].set(w_expert.T).at[GROUP_ROW0:GROUP_ROW0 + N_GROUPS].set(w_group.T)
    return wrt.astype(BF16)


def _moe_experts(x, tp, ts, routed_p, wrt, wg, wu, wd, layer):
    t_total = tp + ts
    assert PAD_SPAN <= t_total <= 1 << PLANE_BITS and tp % ts == 0
    ints_p, wts_p, cnt_p = routed_p
    ints_s, wts_s, cnt = _route(x, wrt, cnt_p, tp, ts)
    ints = jnp.concatenate([ints_p, ints_s], axis=1)
    wts = jnp.concatenate([wts_p, wts_s], axis=1)

    counts = cnt[:, 0].astype(I32)
    pcounts = (counts + BLK - 1) // BLK * BLK
    pend = jnp.cumsum(pcounts)
    pstart = pend - pcounts
    nb = (2 * t_total + N_EXPERTS * (BLK - 1) + BLK - 1) // BLK
    p_rows = nb * BLK
    n_used = (pend[-1] // BLK).astype(I32).reshape(1)
    blk_first = jnp.minimum(jnp.arange(nb, dtype=I32), n_used[0] - 1) * BLK
    blk_e = jnp.sum((pend[None, :] <= blk_first[:, None]).astype(I32), axis=1)
    nblk = pcounts // BLK

    pstart_f = jnp.broadcast_to((pstart + INV_ROW0).astype(F32)[:, None], (N_EXPERTS, LANES))
    dest_flat = _dest(ints, pstart_f).reshape(2 * t_total)

    inv = _invert(dest_flat, pstart + counts + INV_ROW0, p_rows)
    return _experts(blk_e, n_used, nblk, inv, x, wg, wu, wd, layer), wts.T


def kernel(x_prompt, x_sample, state_conv_a, state_conv_b, state_h, a_w_in, a_conv_w, a_w_out,
           b_w_in, b_conv_w, b_conv_b, b_gate_a_w, b_gate_a_b, b_gate_x_w, b_gate_x_b, b_lambda,
           b_w_out, ln1_g, ln1_b, ln2_g, ln2_b, moe_w_group, moe_w_expert, moe_w_gate, moe_w_up,
           moe_w_down):
    bsz, seq, d = x_prompt.shape
    n_s = x_sample.shape[0]
    row = lambda v: v.reshape(1, d)

    tp = bsz * seq
    win, wout = a_w_in[0], a_w_out[0]
    wrt = _router_weight(moe_w_group[0], moe_w_expert[0])
    x1, conv_a_p, *routed = _conv_a_prompt(x_prompt, win, a_conv_w[0], wout, row(ln1_g[0]),
                                           row(ln1_b[0]), wrt, n_s)
    sa = state_conv_a[0]
    x1, u_s = _conv_a_sample(x_sample.reshape(n_s, d), sa[:, 0], sa[:, 1], win, a_conv_w[0], wout,
                             row(ln1_g[0]), row(ln1_b[0]), x1)
    conv_a_s = jnp.stack([sa[:, 1], u_s], axis=1)

    y, w_cols = _moe_experts(x1, tp, n_s, routed, wrt, moe_w_gate, moe_w_up, moe_w_down, 0)
    g2, b2 = row(ln2_g[0]), row(ln2_b[0])
    xs = _combine(x1, y, w_cols, g2, b2, tp, n_s)

    win, wout = b_w_in[0], b_w_out[0]
    wcat = jnp.concatenate([b_gate_a_w[0], b_gate_x_w[0]], axis=-1).astype(BF16)
    args = (win, b_conv_w[0], row(b_conv_b[0]), wcat, row(b_gate_a_b[0]), row(b_gate_x_b[0]),
            row(b_lambda[0]), wout, row(ln1_g[1]), row(ln1_b[1]))
    wrt = _router_weight(moe_w_group[1], moe_w_expert[1])
    x1, conv_b_p, h_p, *routed = _rglru_prompt(x1, y, w_cols, g2, b2, bsz, seq, *args, wrt, n_s)
    sb = state_conv_b[0]
    x1, xr_s, h_s = _rglru_sample(xs, sb[:, 0], sb[:, 1], sb[:, 2], state_h[0], *args, x1)
    conv_b_s = jnp.stack([sb[:, 1], sb[:, 2], xr_s], axis=1)

    y, w_cols = _moe_experts(x1, tp, n_s, routed, wrt, moe_w_gate, moe_w_up, moe_w_down, 1)
    g2, b2 = row(ln2_g[1]), row(ln2_b[1])
    xp = _combine(x1, y, w_cols, g2, b2, 0, tp)
    xs = _combine(x1, y, w_cols, g2, b2, tp, n_s)

    return (xp.reshape(bsz, seq, d), xs.reshape(n_s, 1, d),
            conv_a_p[None], conv_a_s[None], conv_b_p[None], conv_b_s[None],
            h_p.reshape(1, bsz, d), h_s[None])
```
